```python
import math
import jax, jax.numpy as jnp
from jax import lax
import numpy as np

D_MODEL = 1024
BATCH = 8
SEQ = 2048
DEPTH = 4
DEC_BATCH = 128
DEC_SEQ = 8
PAST_LEN = 8192
PAGE_SIZE = 128

N_MIXERS = 2
N_MLA_LAYERS = (DEPTH + N_MIXERS - 1) // N_MIXERS
N_SSM_LAYERS = DEPTH // N_MIXERS
N_HEADS = 16
D_NOPE = 64
D_ROPE = 32
D_V = 64
Q_LORA = 512
KV_LORA = 256
ROPE_THETA = 10000.0
Q_BLOCK = 128
D_IN_MLA = Q_LORA + KV_LORA + D_ROPE
GROUP_SIZE = 16
N_GROUPS = D_MODEL // GROUP_SIZE
STATE_DIM = 64
DT_MIN = 1e-3
DT_MAX = 1e-1
D_FF = 4 * D_MODEL
EPS = 1e-6

kernel_name = "hybrid_mla_s5_decoder_step"


def _rmsnorm(x, g):
    xf = x.astype(jnp.float32)
    y = xf * lax.rsqrt(jnp.mean(xf * xf, axis=-1, keepdims=True) + EPS)
    return (y * g.astype(jnp.float32)).astype(x.dtype)


def _rope_tables(pos):
    half = D_ROPE // 2
    inv = ROPE_THETA ** (-jnp.arange(half, dtype=jnp.float32) / half)
    ang = pos.astype(jnp.float32)[:, None] * inv[None, :]
    return jnp.cos(ang), jnp.sin(ang)


def _apply_rope(x, cos, sin):
    half = D_ROPE // 2
    xf = x.astype(jnp.float32)
    x1, x2 = xf[..., :half], xf[..., half:]
    return jnp.concatenate([x1 * cos - x2 * sin, x2 * cos + x1 * sin], axis=-1).astype(x.dtype)


def _mla_project(h, pos, w_in, g_q, g_kv, w_q_up):
    B, L, _ = h.shape
    proj = h @ w_in
    c_q = _rmsnorm(proj[..., :Q_LORA], g_q)
    c_kv = _rmsnorm(proj[..., Q_LORA:Q_LORA + KV_LORA], g_kv)
    k_pe = proj[..., Q_LORA + KV_LORA:]
    q = (c_q @ w_q_up).reshape(B, L, N_HEADS, D_NOPE + D_ROPE)
    cos, sin = _rope_tables(pos)
    q_nope = q[..., :D_NOPE]
    q_pe = _apply_rope(q[..., D_NOPE:], cos[:, None, :], sin[:, None, :])
    k_pe = _apply_rope(k_pe, cos, sin)
    return q_nope, q_pe, c_kv, k_pe


def _mla_prompt(h, w_in, g_q, g_kv, w_q_up, w_kv_up, w_o):
    B, L, _ = h.shape
    pos = jnp.arange(L, dtype=jnp.int32)
    q_nope, q_pe, c_kv, k_pe = _mla_project(h, pos, w_in, g_q, g_kv, w_q_up)
    kv = (c_kv @ w_kv_up).reshape(B, L, N_HEADS, D_NOPE + D_V)
    k_nope, v = kv[..., :D_NOPE], kv[..., D_NOPE:]
    scale = (D_NOPE + D_ROPE) ** -0.5
    outs = []
    for blk in range(L // Q_BLOCK):
        q0, q1 = blk * Q_BLOCK, (blk + 1) * Q_BLOCK
        s = (jnp.einsum('bqhn,bkhn->bhqk', q_nope[:, q0:q1], k_nope[:, :q1])
             + jnp.einsum('bqhd,bkd->bhqk', q_pe[:, q0:q1], k_pe[:, :q1])).astype(jnp.float32) * scale
        mask = pos[q0:q1, None] >= pos[None, :q1]
        p = jax.nn.softmax(jnp.where(mask, s, -jnp.inf), axis=-1).astype(v.dtype)
        outs.append(jnp.einsum('bhqk,bkhv->bqhv', p, v[:, :q1]))
    o = jnp.concatenate(outs, axis=1).reshape(B, L, N_HEADS * D_V)
    return o @ w_o, c_kv, k_pe


def _mla_sample(h, ckv_pool, kpe_pool, page_table, w_in, g_q, g_kv, w_q_up, w_kv_up, w_o):
    B, L, _ = h.shape
    past = page_table.shape[1] * PAGE_SIZE
    pos = past + jnp.arange(L, dtype=jnp.int32)
    q_nope, q_pe, c_kv, k_pe = _mla_project(h, pos, w_in, g_q, g_kv, w_q_up)
    w_kv = w_kv_up.reshape(KV_LORA, N_HEADS, D_NOPE + D_V)
    w_uk, w_uv = w_kv[..., :D_NOPE], w_kv[..., D_NOPE:]
    q_lat = jnp.einsum('bqhn,rhn->bqhr', q_nope, w_uk)
    ckv_past = ckv_pool[page_table].reshape(B, past, KV_LORA)
    kpe_past = kpe_pool[page_table].reshape(B, past, D_ROPE)
    scale = (D_NOPE + D_ROPE) ** -0.5
    s_past = (jnp.einsum('bqhr,bkr->bhqk', q_lat, ckv_past)
              + jnp.einsum('bqhd,bkd->bhqk', q_pe, kpe_past)).astype(jnp.float32) * scale
    s_new = (jnp.einsum('bqhr,bkr->bhqk', q_lat, c_kv)
             + jnp.einsum('bqhd,bkd->bhqk', q_pe, k_pe)).astype(jnp.float32) * scale
    causal = jnp.tril(jnp.ones((L, L), dtype=bool))
    s = jnp.concatenate([s_past, jnp.where(causal, s_new, -jnp.inf)], axis=-1)
    p = jax.nn.softmax(s, axis=-1).astype(c_kv.dtype)
    o_lat = (jnp.einsum('bhqk,bkr->bqhr', p[..., :past], ckv_past)
             + jnp.einsum('bhqk,bkr->bqhr', p[..., past:], c_kv))
    o = jnp.einsum('bqhr,rhv->bqhv', o_lat, w_uv).reshape(B, L, N_HEADS * D_V)
    return o @ w_o, c_kv, k_pe


def _s5(h, h0_re, h0_im, a_re, a_im, log_dt, b_re, b_im, c_re, c_im, d, w_glu):
    B, L, D = h.shape
    a = lax.complex(a_re.astype(jnp.float32), a_im.astype(jnp.float32))
    dta = jnp.exp(log_dt.astype(jnp.float32))[:, None] * a
    a_bar = jnp.exp(dta)
    b = lax.complex(b_re.astype(jnp.float32), b_im.astype(jnp.float32))
    b_bar = ((a_bar - 1.0) / a)[..., None] * b
    u = h.astype(jnp.float32).reshape(B, L, N_GROUPS, GROUP_SIZE)
    bu = jnp.einsum('blgs,gps->blgp', u.astype(jnp.complex64), b_bar)
    a_seq = jnp.broadcast_to(a_bar, bu.shape)

    def combine(e1, e2):
        a1, x1 = e1
        a2, x2 = e2
        return a2 * a1, a2 * x1 + x2

    _, states = lax.associative_scan(combine, (a_seq, bu), axis=1)
    if h0_re is not None:
        h0 = lax.complex(h0_re.astype(jnp.float32), h0_im.astype(jnp.float32))
        t = jnp.arange(1, L + 1, dtype=jnp.float32)
        decay = jnp.exp(t[:, None, None] * dta)
        states = states + decay[None] * h0[:, None]
    c = lax.complex(c_re.astype(jnp.float32), c_im.astype(jnp.float32))
    y = jnp.real(jnp.einsum('gsp,blgp->blgs', c, states)).reshape(B, L, D)
    y = (y + d.astype(jnp.float32) * h.astype(jnp.float32)).astype(h.dtype)
    g = jax.nn.gelu(y)
    ab = g @ w_glu
    out = ab[..., :D] * jax.nn.sigmoid(ab[..., D:])
    final = states[:, -1]
    return out, jnp.real(final), jnp.imag(final)


def _mlp(h, w1, w2):
    z = jax.nn.relu(h @ w1)
    return (z * z) @ w2


def setup_inputs(seed: int = 0) -> dict:
    key = jax.random.key(seed)
    ks = jax.random.split(key, 32)
    f32 = jnp.float32
    n_pages = PAST_LEN // PAGE_SIZE
    n_used = DEC_BATCH * n_pages
    n_phys = n_used + n_used // 4
    nrm = lambda k, shape, s: jax.random.normal(k, shape, f32) * s
    perm = jax.random.permutation(ks[0], n_phys)
    page_table = perm[:n_used].reshape(DEC_BATCH, n_pages).astype(jnp.int32)
    n_idx = jnp.arange(STATE_DIM, dtype=f32)
    return {
        "x_prompt": nrm(ks[1], (BATCH, SEQ, D_MODEL), 1.0),
        "x_sample": nrm(ks[2], (DEC_BATCH, DEC_SEQ, D_MODEL), 1.0),
        "cache_ckv": nrm(ks[3], (N_MLA_LAYERS, n_phys, PAGE_SIZE, KV_LORA), 1.0),
        "cache_kpe": nrm(ks[4], (N_MLA_LAYERS, n_phys, PAGE_SIZE, D_ROPE), 1.0),
        "state_ssm_re": nrm(ks[5], (N_SSM_LAYERS, DEC_BATCH, N_GROUPS, STATE_DIM), 0.5),
        "state_ssm_im": nrm(ks[6], (N_SSM_LAYERS, DEC_BATCH, N_GROUPS, STATE_DIM), 0.5),
        "page_table": page_table,
        "norm_mix": 1.0 + nrm(ks[7], (DEPTH, D_MODEL), 0.02),
        "norm_mlp": 1.0 + nrm(ks[8], (DEPTH, D_MODEL), 0.02),
        "norm_final": 1.0 + nrm(ks[9], (D_MODEL,), 0.02),
        "mla_w_in": nrm(ks[10], (N_MLA_LAYERS, D_MODEL, D_IN_MLA), D_MODEL ** -0.5),
        "mla_q_norm": 1.0 + nrm(ks[11], (N_MLA_LAYERS, Q_LORA), 0.02),
        "mla_kv_norm": 1.0 + nrm(ks[12], (N_MLA_LAYERS, KV_LORA), 0.02),
        "mla_w_q_up": nrm(ks[13], (N_MLA_LAYERS, Q_LORA, N_HEADS * (D_NOPE + D_ROPE)), Q_LORA ** -0.5),
        "mla_w_kv_up": nrm(ks[14], (N_MLA_LAYERS, KV_LORA, N_HEADS * (D_NOPE + D_V)), KV_LORA ** -0.5),
        "mla_w_o": nrm(ks[15], (N_MLA_LAYERS, N_HEADS * D_V, D_MODEL), (N_HEADS * D_V) ** -0.5),
        "ssm_a_re": -0.5 + nrm(ks[16], (N_SSM_LAYERS, N_GROUPS, STATE_DIM), 0.01),
        "ssm_a_im": math.pi * n_idx + nrm(ks[17], (N_SSM_LAYERS, N_GROUPS, STATE_DIM), 0.01),
        "ssm_log_dt": jax.random.uniform(ks[18], (N_SSM_LAYERS, N_GROUPS), f32, math.log(DT_MIN), math.log(DT_MAX)),
        "ssm_b_re": nrm(ks[19], (N_SSM_LAYERS, N_GROUPS, STATE_DIM, GROUP_SIZE), (2 * GROUP_SIZE) ** -0.5),
        "ssm_b_im": nrm(ks[20], (N_SSM_LAYERS, N_GROUPS, STATE_DIM, GROUP_SIZE), (2 * GROUP_SIZE) ** -0.5),
        "ssm_c_re": nrm(ks[21], (N_SSM_LAYERS, N_GROUPS, GROUP_SIZE, STATE_DIM), (2 * STATE_DIM) ** -0.5),
        "ssm_c_im": nrm(ks[22], (N_SSM_LAYERS, N_GROUPS, GROUP_SIZE, STATE_DIM), (2 * STATE_DIM) ** -0.5),
        "ssm_d": nrm(ks[23], (N_SSM_LAYERS, D_MODEL), 1.0),
        "ssm_w_glu": nrm(ks[24], (N_SSM_LAYERS, D_MODEL, 2 * D_MODEL), D_MODEL ** -0.5),
        "mlp_w1": nrm(ks[25], (DEPTH, D_MODEL, D_FF), D_MODEL ** -0.5),
        "mlp_w2": nrm(ks[26], (DEPTH, D_FF, D_MODEL), D_FF ** -0.5),
    }


def reference(x_prompt, x_sample, cache_ckv, cache_kpe, state_ssm_re, state_ssm_im, page_table,
              norm_mix, norm_mlp, norm_final, mla_w_in, mla_q_norm, mla_kv_norm, mla_w_q_up,
              mla_w_kv_up, mla_w_o, ssm_a_re, ssm_a_im, ssm_log_dt, ssm_b_re, ssm_b_im, ssm_c_re,
              ssm_c_im, ssm_d, ssm_w_glu, mlp_w1, mlp_w2):
    xp, xs = x_prompt, x_sample
    p_ckv, p_kpe, p_re, p_im = [], [], [], []
    s_ckv, s_kpe, s_re, s_im = [], [], [], []
    for i in range(DEPTH):
        j = i // N_MIXERS
        hp = _rmsnorm(xp, norm_mix[i])
        hs = _rmsnorm(xs, norm_mix[i])
        if i % N_MIXERS == 0:
            w = (mla_w_in[j], mla_q_norm[j], mla_kv_norm[j], mla_w_q_up[j], mla_w_kv_up[j], mla_w_o[j])
            op, ckv_p, kpe_p = _mla_prompt(hp, *w)
            osm, ckv_s, kpe_s = _mla_sample(hs, cache_ckv[j], cache_kpe[j], page_table, *w)
            p_ckv.append(ckv_p); p_kpe.append(kpe_p)
            s_ckv.append(ckv_s); s_kpe.append(kpe_s)
        else:
            w = (ssm_a_re[j], ssm_a_im[j], ssm_log_dt[j], ssm_b_re[j], ssm_b_im[j],
                 ssm_c_re[j], ssm_c_im[j], ssm_d[j], ssm_w_glu[j])
            op, re_p, im_p = _s5(hp, None, None, *w)
            osm, re_s, im_s = _s5(hs, state_ssm_re[j], state_ssm_im[j], *w)
            p_re.append(re_p); p_im.append(im_p)
            s_re.append(re_s); s_im.append(im_s)
        xp = xp + op
        xs = xs + osm
        xp = xp + _mlp(_rmsnorm(xp, norm_mlp[i]), mlp_w1[i], mlp_w2[i])
        xs = xs + _mlp(_rmsnorm(xs, norm_mlp[i]), mlp_w1[i], mlp_w2[i])
    y_prompt = _rmsnorm(xp, norm_final)
    y_sample = _rmsnorm(xs, norm_final)
    return (y_prompt, y_sample,
            jnp.stack(p_ckv), jnp.stack(p_kpe), jnp.stack(p_re), jnp.stack(p_im),
            jnp.stack(s_ckv), jnp.stack(s_kpe), jnp.stack(s_re), jnp.stack(s_im))
```

```python
import functools
import math

import jax
import jax.numpy as jnp
from jax import lax
from jax.experimental import pallas as pl
from jax.experimental.pallas import tpu as pltpu

F32 = jnp.float32
BF16 = jnp.bfloat16

D_MODEL = 1024
BATCH = 8
SEQ = 2048
DEPTH = 4
DEC_BATCH = 128
DEC_SEQ = 8
PAST_LEN = 8192
PAGE_SIZE = 128
N_PAGES = PAST_LEN // PAGE_SIZE
N_HEADS = 16
D_NOPE = 64
D_ROPE = 32
D_V = 64
Q_LORA = 512
KV_LORA = 256
ROPE_THETA = 10000.0
GROUP_SIZE = 16
N_GROUPS = D_MODEL // GROUP_SIZE
STATE_DIM = 64
D_FF = 4 * D_MODEL
EPS = 1e-6
SCALE = (D_NOPE + D_ROPE) ** -0.5

NP_ROWS = BATCH * SEQ
NS_ROWS = DEC_BATCH * DEC_SEQ
ROWS = NP_ROWS + NS_ROWS

LANES = 128
HEAD_PAD = 128
QEXT = KV_LORA + LANES
VMEM_LIMIT = 52 * 1024 * 1024

TM = 512
TM_PROJ = 256
TQ = 256
PAGES_PER_STEP = 16
S5_Q_PROMPT = 16
S5_Q_SAMPLE = DEC_SEQ


def _cparams(sem):
    return pltpu.CompilerParams(dimension_semantics=sem, vmem_limit_bytes=VMEM_LIMIT)


def _rms(x, g):
    return x * lax.rsqrt(jnp.mean(x * x, axis=-1, keepdims=True) + EPS) * g


def _dot(a, b):
    return jnp.dot(a, b, preferred_element_type=F32)


def _dot_nt(a, b):
    return lax.dot_general(a, b, (((1,), (1,)), ((), ())), preferred_element_type=F32)


def _const_spec(shape):
    nd = len(shape)
    return pl.BlockSpec(shape, lambda *_: (0,) * nd)


def _rms_body(x_ref, g_ref, o_ref):
    o_ref[...] = _rms(x_ref[...], g_ref[...]).astype(o_ref.dtype)


def rmsnorm(x, g, dtype):
    rows = x.shape[0]
    return pl.pallas_call(
        _rms_body,
        grid=(rows // TM,),
        in_specs=[pl.BlockSpec((TM, D_MODEL), lambda i: (i, 0)), _const_spec((1, D_MODEL))],
        out_specs=pl.BlockSpec((TM, D_MODEL), lambda i: (i, 0)),
        out_shape=jax.ShapeDtypeStruct((rows, D_MODEL), dtype),
        compiler_params=_cparams(("parallel",)),
        name="rmsnorm",
    )(x, g.reshape(1, D_MODEL))


FF_CHUNK = 1024


def _mlp_body(x_ref, g_ref, w1_ref, w2_ref, o_ref):
    x = x_ref[...]
    h = _rms(x, g_ref[...]).astype(BF16)
    acc = x
    for c in range(D_FF // FF_CHUNK):
        sl = slice(c * FF_CHUNK, (c + 1) * FF_CHUNK)
        z = jnp.maximum(_dot(h, w1_ref[:, sl]), 0.0)
        acc = acc + _dot((z * z).astype(BF16), w2_ref[sl, :])
    o_ref[...] = acc


def mlp(x, g, w1, w2):
    rows = x.shape[0]
    return pl.pallas_call(
        _mlp_body,
        grid=(rows // TM,),
        in_specs=[
            pl.BlockSpec((TM, D_MODEL), lambda i: (i, 0)),
            _const_spec((1, D_MODEL)),
            _const_spec((D_MODEL, D_FF)),
            _const_spec((D_FF, D_MODEL)),
        ],
        out_specs=pl.BlockSpec((TM, D_MODEL), lambda i: (i, 0)),
        out_shape=jax.ShapeDtypeStruct((rows, D_MODEL), F32),
        compiler_params=_cparams(("parallel",)),
        name="mlp",
    )(x, g.reshape(1, D_MODEL), w1, w2)


def _proj_body(sample, x_ref, gm_ref, win_ref, gq_ref, gkv_ref, wq_ref, wqs_ref,
               cq_ref, sq_ref, ck_ref, sk_ref, wa_ref, wb_ref, *outs):
    h = _rms(x_ref[...], gm_ref[...]).astype(BF16)
    proj = _dot(h, win_ref[...])
    cq = _rms(proj[:, :Q_LORA], gq_ref[...]).astype(BF16)
    ckv = _rms(proj[:, Q_LORA:Q_LORA + KV_LORA], gkv_ref[...])
    kpe = (proj[:, 768:896] * ck_ref[...] + proj[:, 896:1024] * sk_ref[...])
    ckv_ref, kpe_ref = outs[0], outs[1]
    ckv_ref[...] = ckv
    kpe_ref[...] = kpe[:, :D_ROPE]
    cq_tab = cq_ref[...]
    sq_tab = sq_ref[...]
    for p in range(N_HEADS // 2):
        sl = slice(p * 2 * HEAD_PAD, (p + 1) * 2 * HEAD_PAD)
        qp = _dot(cq, wq_ref[:, sl]) * cq_tab + _dot(cq, wqs_ref[:, sl]) * sq_tab
        qp = qp.astype(BF16)
        if sample:
            qext_ref = outs[2]
            for hh in range(2):
                head = 2 * p + hh
                qh = qp[:, hh * HEAD_PAD:(hh + 1) * HEAD_PAD]
                qext_ref[head] = _dot(qh, wa_ref[head])
        else:
            outs[2][:, sl] = qp
    if not sample:
        k_ref, v_ref = outs[3], outs[4]
        ckv_b = ckv.astype(BF16)
        kpl = pltpu.roll(kpe, D_NOPE, axis=1)
        kpl2 = jnp.concatenate([kpl, kpl], axis=1)
        for p in range(N_HEADS // 2):
            sl = slice(p * 2 * HEAD_PAD, (p + 1) * 2 * HEAD_PAD)
            k_ref[:, sl] = (_dot(ckv_b, wa_ref[:, sl]) + kpl2).astype(BF16)
        v_ref[...] = _dot(ckv_b, wb_ref[...]).astype(BF16)


def mla_proj(x, sample, gm, w, tabs):
    tm = TM_PROJ
    if sample:
        rows, row0, tab_map = NS_ROWS, NP_ROWS // tm, (lambda i: (0, 0))
    else:
        rows, row0, tab_map = NP_ROWS, 0, (lambda i: (i % (SEQ // tm), 0))
    cq, sq, ck, sk = tabs
    wa = w["wuk_ext"] if sample else w["wk"]
    wb = w["wv"]
    out_shape = [jax.ShapeDtypeStruct((rows, KV_LORA), F32),
                 jax.ShapeDtypeStruct((rows, D_ROPE), F32)]
    out_specs = [pl.BlockSpec((tm, KV_LORA), lambda i: (i, 0)),
                 pl.BlockSpec((tm, D_ROPE), lambda i: (i, 0))]
    if sample:
        out_shape.append(jax.ShapeDtypeStruct((N_HEADS, rows, QEXT), F32))
        out_specs.append(pl.BlockSpec((N_HEADS, tm, QEXT), lambda i: (0, i, 0)))
    else:
        out_shape += [jax.ShapeDtypeStruct((rows, N_HEADS * HEAD_PAD), BF16),
                      jax.ShapeDtypeStruct((rows, N_HEADS * HEAD_PAD), BF16),
                      jax.ShapeDtypeStruct((rows, N_HEADS * D_V), BF16)]
        out_specs += [pl.BlockSpec((tm, N_HEADS * HEAD_PAD), lambda i: (i, 0)),
                      pl.BlockSpec((tm, N_HEADS * HEAD_PAD), lambda i: (i, 0)),
                      pl.BlockSpec((tm, N_HEADS * D_V), lambda i: (i, 0))]
    return pl.pallas_call(
        functools.partial(_proj_body, sample),
        grid=(rows // tm,),
        in_specs=[
            pl.BlockSpec((tm, D_MODEL), lambda i: (i + row0, 0)),
            _const_spec((1, D_MODEL)),
            _const_spec(w["win"].shape),
            _const_spec((1, Q_LORA)),
            _const_spec((1, KV_LORA)),
            _const_spec(w["wq"].shape),
            _const_spec(w["wqs"].shape),
            pl.BlockSpec((tm, 2 * HEAD_PAD), tab_map),
            pl.BlockSpec((tm, 2 * HEAD_PAD), tab_map),
            pl.BlockSpec((tm, LANES), tab_map),
            pl.BlockSpec((tm, LANES), tab_map),
            _const_spec(wa.shape),
            _const_spec(wb.shape),
        ],
        out_specs=out_specs,
        out_shape=out_shape,
        compiler_params=_cparams(("parallel",)),
        name="mla_proj_sample" if sample else "mla_proj_prompt",
    )(x, gm.reshape(1, D_MODEL), w["win"], w["gq"], w["gkv"], w["wq"], w["wqs"],
      cq, sq, ck, sk, wa, wb)


def _flash_body(q_ref, k_ref, v_ref, o_ref):
    qi = pl.program_id(2)
    row = lax.broadcasted_iota(jnp.int32, (TQ, TQ), 0)
    col = lax.broadcasted_iota(jnp.int32, (TQ, TQ), 1)
    outs = []
    for hh in range(2):
        q = q_ref[:, hh * HEAD_PAD:(hh + 1) * HEAD_PAD]

        def kv_step(j, carry, masked, hh=hh, q=q):
            m, l, acc = carry
            start = pl.multiple_of(j * TQ, TQ)
            k = k_ref[pl.ds(start, TQ), hh * HEAD_PAD:(hh + 1) * HEAD_PAD]
            v = v_ref[pl.ds(start, TQ), hh * D_V:(hh + 1) * D_V]
            s = _dot_nt(q, k) * SCALE
            if masked:
                s = jnp.where(row >= col, s, -jnp.inf)
            m_new = jnp.maximum(m, jnp.max(s, axis=-1, keepdims=True))
            alpha = jnp.exp(m - m_new)
            p = jnp.exp(s - m_new)
            l = alpha * l + jnp.sum(p, axis=-1, keepdims=True)
            acc = alpha * acc + _dot(p.astype(BF16), v)
            return m_new, l, acc

        init = (jnp.full((TQ, 1), -jnp.inf, F32), jnp.zeros((TQ, 1), F32),
                jnp.zeros((TQ, D_V), F32))
        carry = lax.fori_loop(0, qi, lambda j, c: kv_step(j, c, False), init)
        _, l, acc = kv_step(qi, carry, True)
        outs.append(acc / l)
    o_ref[...] = jnp.concatenate(outs, axis=1).astype(o_ref.dtype)


def flash_prompt(q, k, v):
    nq = SEQ // TQ
    return pl.pallas_call(
        _flash_body,
        grid=(BATCH, N_HEADS // 2, nq),
        in_specs=[
            pl.BlockSpec((TQ, 2 * HEAD_PAD), lambda b, p, i: (b * nq + i, p)),
            pl.BlockSpec((SEQ, 2 * HEAD_PAD), lambda b, p, i: (b, p)),
            pl.BlockSpec((SEQ, 2 * D_V), lambda b, p, i: (b, p)),
        ],
        out_specs=pl.BlockSpec((TQ, 2 * D_V), lambda b, p, i: (b * nq + i, p)),
        out_shape=jax.ShapeDtypeStruct((NP_ROWS, N_HEADS * D_V), BF16),
        compiler_params=_cparams(("parallel", "parallel", "arbitrary")),
        name="flash_prompt",
    )(q, k, v)


N_CHUNKS = N_PAGES // PAGES_PER_STEP
QROWS = N_HEADS * DEC_SEQ


def _sattn_body(pt_ref, q_ref, cnew_ref, knew_ref, *rest):
    del pt_ref
    n = PAGES_PER_STEP
    ckv_refs, kpe_refs = rest[:n], rest[n:2 * n]
    o_ref, qlat_s, qpe_s, m_s, l_s, acc_s = rest[2 * n:]
    c = pl.program_id(1)

    @pl.when(c == 0)
    def _():
        q = q_ref[...].reshape(QROWS, QEXT)
        qlat = q[:, :KV_LORA].astype(BF16)
        qpe = q[:, KV_LORA:KV_LORA + D_ROPE].astype(BF16)
        qlat_s[...] = qlat
        qpe_s[...] = qpe
        pad = PAGE_SIZE - DEC_SEQ
        cn = jnp.concatenate([cnew_ref[...], jnp.zeros((pad, KV_LORA), F32)], axis=0).astype(BF16)
        kn = jnp.concatenate([knew_ref[...], jnp.zeros((pad, D_ROPE), F32)], axis=0).astype(BF16)
        s = (_dot_nt(qlat, cn) + _dot_nt(qpe, kn)) * SCALE
        tok = lax.broadcasted_iota(jnp.int32, (QROWS, PAGE_SIZE), 0) % DEC_SEQ
        key = lax.broadcasted_iota(jnp.int32, (QROWS, PAGE_SIZE), 1)
        s = jnp.where(key <= tok, s, -jnp.inf)
        m = jnp.max(s, axis=-1, keepdims=True)
        p = jnp.exp(s - m)
        m_s[...] = m
        l_s[...] = jnp.sum(p, axis=-1, keepdims=True)
        acc_s[...] = _dot(p.astype(BF16), cn)

    qlat = qlat_s[...]
    qpe = qpe_s[...]
    cks = [r[...].astype(BF16) for r in ckv_refs]
    ss = [_dot_nt(qlat, cks[k]) + _dot_nt(qpe, kpe_refs[k][...].astype(BF16)) for k in range(n)]
    s = jnp.concatenate(ss, axis=1) * SCALE
    m_old = m_s[...]
    m_new = jnp.maximum(m_old, jnp.max(s, axis=-1, keepdims=True))
    alpha = jnp.exp(m_old - m_new)
    p = jnp.exp(s - m_new)
    l_s[...] = alpha * l_s[...] + jnp.sum(p, axis=-1, keepdims=True)
    pb = p.astype(BF16)
    acc = alpha * acc_s[...]
    for k in range(n):
        acc = acc + _dot(pb[:, k * PAGE_SIZE:(k + 1) * PAGE_SIZE], cks[k])
    acc_s[...] = acc
    m_s[...] = m_new

    @pl.when(c == N_CHUNKS - 1)
    def _():
        o_ref[...] = (acc_s[...] / l_s[...]).reshape(N_HEADS, DEC_SEQ, KV_LORA)


def attn_sample(layer, page_table, qext, ckv_all, kpe_all, cache_ckv, cache_kpe):
    n = PAGES_PER_STEP

    def page_spec(width, k):
        return pl.BlockSpec((None, None, PAGE_SIZE, width),
                            lambda b, c, pt, k=k: (layer, pt[b, c * n + k], 0, 0))

    in_specs = [
        pl.BlockSpec((N_HEADS, DEC_SEQ, QEXT), lambda b, c, pt: (0, b, 0)),
        pl.BlockSpec((DEC_SEQ, KV_LORA), lambda b, c, pt: (b, 0)),
        pl.BlockSpec((DEC_SEQ, D_ROPE), lambda b, c, pt: (b, 0)),
    ]
    in_specs += [page_spec(KV_LORA, k) for k in range(n)]
    in_specs += [page_spec(D_ROPE, k) for k in range(n)]
    grid_spec = pltpu.PrefetchScalarGridSpec(
        num_scalar_prefetch=1,
        grid=(DEC_BATCH, N_CHUNKS),
        in_specs=in_specs,
        out_specs=pl.BlockSpec((N_HEADS, DEC_SEQ, KV_LORA), lambda b, c, pt: (0, b, 0)),
        scratch_shapes=[
            pltpu.VMEM((QROWS, KV_LORA), BF16),
            pltpu.VMEM((QROWS, D_ROPE), BF16),
            pltpu.VMEM((QROWS, 1), F32),
            pltpu.VMEM((QROWS, 1), F32),
            pltpu.VMEM((QROWS, KV_LORA), F32),
        ],
    )
    return pl.pallas_call(
        _sattn_body,
        grid_spec=grid_spec,
        out_shape=jax.ShapeDtypeStruct((N_HEADS, NS_ROWS, KV_LORA), F32),
        compiler_params=_cparams(("parallel", "arbitrary")),
        name="attn_sample",
    )(page_table, qext, ckv_all, kpe_all, *([cache_ckv] * n), *([cache_kpe] * n))


def _oproj_prompt_body(x_ref, o_ref, wo_ref, out_ref):
    out_ref[...] = x_ref[...] + _dot(o_ref[...], wo_ref[...])


def oproj_prompt(x, o, wo):
    return pl.pallas_call(
        _oproj_prompt_body,
        grid=(NP_ROWS // TM,),
        in_specs=[pl.BlockSpec((TM, D_MODEL), lambda i: (i, 0)),
                  pl.BlockSpec((TM, N_HEADS * D_V), lambda i: (i, 0)),
                  _const_spec(wo.shape)],
        out_specs=pl.BlockSpec((TM, D_MODEL), lambda i: (i, 0)),
        out_shape=jax.ShapeDtypeStruct((ROWS, D_MODEL), F32),
        input_output_aliases={0: 0},
        compiler_params=_cparams(("parallel",)),
        name="oproj_prompt",
    )(x, o, wo)


def _oproj_sample_body(x_ref, ol_ref, wuv_ref, wo_ref, out_ref):
    acc = x_ref[...]
    for p in range(N_HEADS // 2):
        pair = jnp.concatenate([ol_ref[2 * p], ol_ref[2 * p + 1]], axis=1).astype(BF16)
        o_pair = _dot(pair, wuv_ref[p]).astype(BF16)
        acc = acc + _dot(o_pair, wo_ref[p * 2 * D_V:(p + 1) * 2 * D_V, :])
    out_ref[...] = acc


def oproj_sample(x, o_lat, wuv_pair, wo):
    row0 = NP_ROWS // TM
    return pl.pallas_call(
        _oproj_sample_body,
        grid=(NS_ROWS // TM,),
        in_specs=[pl.BlockSpec((TM, D_MODEL), lambda i: (i + row0, 0)),
                  pl.BlockSpec((N_HEADS, TM, KV_LORA), lambda i: (0, i, 0)),
                  _const_spec(wuv_pair.shape),
                  _const_spec(wo.shape)],
        out_specs=pl.BlockSpec((TM, D_MODEL), lambda i: (i + row0, 0)),
        out_shape=jax.ShapeDtypeStruct((ROWS, D_MODEL), F32),
        input_output_aliases={0: 0},
        compiler_params=_cparams(("parallel",)),
        name="oproj_sample",
    )(x, o_lat, wuv_pair, wo)


def _s5_body(n_chunks, bt, u_ref, t_ref, mre_ref, mim_ref, nre_ref, nim_ref,
             are_ref, aim_ref, h0re_ref, h0im_ref, y_ref, hre_ref, him_ref,
             sre_s, sim_s, pre_s, pim_s):
    u = u_ref[...]
    sre_s[...] = _dot(u, mre_ref[...])
    sim_s[...] = _dot(u, mim_ref[...])
    are = are_ref[...]
    aim = aim_ref[...]

    def step(c, carry):
        hre, him = carry
        r0 = pl.multiple_of(c * bt, bt)
        pre_s[pl.ds(r0, bt), :] = hre
        pim_s[pl.ds(r0, bt), :] = him
        nre = are * hre - aim * him + sre_s[pl.ds(r0, bt), :]
        nim = are * him + aim * hre + sim_s[pl.ds(r0, bt), :]
        return nre, nim

    hre, him = lax.fori_loop(0, n_chunks, step, (h0re_ref[...], h0im_ref[...]))
    hre_ref[...] = hre
    him_ref[...] = him
    y = _dot(u, t_ref[...])
    y = y + _dot(pre_s[...].astype(BF16), nre_ref[...])
    y = y - _dot(pim_s[...].astype(BF16), nim_ref[...])
    y_ref[...] = y


def s5_core(u, mats, h0re, h0im, n_chunks, bt):
    g, rows, qs = u.shape
    t, mre, mim, nre, nim, are, aim = mats

    def gspec(shape):
        return pl.BlockSpec((None,) + shape, lambda i: (i, 0, 0))

    return pl.pallas_call(
        functools.partial(_s5_body, n_chunks, bt),
        grid=(g,),
        in_specs=[gspec((rows, qs)), gspec((qs, qs)), gspec((qs, STATE_DIM)), gspec((qs, STATE_DIM)),
                  gspec((STATE_DIM, qs)), gspec((STATE_DIM, qs)), gspec((1, STATE_DIM)),
                  gspec((1, STATE_DIM)), gspec((bt, STATE_DIM)), gspec((bt, STATE_DIM))],
        out_specs=[gspec((rows, qs)), gspec((bt, STATE_DIM)), gspec((bt, STATE_DIM))],
        out_shape=[jax.ShapeDtypeStruct((g, rows, qs), F32),
                   jax.ShapeDtypeStruct((g, bt, STATE_DIM), F32),
                   jax.ShapeDtypeStruct((g, bt, STATE_DIM), F32)],
        scratch_shapes=[pltpu.VMEM((rows, STATE_DIM), F32)] * 4,
        compiler_params=_cparams(("parallel",)),
        name="s5_core",
    )(u, t, mre, mim, nre, nim, are, aim, h0re, h0im)


def _glu_body(x_ref, y_ref, g_ref, d_ref, w_ref, o_ref):
    x = x_ref[...]
    h = _rms(x, g_ref[...])
    y = y_ref[...] + d_ref[...] * h
    gl = jax.nn.gelu(y).astype(BF16)
    ab = _dot(gl, w_ref[...])
    o_ref[...] = x + ab[:, :D_MODEL] * jax.nn.sigmoid(ab[:, D_MODEL:])


def glu(x, y, g, d, w):
    rows = x.shape[0]
    return pl.pallas_call(
        _glu_body,
        grid=(rows // TM,),
        in_specs=[pl.BlockSpec((TM, D_MODEL), lambda i: (i, 0)),
                  pl.BlockSpec((TM, D_MODEL), lambda i: (i, 0)),
                  _const_spec((1, D_MODEL)), _const_spec((1, D_MODEL)),
                  _const_spec(w.shape)],
        out_specs=pl.BlockSpec((TM, D_MODEL), lambda i: (i, 0)),
        out_shape=jax.ShapeDtypeStruct((rows, D_MODEL), F32),
        compiler_params=_cparams(("parallel",)),
        name="glu",
    )(x, y, g.reshape(1, D_MODEL), d.reshape(1, D_MODEL), w)


def _prep_mla(w_in, g_q, g_kv, w_q_up, w_kv_up, w_o):
    z = lambda *s: jnp.zeros(s, F32)
    wk_pe = w_in[:, Q_LORA + KV_LORA:]
    half = D_ROPE // 2
    win = jnp.concatenate(
        [w_in[:, :Q_LORA + KV_LORA], wk_pe, z(D_MODEL, LANES - D_ROPE),
         -wk_pe[:, half:], wk_pe[:, :half], z(D_MODEL, LANES - D_ROPE)], axis=1)
    wq3 = w_q_up.reshape(Q_LORA, N_HEADS, D_NOPE + D_ROPE)
    nope, x1, x2 = wq3[..., :D_NOPE], wq3[..., D_NOPE:D_NOPE + half], wq3[..., D_NOPE + half:]
    zpad = z(Q_LORA, N_HEADS, HEAD_PAD - D_NOPE - D_ROPE)
    wq = jnp.concatenate([nope, x1, x2, zpad], axis=-1).reshape(Q_LORA, N_HEADS * HEAD_PAD)
    wqs = jnp.concatenate([z(Q_LORA, N_HEADS, D_NOPE), -x2, x1, zpad], axis=-1)
    wqs = wqs.reshape(Q_LORA, N_HEADS * HEAD_PAD)
    wkv3 = w_kv_up.reshape(KV_LORA, N_HEADS, D_NOPE + D_V)
    w_uk, w_uv = wkv3[..., :D_NOPE], wkv3[..., D_NOPE:]
    wk = jnp.concatenate([w_uk, z(KV_LORA, N_HEADS, HEAD_PAD - D_NOPE)], axis=-1)
    wk = wk.reshape(KV_LORA, N_HEADS * HEAD_PAD)
    wv = w_uv.reshape(KV_LORA, N_HEADS * D_V)
    wuk_ext = jnp.zeros((N_HEADS, HEAD_PAD, QEXT), F32)
    wuk_ext = wuk_ext.at[:, :D_NOPE, :KV_LORA].set(jnp.transpose(w_uk, (1, 2, 0)))
    wuk_ext = wuk_ext.at[:, D_NOPE:D_NOPE + D_ROPE, KV_LORA:KV_LORA + D_ROPE].set(
        jnp.broadcast_to(jnp.eye(D_ROPE, dtype=F32), (N_HEADS, D_ROPE, D_ROPE)))
    uv = jnp.transpose(w_uv, (1, 0, 2)).reshape(N_HEADS // 2, 2, KV_LORA, D_V)
    wuv_pair = jnp.zeros((N_HEADS // 2, 2 * KV_LORA, 2 * D_V), F32)
    wuv_pair = wuv_pair.at[:, :KV_LORA, :D_V].set(uv[:, 0])
    wuv_pair = wuv_pair.at[:, KV_LORA:, D_V:].set(uv[:, 1])
    return dict(win=win.astype(BF16), gq=g_q.reshape(1, Q_LORA), gkv=g_kv.reshape(1, KV_LORA),
                wq=wq.astype(BF16), wqs=wqs.astype(BF16), wk=wk.astype(BF16), wv=wv.astype(BF16),
                wuk_ext=wuk_ext.astype(BF16), wuv_pair=wuv_pair.astype(BF16), wo=w_o.astype(BF16))


def _rope_tabs(pos, reps):
    half = D_ROPE // 2
    inv = ROPE_THETA ** (-jnp.arange(half, dtype=F32) / half)
    ang = pos.astype(F32)[:, None] * inv[None, :]
    cos, sin = jnp.cos(ang), jnp.sin(ang)
    n = pos.shape[0]
    one, zero = jnp.ones((n, D_NOPE), F32), jnp.zeros((n, D_NOPE), F32)
    zpad = jnp.zeros((n, HEAD_PAD - D_NOPE - D_ROPE), F32)
    cq = jnp.concatenate([one, cos, cos, zpad], axis=1)
    sq = jnp.concatenate([zero, sin, sin, zpad], axis=1)
    kpad = jnp.zeros((n, LANES - D_ROPE), F32)
    ck = jnp.concatenate([cos, cos, kpad], axis=1)
    sk = jnp.concatenate([sin, sin, kpad], axis=1)
    cq, sq = jnp.tile(cq, (reps, 2)), jnp.tile(sq, (reps, 2))
    return cq, sq, jnp.tile(ck, (reps, 1)), jnp.tile(sk, (reps, 1))


def _prep_s5(a_re, a_im, log_dt, b_re, b_im, c_re, c_im, q):
    hi = lax.Precision.HIGHEST
    dt = jnp.exp(log_dt)[:, None]
    dre, dim = dt * a_re, dt * a_im
    mag = jnp.exp(dre)
    abr, abi = mag * jnp.cos(dim), mag * jnp.sin(dim)
    den = a_re * a_re + a_im * a_im
    xr, xi = abr - 1.0, abi
    fr, fi = (xr * a_re + xi * a_im) / den, (xi * a_re - xr * a_im) / den
    bbr = fr[..., None] * b_re - fi[..., None] * b_im
    bbi = fr[..., None] * b_im + fi[..., None] * b_re
    j = jnp.arange(q + 1, dtype=F32)[:, None, None]
    pmag = jnp.exp(j * dre)
    pwr, pwi = pmag * jnp.cos(j * dim), pmag * jnp.sin(j * dim)
    pbr = pwr[..., None] * bbr - pwi[..., None] * bbi
    pbi = pwr[..., None] * bbi + pwi[..., None] * bbr
    kk = (jnp.einsum("gtp,jgps->jgst", c_re, pbr[:q], precision=hi)
          - jnp.einsum("gtp,jgps->jgst", c_im, pbi[:q], precision=hi))
    lag = jnp.arange(q)[None, :] - jnp.arange(q)[:, None]
    tk = kk[jnp.clip(lag, 0, q - 1)]
    tk = jnp.where((lag >= 0)[:, :, None, None, None], tk, 0.0)
    g = a_re.shape[0]
    t = jnp.transpose(tk, (2, 0, 3, 1, 4)).reshape(g, q * GROUP_SIZE, q * GROUP_SIZE)
    mre = jnp.transpose(pbr[:q][::-1], (1, 0, 3, 2)).reshape(g, q * GROUP_SIZE, STATE_DIM)
    mim = jnp.transpose(pbi[:q][::-1], (1, 0, 3, 2)).reshape(g, q * GROUP_SIZE, STATE_DIM)
    ncr = pwr[1:, :, None, :] * c_re[None] - pwi[1:, :, None, :] * c_im[None]
    nci = pwr[1:, :, None, :] * c_im[None] + pwi[1:, :, None, :] * c_re[None]
    nre = jnp.transpose(ncr, (1, 3, 0, 2)).reshape(g, STATE_DIM, q * GROUP_SIZE)
    nim = jnp.transpose(nci, (1, 3, 0, 2)).reshape(g, STATE_DIM, q * GROUP_SIZE)
    are, aim = pwr[q][:, None, :], pwi[q][:, None, :]
    return (t.astype(BF16), mre.astype(BF16), mim.astype(BF16),
            nre.astype(BF16), nim.astype(BF16), are, aim)


def _s5_layer(x, g_mix, p, h0re, h0im):
    h = rmsnorm(x, g_mix, BF16)
    qp, qs = S5_Q_PROMPT, S5_Q_SAMPLE
    ncp = SEQ // qp
    up = h[:NP_ROWS].reshape(BATCH, ncp, qp, N_GROUPS, GROUP_SIZE)
    up = jnp.transpose(up, (3, 1, 0, 2, 4)).reshape(N_GROUPS, ncp * BATCH, qp * GROUP_SIZE)
    us = h[NP_ROWS:].reshape(DEC_BATCH, qs, N_GROUPS, GROUP_SIZE)
    us = jnp.transpose(us, (2, 0, 1, 3)).reshape(N_GROUPS, DEC_BATCH, qs * GROUP_SIZE)
    zero = jnp.zeros((N_GROUPS, BATCH, STATE_DIM), F32)
    yp, pre, pim = s5_core(up, _prep_s5(*p, qp), zero, zero, ncp, BATCH)
    ys, sre, sim = s5_core(us, _prep_s5(*p, qs), jnp.transpose(h0re, (1, 0, 2)),
                           jnp.transpose(h0im, (1, 0, 2)), 1, DEC_BATCH)
    yp = yp.reshape(N_GROUPS, ncp, BATCH, qp, GROUP_SIZE)
    yp = jnp.transpose(yp, (2, 1, 3, 0, 4)).reshape(NP_ROWS, D_MODEL)
    ys = ys.reshape(N_GROUPS, DEC_BATCH, qs, GROUP_SIZE)
    ys = jnp.transpose(ys, (1, 2, 0, 3)).reshape(NS_ROWS, D_MODEL)
    y = jnp.concatenate([yp, ys], axis=0)
    tr = lambda a: jnp.transpose(a, (1, 0, 2))
    return y, tr(pre), tr(pim), tr(sre), tr(sim)


def kernel(x_prompt, x_sample, cache_ckv, cache_kpe, state_ssm_re, state_ssm_im, page_table, norm_mix, norm_mlp, norm_final, mla_w_in, mla_q_norm, mla_kv_norm, mla_w_q_up, mla_w_kv_up, mla_w_o, ssm_a_re, ssm_a_im, ssm_log_dt, ssm_b_re, ssm_b_im, ssm_c_re, ssm_c_im, ssm_d, ssm_w_glu, mlp_w1, mlp_w2):
    x = jnp.concatenate([x_prompt.reshape(NP_ROWS, D_MODEL), x_sample.reshape(NS_ROWS, D_MODEL)], axis=0)
    tabs_p = _rope_tabs(jnp.arange(SEQ, dtype=jnp.int32), 1)
    tabs_s = _rope_tabs(PAST_LEN + jnp.arange(DEC_SEQ, dtype=jnp.int32), TM_PROJ // DEC_SEQ)
    p_ckv, p_kpe, p_re, p_im = [], [], [], []
    s_ckv, s_kpe, s_re, s_im = [], [], [], []
    for i in range(DEPTH):
        j = i // 2
        if i % 2 == 0:
            w = _prep_mla(mla_w_in[j], mla_q_norm[j], mla_kv_norm[j], mla_w_q_up[j],
                          mla_w_kv_up[j], mla_w_o[j])
            ckv_p, kpe_p, q_p, k_p, v_p = mla_proj(x, False, norm_mix[i], w, tabs_p)
            ckv_s, kpe_s, qext = mla_proj(x, True, norm_mix[i], w, tabs_s)
            o_p = flash_prompt(q_p, k_p, v_p)
            o_lat = attn_sample(j, page_table, qext, ckv_s, kpe_s, cache_ckv, cache_kpe)
            x = oproj_prompt(x, o_p, w["wo"])
            x = oproj_sample(x, o_lat, w["wuv_pair"], w["wo"])
            p_ckv.append(ckv_p.reshape(BATCH, SEQ, KV_LORA))
            p_kpe.append(kpe_p.reshape(BATCH, SEQ, D_ROPE))
            s_ckv.append(ckv_s.reshape(DEC_BATCH, DEC_SEQ, KV_LORA))
            s_kpe.append(kpe_s.reshape(DEC_BATCH, DEC_SEQ, D_ROPE))
        else:
            p = (ssm_a_re[j], ssm_a_im[j], ssm_log_dt[j], ssm_b_re[j], ssm_b_im[j],
                 ssm_c_re[j], ssm_c_im[j])
            y, pre, pim, sre, sim = _s5_layer(x, norm_mix[i], p, state_ssm_re[j], state_ssm_im[j])
            x = glu(x, y, norm_mix[i], ssm_d[j], ssm_w_glu[j].astype(BF16))
            p_re.append(pre); p_im.append(pim); s_re.append(sre); s_im.append(sim)
        x = mlp(x, norm_mlp[i], mlp_w1[i].astype(BF16), mlp_w2[i].astype(BF16))
    y = rmsnorm(x, norm_final, F32)
    y_prompt = y[:NP_ROWS].reshape(BATCH, SEQ, D_MODEL)
    y_sample = y[NP_ROWS:].reshape(DEC_BATCH, DEC_SEQ, D_MODEL)
    return (y_prompt, y_sample,
            jnp.stack(p_ckv), jnp.stack(p_kpe), jnp.stack(p_re), jnp.stack(p_im),
            jnp.stack(s_ckv), jnp.stack(s_kpe), jnp.stack(s_re), jnp.stack(s_im))
```

```python
import functools
import math

import jax
import jax.numpy as jnp
from jax import lax
from jax.experimental import pallas as pl
from jax.experimental.pallas import tpu as pltpu

F32 = jnp.float32
BF16 = jnp.bfloat16

D_MODEL = 1024
BATCH = 8
SEQ = 2048
DEPTH = 4
DEC_BATCH = 128
DEC_SEQ = 8
PAST_LEN = 8192
PAGE_SIZE = 128
N_PAGES = PAST_LEN // PAGE_SIZE
N_HEADS = 16
D_NOPE = 64
D_ROPE = 32
D_V = 64
Q_LORA = 512
KV_LORA = 256
ROPE_THETA = 10000.0
GROUP_SIZE = 16
N_GROUPS = D_MODEL // GROUP_SIZE
STATE_DIM = 64
D_FF = 4 * D_MODEL
EPS = 1e-6
SCALE = (D_NOPE + D_ROPE) ** -0.5
EXP2_SCALE = SCALE * math.log2(math.e)

NP_ROWS = BATCH * SEQ
NS_ROWS = DEC_BATCH * DEC_SEQ
ROWS = NP_ROWS + NS_ROWS

LANES = 128
HEAD_PAD = 128
QEXT = KV_LORA + LANES
VMEM_LIMIT = 52 * 1024 * 1024

TM = 512
TM_PROJ = 256
TQ = 256
PAGES_PER_STEP = 32
S5_Q_PROMPT = 16
S5_Q_SAMPLE = DEC_SEQ


def _cparams(sem):
    return pltpu.CompilerParams(dimension_semantics=sem, vmem_limit_bytes=VMEM_LIMIT)


def _rms(x, g):
    return x * lax.rsqrt(jnp.mean(x * x, axis=-1, keepdims=True) + EPS) * g


def _dot(a, b):
    return jnp.dot(a, b, preferred_element_type=F32)


def _dot_nt(a, b):
    return lax.dot_general(a, b, (((1,), (1,)), ((), ())), preferred_element_type=F32)


def _const_spec(shape):
    nd = len(shape)
    return pl.BlockSpec(shape, lambda *_: (0,) * nd)


def _rms_body(x_ref, g_ref, o_ref):
    o_ref[...] = _rms(x_ref[...], g_ref[...]).astype(o_ref.dtype)


def rmsnorm(x, g, dtype, row0=0, rows=ROWS):
    blk0 = row0 // TM
    return pl.pallas_call(
        _rms_body,
        grid=(rows // TM,),
        in_specs=[pl.BlockSpec((TM, D_MODEL), lambda i: (i + blk0, 0)), _const_spec((1, D_MODEL))],
        out_specs=pl.BlockSpec((TM, D_MODEL), lambda i: (i, 0)),
        out_shape=jax.ShapeDtypeStruct((rows, D_MODEL), dtype),
        compiler_params=_cparams(("parallel",)),
        name="rmsnorm",
    )(x, g.reshape(1, D_MODEL))


FF_CHUNK = 1024


def _mlp_body(x_ref, g_ref, w1_ref, w2_ref, o_ref):
    x = x_ref[...]
    h = _rms(x, g_ref[...]).astype(BF16)
    acc = x
    for c in range(D_FF // FF_CHUNK):
        sl = slice(c * FF_CHUNK, (c + 1) * FF_CHUNK)
        z = jnp.maximum(_dot(h, w1_ref[:, sl]), 0.0)
        acc = acc + _dot((z * z).astype(BF16), w2_ref[sl, :])
    o_ref[...] = acc


def mlp(x, g, w1, w2):
    rows = x.shape[0]
    return pl.pallas_call(
        _mlp_body,
        grid=(rows // TM,),
        in_specs=[
            pl.BlockSpec((TM, D_MODEL), lambda i: (i, 0)),
            _const_spec((1, D_MODEL)),
            _const_spec((D_MODEL, D_FF)),
            _const_spec((D_FF, D_MODEL)),
        ],
        out_specs=pl.BlockSpec((TM, D_MODEL), lambda i: (i, 0)),
        out_shape=jax.ShapeDtypeStruct((rows, D_MODEL), F32),
        compiler_params=_cparams(("parallel",)),
        name="mlp",
    )(x, g.reshape(1, D_MODEL), w1, w2)


def _proj_body(sample, x_ref, gm_ref, win_ref, gq_ref, gkv_ref, wq_ref, wqs_ref,
               cq_ref, sq_ref, ck_ref, sk_ref, wa_ref, wb_ref, *outs):
    h = _rms(x_ref[...], gm_ref[...]).astype(BF16)
    proj = _dot(h, win_ref[...])
    cq = _rms(proj[:, :Q_LORA], gq_ref[...]).astype(BF16)
    ckv = _rms(proj[:, Q_LORA:Q_LORA + KV_LORA], gkv_ref[...])
    kpe = (proj[:, 768:896] * ck_ref[...] + proj[:, 896:1024] * sk_ref[...])
    ckv_ref, kpe_ref = outs[0], outs[1]
    ckv_ref[...] = ckv
    kpe_ref[...] = kpe[:, :D_ROPE]
    cq_tab = cq_ref[...]
    sq_tab = sq_ref[...]
    for p in range(N_HEADS // 2):
        sl = slice(p * 2 * HEAD_PAD, (p + 1) * 2 * HEAD_PAD)
        qp = _dot(cq, wq_ref[:, sl]) * cq_tab + _dot(cq, wqs_ref[:, sl]) * sq_tab
        qp = qp.astype(BF16)
        if sample:
            qext_ref = outs[2]
            for hh in range(2):
                head = 2 * p + hh
                qh = qp[:, hh * HEAD_PAD:(hh + 1) * HEAD_PAD]
                qext_ref[head] = _dot(qh, wa_ref[head])
        else:
            outs[2][:, sl] = qp
    if not sample:
        k_ref, vt_ref = outs[3], outs[4]
        ckv_b = ckv.astype(BF16)
        kpl = pltpu.roll(kpe, D_NOPE, axis=1)
        kpl2 = jnp.concatenate([kpl, kpl], axis=1)
        for p in range(N_HEADS // 2):
            sl = slice(p * 2 * HEAD_PAD, (p + 1) * 2 * HEAD_PAD)
            k_ref[:, sl] = (_dot(ckv_b, wa_ref[:, sl]) + kpl2).astype(BF16)
        vt_ref[...] = _dot_nt(wb_ref[...], ckv_b).astype(BF16)


def mla_proj(x, sample, gm, w, tabs):
    tm = TM_PROJ
    if sample:
        rows, row0, tab_map = NS_ROWS, NP_ROWS // tm, (lambda i: (0, 0))
    else:
        rows, row0, tab_map = NP_ROWS, 0, (lambda i: (i % (SEQ // tm), 0))
    cq, sq, ck, sk = tabs
    wa = w["wuk_ext"] if sample else w["wk"]
    wb = w["wv"]
    out_shape = [jax.ShapeDtypeStruct((rows, KV_LORA), F32),
                 jax.ShapeDtypeStruct((rows, D_ROPE), F32)]
    out_specs = [pl.BlockSpec((tm, KV_LORA), lambda i: (i, 0)),
                 pl.BlockSpec((tm, D_ROPE), lambda i: (i, 0))]
    if sample:
        out_shape.append(jax.ShapeDtypeStruct((N_HEADS, rows, QEXT), F32))
        out_specs.append(pl.BlockSpec((N_HEADS, tm, QEXT), lambda i: (0, i, 0)))
    else:
        out_shape += [jax.ShapeDtypeStruct((rows, N_HEADS * HEAD_PAD), BF16),
                      jax.ShapeDtypeStruct((rows, N_HEADS * HEAD_PAD), BF16),
                      jax.ShapeDtypeStruct((N_HEADS * D_V, rows), BF16)]
        out_specs += [pl.BlockSpec((tm, N_HEADS * HEAD_PAD), lambda i: (i, 0)),
                      pl.BlockSpec((tm, N_HEADS * HEAD_PAD), lambda i: (i, 0)),
                      pl.BlockSpec((N_HEADS * D_V, tm), lambda i: (0, i))]
    return pl.pallas_call(
        functools.partial(_proj_body, sample),
        grid=(rows // tm,),
        in_specs=[
            pl.BlockSpec((tm, D_MODEL), lambda i: (i + row0, 0)),
            _const_spec((1, D_MODEL)),
            _const_spec(w["win"].shape),
            _const_spec((1, Q_LORA)),
            _const_spec((1, KV_LORA)),
            _const_spec(w["wq"].shape),
            _const_spec(w["wqs"].shape),
            pl.BlockSpec((tm, 2 * HEAD_PAD), tab_map),
            pl.BlockSpec((tm, 2 * HEAD_PAD), tab_map),
            pl.BlockSpec((tm, LANES), tab_map),
            pl.BlockSpec((tm, LANES), tab_map),
            _const_spec(wa.shape),
            _const_spec(wb.shape),
        ],
        out_specs=out_specs,
        out_shape=out_shape,
        compiler_params=_cparams(("parallel",)),
        name="mla_proj_sample" if sample else "mla_proj_prompt",
    )(x, gm.reshape(1, D_MODEL), w["win"], w["gq"], w["gkv"], w["wq"], w["wqs"],
      cq, sq, ck, sk, wa, wb)


def _flash_body(q_ref, k_ref, vt_ref, o_ref):
    key = lax.broadcasted_iota(jnp.int32, (TQ, TQ), 0)
    qry = lax.broadcasted_iota(jnp.int32, (TQ, TQ), 1)
    for qi in range(SEQ // TQ):
        q0 = qi * TQ
        ots = []
        for hh in range(2):
            hs = slice(hh * HEAD_PAD, (hh + 1) * HEAD_PAD)
            vs = slice(hh * D_V, (hh + 1) * D_V)
            q = q_ref[q0:q0 + TQ, hs]
            sd = _dot_nt(k_ref[q0:q0 + TQ, hs], q)
            sd = jnp.where(key <= qry, sd, -jnp.inf)
            m = jnp.max(sd, axis=0, keepdims=True)
            if qi > 0:
                sm = _dot_nt(k_ref[0:q0, hs], q)
                m = jnp.maximum(m, jnp.max(sm, axis=0, keepdims=True))
            pd = jnp.exp2((sd - m) * EXP2_SCALE)
            l = jnp.sum(pd, axis=0, keepdims=True)
            ot = _dot(vt_ref[vs, q0:q0 + TQ], pd.astype(BF16))
            if qi > 0:
                pm = jnp.exp2((sm - m) * EXP2_SCALE)
                l = l + jnp.sum(pm, axis=0, keepdims=True)
                ot = ot + _dot(vt_ref[vs, 0:q0], pm.astype(BF16))
            ots.append(ot / l)
        o_ref[q0:q0 + TQ, :] = jnp.concatenate(ots, axis=0).T.astype(o_ref.dtype)


def flash_prompt(q, k, vt):
    return pl.pallas_call(
        _flash_body,
        grid=(BATCH, N_HEADS // 2),
        in_specs=[
            pl.BlockSpec((SEQ, 2 * HEAD_PAD), lambda b, p: (b, p)),
            pl.BlockSpec((SEQ, 2 * HEAD_PAD), lambda b, p: (b, p)),
            pl.BlockSpec((2 * D_V, SEQ), lambda b, p: (p, b)),
        ],
        out_specs=pl.BlockSpec((SEQ, 2 * D_V), lambda b, p: (b, p)),
        out_shape=jax.ShapeDtypeStruct((NP_ROWS, N_HEADS * D_V), BF16),
        compiler_params=_cparams(("parallel", "parallel")),
        name="flash_prompt",
    )(q, k, vt)


N_CHUNKS = N_PAGES // PAGES_PER_STEP
QROWS = N_HEADS * DEC_SEQ


def _sattn_body(pt_ref, q_ref, cnew_ref, knew_ref, *rest):
    del pt_ref
    n = PAGES_PER_STEP
    ckv_refs, kpet_refs = rest[:n], rest[n:2 * n]
    o_ref, qlat_s, qpe_s, m_s, l_s, acc_s, ck_s, kt_s = rest[2 * n:]
    c = pl.program_id(1)

    @pl.when(c == 0)
    def _():
        q = q_ref[...].reshape(QROWS, QEXT)
        qlat = q[:, :KV_LORA].astype(BF16)
        qpe = q[:, KV_LORA:KV_LORA + D_ROPE].astype(BF16)
        qlat_s[...] = qlat
        qpe_s[...] = qpe
        pad = PAGE_SIZE - DEC_SEQ
        cn = jnp.concatenate([cnew_ref[...], jnp.zeros((pad, KV_LORA), F32)], axis=0).astype(BF16)
        kn = jnp.concatenate([knew_ref[...], jnp.zeros((pad, D_ROPE), F32)], axis=0).astype(BF16)
        s = _dot_nt(qlat, cn) + _dot_nt(qpe, kn)
        tok = lax.broadcasted_iota(jnp.int32, (QROWS, PAGE_SIZE), 0) % DEC_SEQ
        key = lax.broadcasted_iota(jnp.int32, (QROWS, PAGE_SIZE), 1)
        s = jnp.where(key <= tok, s, -jnp.inf)
        m = jnp.max(s, axis=-1, keepdims=True)
        p = jnp.exp2((s - m) * EXP2_SCALE)
        m_s[...] = m
        l_s[...] = jnp.sum(p, axis=-1, keepdims=True)
        acc_s[...] = _dot(p.astype(BF16), cn)

    for k in range(n):
        ck_s[k * PAGE_SIZE:(k + 1) * PAGE_SIZE, :] = ckv_refs[k][...].astype(BF16)
        kt_s[:, k * PAGE_SIZE:(k + 1) * PAGE_SIZE] = kpet_refs[k][...].astype(BF16)
    ck = ck_s[...]
    s = _dot_nt(qlat_s[...], ck) + _dot(qpe_s[...], kt_s[...])
    m_old = m_s[...]
    m_new = jnp.maximum(m_old, jnp.max(s, axis=-1, keepdims=True))
    alpha = jnp.exp2((m_old - m_new) * EXP2_SCALE)
    p = jnp.exp2((s - m_new) * EXP2_SCALE)
    l_s[...] = alpha * l_s[...] + jnp.sum(p, axis=-1, keepdims=True)
    acc_s[...] = alpha * acc_s[...] + _dot(p.astype(BF16), ck)
    m_s[...] = m_new

    @pl.when(c == N_CHUNKS - 1)
    def _():
        o_ref[...] = (acc_s[...] / l_s[...]).reshape(N_HEADS, DEC_SEQ, KV_LORA)


def attn_sample(layer, page_table, qext, ckv_all, kpe_all, cache_ckv, cache_kpet):
    n = PAGES_PER_STEP

    def page_spec(rows, width, k):
        return pl.BlockSpec((None, None, rows, width),
                            lambda b, c, pt, k=k: (layer, pt[b, c * n + k], 0, 0))

    in_specs = [
        pl.BlockSpec((N_HEADS, DEC_SEQ, QEXT), lambda b, c, pt: (0, b, 0)),
        pl.BlockSpec((DEC_SEQ, KV_LORA), lambda b, c, pt: (b, 0)),
        pl.BlockSpec((DEC_SEQ, D_ROPE), lambda b, c, pt: (b, 0)),
    ]
    in_specs += [page_spec(PAGE_SIZE, KV_LORA, k) for k in range(n)]
    in_specs += [page_spec(D_ROPE, PAGE_SIZE, k) for k in range(n)]
    grid_spec = pltpu.PrefetchScalarGridSpec(
        num_scalar_prefetch=1,
        grid=(DEC_BATCH, N_CHUNKS),
        in_specs=in_specs,
        out_specs=pl.BlockSpec((N_HEADS, DEC_SEQ, KV_LORA), lambda b, c, pt: (0, b, 0)),
        scratch_shapes=[
            pltpu.VMEM((QROWS, KV_LORA), BF16),
            pltpu.VMEM((QROWS, D_ROPE), BF16),
            pltpu.VMEM((QROWS, 1), F32),
            pltpu.VMEM((QROWS, 1), F32),
            pltpu.VMEM((QROWS, KV_LORA), F32),
            pltpu.VMEM((n * PAGE_SIZE, KV_LORA), BF16),
            pltpu.VMEM((D_ROPE, n * PAGE_SIZE), BF16),
        ],
    )
    return pl.pallas_call(
        _sattn_body,
        grid_spec=grid_spec,
        out_shape=jax.ShapeDtypeStruct((N_HEADS, NS_ROWS, KV_LORA), F32),
        compiler_params=_cparams(("parallel", "arbitrary")),
        name="attn_sample",
    )(page_table, qext, ckv_all, kpe_all, *([cache_ckv] * n), *([cache_kpet] * n))


def _oproj_prompt_body(x_ref, o_ref, wo_ref, out_ref):
    out_ref[...] = x_ref[...] + _dot(o_ref[...], wo_ref[...])


def oproj_prompt(x, o, wo):
    return pl.pallas_call(
        _oproj_prompt_body,
        grid=(NP_ROWS // TM,),
        in_specs=[pl.BlockSpec((TM, D_MODEL), lambda i: (i, 0)),
                  pl.BlockSpec((TM, N_HEADS * D_V), lambda i: (i, 0)),
                  _const_spec(wo.shape)],
        out_specs=pl.BlockSpec((TM, D_MODEL), lambda i: (i, 0)),
        out_shape=jax.ShapeDtypeStruct((ROWS, D_MODEL), F32),
        input_output_aliases={0: 0},
        compiler_params=_cparams(("parallel",)),
        name="oproj_prompt",
    )(x, o, wo)


def _oproj_sample_body(x_ref, ol_ref, wuv_ref, wo_ref, out_ref):
    acc = x_ref[...]
    for p in range(N_HEADS // 2):
        pair = jnp.concatenate([ol_ref[2 * p], ol_ref[2 * p + 1]], axis=1).astype(BF16)
        o_pair = _dot(pair, wuv_ref[p]).astype(BF16)
        acc = acc + _dot(o_pair, wo_ref[p * 2 * D_V:(p + 1) * 2 * D_V, :])
    out_ref[...] = acc


def oproj_sample(x, o_lat, wuv_pair, wo):
    row0 = NP_ROWS // TM
    return pl.pallas_call(
        _oproj_sample_body,
        grid=(NS_ROWS // TM,),
        in_specs=[pl.BlockSpec((TM, D_MODEL), lambda i: (i + row0, 0)),
                  pl.BlockSpec((N_HEADS, TM, KV_LORA), lambda i: (0, i, 0)),
                  _const_spec(wuv_pair.shape),
                  _const_spec(wo.shape)],
        out_specs=pl.BlockSpec((TM, D_MODEL), lambda i: (i + row0, 0)),
        out_shape=jax.ShapeDtypeStruct((ROWS, D_MODEL), F32),
        input_output_aliases={0: 0},
        compiler_params=_cparams(("parallel",)),
        name="oproj_sample",
    )(x, o_lat, wuv_pair, wo)


def _s5_body(n_chunks, bt, u_ref, t_ref, mre_ref, mim_ref, nre_ref, nim_ref,
             are_ref, aim_ref, h0re_ref, h0im_ref, y_ref, hre_ref, him_ref,
             sre_s, sim_s, pre_s, pim_s):
    u = u_ref[...]
    sre_s[...] = _dot(u, mre_ref[...])
    sim_s[...] = _dot(u, mim_ref[...])
    are = are_ref[...]
    aim = aim_ref[...]

    def step(c, carry):
        hre, him = carry
        r0 = pl.multiple_of(c * bt, bt)
        pre_s[pl.ds(r0, bt), :] = hre
        pim_s[pl.ds(r0, bt), :] = him
        nre = are * hre - aim * him + sre_s[pl.ds(r0, bt), :]
        nim = are * him + aim * hre + sim_s[pl.ds(r0, bt), :]
        return nre, nim

    hre, him = lax.fori_loop(0, n_chunks, step, (h0re_ref[...], h0im_ref[...]))
    hre_ref[...] = hre
    him_ref[...] = him
    y = _dot(u, t_ref[...])
    y = y + _dot(pre_s[...].astype(BF16), nre_ref[...])
    y = y - _dot(pim_s[...].astype(BF16), nim_ref[...])
    y_ref[...] = y


def s5_core(u, mats, h0re, h0im, n_chunks, bt):
    g, rows, qs = u.shape
    t, mre, mim, nre, nim, are, aim = mats

    def gspec(shape):
        return pl.BlockSpec((None,) + shape, lambda i: (i, 0, 0))

    return pl.pallas_call(
        functools.partial(_s5_body, n_chunks, bt),
        grid=(g,),
        in_specs=[gspec((rows, qs)), gspec((qs, qs)), gspec((qs, STATE_DIM)), gspec((qs, STATE_DIM)),
                  gspec((STATE_DIM, qs)), gspec((STATE_DIM, qs)), gspec((1, STATE_DIM)),
                  gspec((1, STATE_DIM)), gspec((bt, STATE_DIM)), gspec((bt, STATE_DIM))],
        out_specs=[gspec((rows, qs)), gspec((bt, STATE_DIM)), gspec((bt, STATE_DIM))],
        out_shape=[jax.ShapeDtypeStruct((g, rows, qs), F32),
                   jax.ShapeDtypeStruct((g, bt, STATE_DIM), F32),
                   jax.ShapeDtypeStruct((g, bt, STATE_DIM), F32)],
        scratch_shapes=[pltpu.VMEM((rows, STATE_DIM), F32)] * 4,
        compiler_params=_cparams(("parallel",)),
        name="s5_core",
    )(u, t, mre, mim, nre, nim, are, aim, h0re, h0im)


def _glu_body(x_ref, y_ref, g_ref, d_ref, w_ref, o_ref):
    x = x_ref[...]
    h = _rms(x, g_ref[...])
    y = y_ref[...] + d_ref[...] * h
    gl = jax.nn.gelu(y).astype(BF16)
    ab = _dot(gl, w_ref[...])
    o_ref[...] = x + ab[:, :D_MODEL] * jax.nn.sigmoid(ab[:, D_MODEL:])


def glu(x, y, g, d, w):
    rows = x.shape[0]
    return pl.pallas_call(
        _glu_body,
        grid=(rows // TM,),
        in_specs=[pl.BlockSpec((TM, D_MODEL), lambda i: (i, 0)),
                  pl.BlockSpec((TM, D_MODEL), lambda i: (i, 0)),
                  _const_spec((1, D_MODEL)), _const_spec((1, D_MODEL)),
                  _const_spec(w.shape)],
        out_specs=pl.BlockSpec((TM, D_MODEL), lambda i: (i, 0)),
        out_shape=jax.ShapeDtypeStruct((rows, D_MODEL), F32),
        compiler_params=_cparams(("parallel",)),
        name="glu",
    )(x, y, g.reshape(1, D_MODEL), d.reshape(1, D_MODEL), w)


def _prep_mla(w_in, g_q, g_kv, w_q_up, w_kv_up, w_o):
    z = lambda *s: jnp.zeros(s, F32)
    wk_pe = w_in[:, Q_LORA + KV_LORA:]
    half = D_ROPE // 2
    win = jnp.concatenate(
        [w_in[:, :Q_LORA + KV_LORA], wk_pe, z(D_MODEL, LANES - D_ROPE),
         -wk_pe[:, half:], wk_pe[:, :half], z(D_MODEL, LANES - D_ROPE)], axis=1)
    wq3 = w_q_up.reshape(Q_LORA, N_HEADS, D_NOPE + D_ROPE)
    nope, x1, x2 = wq3[..., :D_NOPE], wq3[..., D_NOPE:D_NOPE + half], wq3[..., D_NOPE + half:]
    zpad = z(Q_LORA, N_HEADS, HEAD_PAD - D_NOPE - D_ROPE)
    wq = jnp.concatenate([nope, x1, x2, zpad], axis=-1).reshape(Q_LORA, N_HEADS * HEAD_PAD)
    wqs = jnp.concatenate([z(Q_LORA, N_HEADS, D_NOPE), -x2, x1, zpad], axis=-1)
    wqs = wqs.reshape(Q_LORA, N_HEADS * HEAD_PAD)
    wkv3 = w_kv_up.reshape(KV_LORA, N_HEADS, D_NOPE + D_V)
    w_uk, w_uv = wkv3[..., :D_NOPE], wkv3[..., D_NOPE:]
    wk = jnp.concatenate([w_uk, z(KV_LORA, N_HEADS, HEAD_PAD - D_NOPE)], axis=-1)
    wk = wk.reshape(KV_LORA, N_HEADS * HEAD_PAD)
    wv = w_uv.reshape(KV_LORA, N_HEADS * D_V).T
    wuk_ext = jnp.zeros((N_HEADS, HEAD_PAD, QEXT), F32)
    wuk_ext = wuk_ext.at[:, :D_NOPE, :KV_LORA].set(jnp.transpose(w_uk, (1, 2, 0)))
    wuk_ext = wuk_ext.at[:, D_NOPE:D_NOPE + D_ROPE, KV_LORA:KV_LORA + D_ROPE].set(
        jnp.broadcast_to(jnp.eye(D_ROPE, dtype=F32), (N_HEADS, D_ROPE, D_ROPE)))
    uv = jnp.transpose(w_uv, (1, 0, 2)).reshape(N_HEADS // 2, 2, KV_LORA, D_V)
    wuv_pair = jnp.zeros((N_HEADS // 2, 2 * KV_LORA, 2 * D_V), F32)
    wuv_pair = wuv_pair.at[:, :KV_LORA, :D_V].set(uv[:, 0])
    wuv_pair = wuv_pair.at[:, KV_LORA:, D_V:].set(uv[:, 1])
    return dict(win=win.astype(BF16), gq=g_q.reshape(1, Q_LORA), gkv=g_kv.reshape(1, KV_LORA),
                wq=wq.astype(BF16), wqs=wqs.astype(BF16), wk=wk.astype(BF16), wv=wv.astype(BF16),
                wuk_ext=wuk_ext.astype(BF16), wuv_pair=wuv_pair.astype(BF16), wo=w_o.astype(BF16))


def _rope_tabs(pos, reps):
    half = D_ROPE // 2
    inv = ROPE_THETA ** (-jnp.arange(half, dtype=F32) / half)
    ang = pos.astype(F32)[:, None] * inv[None, :]
    cos, sin = jnp.cos(ang), jnp.sin(ang)
    n = pos.shape[0]
    one, zero = jnp.ones((n, D_NOPE), F32), jnp.zeros((n, D_NOPE), F32)
    zpad = jnp.zeros((n, HEAD_PAD - D_NOPE - D_ROPE), F32)
    cq = jnp.concatenate([one, cos, cos, zpad], axis=1)
    sq = jnp.concatenate([zero, sin, sin, zpad], axis=1)
    kpad = jnp.zeros((n, LANES - D_ROPE), F32)
    ck = jnp.concatenate([cos, cos, kpad], axis=1)
    sk = jnp.concatenate([sin, sin, kpad], axis=1)
    cq, sq = jnp.tile(cq, (reps, 2)), jnp.tile(sq, (reps, 2))
    return cq, sq, jnp.tile(ck, (reps, 1)), jnp.tile(sk, (reps, 1))


def _prep_s5(a_re, a_im, log_dt, b_re, b_im, c_re, c_im, q):
    hi = lax.Precision.HIGHEST
    dt = jnp.exp(log_dt)[:, None]
    dre, dim = dt * a_re, dt * a_im
    mag = jnp.exp(dre)
    abr, abi = mag * jnp.cos(dim), mag * jnp.sin(dim)
    den = a_re * a_re + a_im * a_im
    xr, xi = abr - 1.0, abi
    fr, fi = (xr * a_re + xi * a_im) / den, (xi * a_re - xr * a_im) / den
    bbr = fr[..., None] * b_re - fi[..., None] * b_im
    bbi = fr[..., None] * b_im + fi[..., None] * b_re
    j = jnp.arange(q + 1, dtype=F32)[:, None, None]
    pmag = jnp.exp(j * dre)
    pwr, pwi = pmag * jnp.cos(j * dim), pmag * jnp.sin(j * dim)
    pbr = pwr[..., None] * bbr - pwi[..., None] * bbi
    pbi = pwr[..., None] * bbi + pwi[..., None] * bbr
    kk = (jnp.einsum("gtp,jgps->jgst", c_re, pbr[:q], precision=hi)
          - jnp.einsum("gtp,jgps->jgst", c_im, pbi[:q], precision=hi))
    lag = jnp.arange(q)[None, :] - jnp.arange(q)[:, None]
    tk = kk[jnp.clip(lag, 0, q - 1)]
    tk = jnp.where((lag >= 0)[:, :, None, None, None], tk, 0.0)
    g = a_re.shape[0]
    t = jnp.transpose(tk, (2, 0, 3, 1, 4)).reshape(g, q * GROUP_SIZE, q * GROUP_SIZE)
    mre = jnp.transpose(pbr[:q][::-1], (1, 0, 3, 2)).reshape(g, q * GROUP_SIZE, STATE_DIM)
    mim = jnp.transpose(pbi[:q][::-1], (1, 0, 3, 2)).reshape(g, q * GROUP_SIZE, STATE_DIM)
    ncr = pwr[1:, :, None, :] * c_re[None] - pwi[1:, :, None, :] * c_im[None]
    nci = pwr[1:, :, None, :] * c_im[None] + pwi[1:, :, None, :] * c_re[None]
    nre = jnp.transpose(ncr, (1, 3, 0, 2)).reshape(g, STATE_DIM, q * GROUP_SIZE)
    nim = jnp.transpose(nci, (1, 3, 0, 2)).reshape(g, STATE_DIM, q * GROUP_SIZE)
    are, aim = pwr[q][:, None, :], pwi[q][:, None, :]
    return (t.astype(BF16), mre.astype(BF16), mim.astype(BF16),
            nre.astype(BF16), nim.astype(BF16), are, aim)


def _s5_layer(x, g_mix, p, h0re, h0im):
    h = rmsnorm(x, g_mix, BF16)
    qp, qs = S5_Q_PROMPT, S5_Q_SAMPLE
    ncp = SEQ // qp
    up = h[:NP_ROWS].reshape(BATCH, ncp, qp, N_GROUPS, GROUP_SIZE)
    up = jnp.transpose(up, (3, 1, 0, 2, 4)).reshape(N_GROUPS, ncp * BATCH, qp * GROUP_SIZE)
    us = h[NP_ROWS:].reshape(DEC_BATCH, qs, N_GROUPS, GROUP_SIZE)
    us = jnp.transpose(us, (2, 0, 1, 3)).reshape(N_GROUPS, DEC_BATCH, qs * GROUP_SIZE)
    zero = jnp.zeros((N_GROUPS, BATCH, STATE_DIM), F32)
    yp, pre, pim = s5_core(up, _prep_s5(*p, qp), zero, zero, ncp, BATCH)
    ys, sre, sim = s5_core(us, _prep_s5(*p, qs), jnp.transpose(h0re, (1, 0, 2)),
                           jnp.transpose(h0im, (1, 0, 2)), 1, DEC_BATCH)
    yp = yp.reshape(N_GROUPS, ncp, BATCH, qp, GROUP_SIZE)
    yp = jnp.transpose(yp, (2, 1, 3, 0, 4)).reshape(NP_ROWS, D_MODEL)
    ys = ys.reshape(N_GROUPS, DEC_BATCH, qs, GROUP_SIZE)
    ys = jnp.transpose(ys, (1, 2, 0, 3)).reshape(NS_ROWS, D_MODEL)
    y = jnp.concatenate([yp, ys], axis=0)
    tr = lambda a: jnp.transpose(a, (1, 0, 2))
    return y, tr(pre), tr(pim), tr(sre), tr(sim)


def kernel(x_prompt, x_sample, cache_ckv, cache_kpe, state_ssm_re, state_ssm_im, page_table, norm_mix, norm_mlp, norm_final, mla_w_in, mla_q_norm, mla_kv_norm, mla_w_q_up, mla_w_kv_up, mla_w_o, ssm_a_re, ssm_a_im, ssm_log_dt, ssm_b_re, ssm_b_im, ssm_c_re, ssm_c_im, ssm_d, ssm_w_glu, mlp_w1, mlp_w2):
    x = jnp.concatenate([x_prompt.reshape(NP_ROWS, D_MODEL), x_sample.reshape(NS_ROWS, D_MODEL)], axis=0)
    tabs_p = _rope_tabs(jnp.arange(SEQ, dtype=jnp.int32), 1)
    tabs_s = _rope_tabs(PAST_LEN + jnp.arange(DEC_SEQ, dtype=jnp.int32), TM_PROJ // DEC_SEQ)
    cache_kpet = jnp.swapaxes(cache_kpe, 2, 3)
    p_ckv, p_kpe, p_re, p_im = [], [], [], []
    s_ckv, s_kpe, s_re, s_im = [], [], [], []
    for i in range(DEPTH):
        j = i // 2
        if i % 2 == 0:
            w = _prep_mla(mla_w_in[j], mla_q_norm[j], mla_kv_norm[j], mla_w_q_up[j],
                          mla_w_kv_up[j], mla_w_o[j])
            ckv_p, kpe_p, q_p, k_p, v_p = mla_proj(x, False, norm_mix[i], w, tabs_p)
            ckv_s, kpe_s, qext = mla_proj(x, True, norm_mix[i], w, tabs_s)
            o_p = flash_prompt(q_p, k_p, v_p)
            o_lat = attn_sample(j, page_table, qext, ckv_s, kpe_s, cache_ckv, cache_kpet)
            x = oproj_prompt(x, o_p, w["wo"])
            x = oproj_sample(x, o_lat, w["wuv_pair"], w["wo"])
            p_ckv.append(ckv_p.reshape(BATCH, SEQ, KV_LORA))
            p_kpe.append(kpe_p.reshape(BATCH, SEQ, D_ROPE))
            s_ckv.append(ckv_s.reshape(DEC_BATCH, DEC_SEQ, KV_LORA))
            s_kpe.append(kpe_s.reshape(DEC_BATCH, DEC_SEQ, D_ROPE))
        else:
            p = (ssm_a_re[j], ssm_a_im[j], ssm_log_dt[j], ssm_b_re[j], ssm_b_im[j],
                 ssm_c_re[j], ssm_c_im[j])
            y, pre, pim, sre, sim = _s5_layer(x, norm_mix[i], p, state_ssm_re[j], state_ssm_im[j])
            x = glu(x, y, norm_mix[i], ssm_d[j], ssm_w_glu[j].astype(BF16))
            p_re.append(pre); p_im.append(pim); s_re.append(sre); s_im.append(sim)
        x = mlp(x, norm_mlp[i], mlp_w1[i].astype(BF16), mlp_w2[i].astype(BF16))
    y_prompt = rmsnorm(x, norm_final, F32, 0, NP_ROWS).reshape(BATCH, SEQ, D_MODEL)
    y_sample = rmsnorm(x, norm_final, F32, NP_ROWS, NS_ROWS).reshape(DEC_BATCH, DEC_SEQ, D_MODEL)
    return (y_prompt, y_sample,
            jnp.stack(p_ckv), jnp.stack(p_kpe), jnp.stack(p_re), jnp.stack(p_im),
            jnp.stack(s_ckv), jnp.stack(s_kpe), jnp.stack(s_re), jnp.stack(s_im))
```

```python
import functools
import math

import jax
import jax.numpy as jnp
from jax import lax
from jax.experimental import pallas as pl
from jax.experimental.pallas import tpu as pltpu

F32 = jnp.float32
BF16 = jnp.bfloat16

D_MODEL = 1024
BATCH = 8
SEQ = 2048
DEPTH = 4
DEC_BATCH = 128
DEC_SEQ = 8
PAST_LEN = 8192
PAGE_SIZE = 128
N_PAGES = PAST_LEN // PAGE_SIZE
N_HEADS = 16
D_NOPE = 64
D_ROPE = 32
D_V = 64
Q_LORA = 512
KV_LORA = 256
ROPE_THETA = 10000.0
GROUP_SIZE = 16
N_GROUPS = D_MODEL // GROUP_SIZE
STATE_DIM = 64
D_FF = 4 * D_MODEL
EPS = 1e-6
SCALE = (D_NOPE + D_ROPE) ** -0.5
EXP2_SCALE = SCALE * math.log2(math.e)

NP_ROWS = BATCH * SEQ
NS_ROWS = DEC_BATCH * DEC_SEQ
ROWS = NP_ROWS + NS_ROWS

LANES = 128
HEAD_PAD = 128
QEXT = KV_LORA + LANES
VMEM_LIMIT = 52 * 1024 * 1024

TM = 512
TM_PROJ = 256
TQ = 256
PAGES_PER_STEP = 32


def _cparams(sem):
    return pltpu.CompilerParams(dimension_semantics=sem, vmem_limit_bytes=VMEM_LIMIT)


def _rms(x, g):
    return x * lax.rsqrt(jnp.mean(x * x, axis=-1, keepdims=True) + EPS) * g


def _dot(a, b):
    return jnp.dot(a, b, preferred_element_type=F32)


def _dot_nt(a, b):
    return lax.dot_general(a, b, (((1,), (1,)), ((), ())), preferred_element_type=F32)


def _const_spec(shape):
    nd = len(shape)
    return pl.BlockSpec(shape, lambda *_: (0,) * nd)


def _rms_body(x_ref, g_ref, o_ref):
    o_ref[...] = _rms(x_ref[...], g_ref[...]).astype(o_ref.dtype)


def rmsnorm(x, g, dtype, row0=0, rows=ROWS):
    blk0 = row0 // TM
    return pl.pallas_call(
        _rms_body,
        grid=(rows // TM,),
        in_specs=[pl.BlockSpec((TM, D_MODEL), lambda i: (i + blk0, 0)), _const_spec((1, D_MODEL))],
        out_specs=pl.BlockSpec((TM, D_MODEL), lambda i: (i, 0)),
        out_shape=jax.ShapeDtypeStruct((rows, D_MODEL), dtype),
        compiler_params=_cparams(("parallel",)),
        name="rmsnorm",
    )(x, g.reshape(1, D_MODEL))


FF_CHUNK = 1024


def _mlp_body(x_ref, g_ref, w1_ref, w2_ref, o_ref):
    x = x_ref[...]
    h = _rms(x, g_ref[...]).astype(BF16)
    acc = x
    for c in range(D_FF // FF_CHUNK):
        sl = slice(c * FF_CHUNK, (c + 1) * FF_CHUNK)
        z = jnp.maximum(_dot(h, w1_ref[:, sl]), 0.0)
        acc = acc + _dot((z * z).astype(BF16), w2_ref[sl, :])
    o_ref[...] = acc


def mlp(x, g, w1, w2):
    rows = x.shape[0]
    return pl.pallas_call(
        _mlp_body,
        grid=(rows // TM,),
        in_specs=[
            pl.BlockSpec((TM, D_MODEL), lambda i: (i, 0)),
            _const_spec((1, D_MODEL)),
            _const_spec((D_MODEL, D_FF)),
            _const_spec((D_FF, D_MODEL)),
        ],
        out_specs=pl.BlockSpec((TM, D_MODEL), lambda i: (i, 0)),
        out_shape=jax.ShapeDtypeStruct((rows, D_MODEL), F32),
        compiler_params=_cparams(("parallel",)),
        name="mlp",
    )(x, g.reshape(1, D_MODEL), w1, w2)


def _proj_body(sample, x_ref, gm_ref, win_ref, gq_ref, gkv_ref, wq_ref, wqs_ref,
               cq_ref, sq_ref, ck_ref, sk_ref, wa_ref, wb_ref, *outs):
    h = _rms(x_ref[...], gm_ref[...]).astype(BF16)
    proj = _dot(h, win_ref[...])
    cq = _rms(proj[:, :Q_LORA], gq_ref[...]).astype(BF16)
    ckv = _rms(proj[:, Q_LORA:Q_LORA + KV_LORA], gkv_ref[...])
    kpe = (proj[:, 768:896] * ck_ref[...] + proj[:, 896:1024] * sk_ref[...])
    ckv_ref, kpe_ref = outs[0], outs[1]
    ckv_ref[...] = ckv
    kpe_ref[...] = kpe[:, :D_ROPE]
    cq_tab = cq_ref[...]
    sq_tab = sq_ref[...]
    for p in range(N_HEADS // 2):
        sl = slice(p * 2 * HEAD_PAD, (p + 1) * 2 * HEAD_PAD)
        qp = _dot(cq, wq_ref[:, sl]) * cq_tab + _dot(cq, wqs_ref[:, sl]) * sq_tab
        qp = qp.astype(BF16)
        if sample:
            qext_ref = outs[2]
            for hh in range(2):
                head = 2 * p + hh
                qh = qp[:, hh * HEAD_PAD:(hh + 1) * HEAD_PAD]
                qext_ref[head] = _dot(qh, wa_ref[head])
        else:
            outs[2][:, sl] = qp
    if not sample:
        k_ref, vt_ref = outs[3], outs[4]
        ckv_b = ckv.astype(BF16)
        kpl = pltpu.roll(kpe, D_NOPE, axis=1)
        kpl2 = jnp.concatenate([kpl, kpl], axis=1)
        for p in range(N_HEADS // 2):
            sl = slice(p * 2 * HEAD_PAD, (p + 1) * 2 * HEAD_PAD)
            k_ref[:, sl] = (_dot(ckv_b, wa_ref[:, sl]) + kpl2).astype(BF16)
        vt_ref[...] = _dot_nt(wb_ref[...], ckv_b).astype(BF16)


def mla_proj(x, sample, gm, w, tabs):
    tm = TM_PROJ
    if sample:
        rows, row0, tab_map = NS_ROWS, NP_ROWS // tm, (lambda i: (0, 0))
    else:
        rows, row0, tab_map = NP_ROWS, 0, (lambda i: (i % (SEQ // tm), 0))
    cq, sq, ck, sk = tabs
    wa = w["wuk_ext"] if sample else w["wk"]
    wb = w["wv"]
    out_shape = [jax.ShapeDtypeStruct((rows, KV_LORA), F32),
                 jax.ShapeDtypeStruct((rows, D_ROPE), F32)]
    out_specs = [pl.BlockSpec((tm, KV_LORA), lambda i: (i, 0)),
                 pl.BlockSpec((tm, D_ROPE), lambda i: (i, 0))]
    if sample:
        out_shape.append(jax.ShapeDtypeStruct((N_HEADS, rows, QEXT), F32))
        out_specs.append(pl.BlockSpec((N_HEADS, tm, QEXT), lambda i: (0, i, 0)))
    else:
        out_shape += [jax.ShapeDtypeStruct((rows, N_HEADS * HEAD_PAD), BF16),
                      jax.ShapeDtypeStruct((rows, N_HEADS * HEAD_PAD), BF16),
                      jax.ShapeDtypeStruct((N_HEADS * D_V, rows), BF16)]
        out_specs += [pl.BlockSpec((tm, N_HEADS * HEAD_PAD), lambda i: (i, 0)),
                      pl.BlockSpec((tm, N_HEADS * HEAD_PAD), lambda i: (i, 0)),
                      pl.BlockSpec((N_HEADS * D_V, tm), lambda i: (0, i))]
    return pl.pallas_call(
        functools.partial(_proj_body, sample),
        grid=(rows // tm,),
        in_specs=[
            pl.BlockSpec((tm, D_MODEL), lambda i: (i + row0, 0)),
            _const_spec((1, D_MODEL)),
            _const_spec(w["win"].shape),
            _const_spec((1, Q_LORA)),
            _const_spec((1, KV_LORA)),
            _const_spec(w["wq"].shape),
            _const_spec(w["wqs"].shape),
            pl.BlockSpec((tm, 2 * HEAD_PAD), tab_map),
            pl.BlockSpec((tm, 2 * HEAD_PAD), tab_map),
            pl.BlockSpec((tm, LANES), tab_map),
            pl.BlockSpec((tm, LANES), tab_map),
            _const_spec(wa.shape),
            _const_spec(wb.shape),
        ],
        out_specs=out_specs,
        out_shape=out_shape,
        compiler_params=_cparams(("parallel",)),
        name="mla_proj_sample" if sample else "mla_proj_prompt",
    )(x, gm.reshape(1, D_MODEL), w["win"], w["gq"], w["gkv"], w["wq"], w["wqs"],
      cq, sq, ck, sk, wa, wb)


def _flash_body(q_ref, k_ref, vt_ref, o_ref):
    key = lax.broadcasted_iota(jnp.int32, (TQ, TQ), 0)
    qry = lax.broadcasted_iota(jnp.int32, (TQ, TQ), 1)
    for qi in range(SEQ // TQ):
        q0 = qi * TQ
        ots = []
        for hh in range(2):
            hs = slice(hh * HEAD_PAD, (hh + 1) * HEAD_PAD)
            vs = slice(hh * D_V, (hh + 1) * D_V)
            q = q_ref[q0:q0 + TQ, hs]
            sd = _dot_nt(k_ref[q0:q0 + TQ, hs], q)
            sd = jnp.where(key <= qry, sd, -jnp.inf)
            m = jnp.max(sd, axis=0, keepdims=True)
            if qi > 0:
                sm = _dot_nt(k_ref[0:q0, hs], q)
                m = jnp.maximum(m, jnp.max(sm, axis=0, keepdims=True))
            pd = jnp.exp2((sd - m) * EXP2_SCALE)
            l = jnp.sum(pd, axis=0, keepdims=True)
            ot = _dot(vt_ref[vs, q0:q0 + TQ], pd.astype(BF16))
            if qi > 0:
                pm = jnp.exp2((sm - m) * EXP2_SCALE)
                l = l + jnp.sum(pm, axis=0, keepdims=True)
                ot = ot + _dot(vt_ref[vs, 0:q0], pm.astype(BF16))
            ots.append(ot / l)
        o_ref[q0:q0 + TQ, :] = jnp.concatenate(ots, axis=0).T.astype(o_ref.dtype)


def flash_prompt(q, k, vt):
    return pl.pallas_call(
        _flash_body,
        grid=(BATCH, N_HEADS // 2),
        in_specs=[
            pl.BlockSpec((SEQ, 2 * HEAD_PAD), lambda b, p: (b, p)),
            pl.BlockSpec((SEQ, 2 * HEAD_PAD), lambda b, p: (b, p)),
            pl.BlockSpec((2 * D_V, SEQ), lambda b, p: (p, b)),
        ],
        out_specs=pl.BlockSpec((SEQ, 2 * D_V), lambda b, p: (b, p)),
        out_shape=jax.ShapeDtypeStruct((NP_ROWS, N_HEADS * D_V), BF16),
        compiler_params=_cparams(("parallel", "parallel")),
        name="flash_prompt",
    )(q, k, vt)


N_CHUNKS = N_PAGES // PAGES_PER_STEP
assert N_CHUNKS % 2 == 0 and N_CHUNKS * PAGES_PER_STEP == N_PAGES
QROWS = N_HEADS * DEC_SEQ


def _sattn_body(layer, pt_ref, q_ref, cnew_ref, knew_ref, ckv_hbm, kpet_hbm, o_ref,
                cbuf, kbuf, sem):
    n = PAGES_PER_STEP
    b = pl.program_id(0)

    def chunk_copies(req, c, slot):
        cps = []
        for k in range(n):
            page = pt_ref[req, c * n + k]
            rows = pl.ds(k * PAGE_SIZE, PAGE_SIZE)
            cps.append(pltpu.make_async_copy(ckv_hbm.at[layer, page], cbuf.at[slot, rows, :],
                                             sem.at[slot, 0]))
            cps.append(pltpu.make_async_copy(kpet_hbm.at[layer, page], kbuf.at[slot, :, rows],
                                             sem.at[slot, 1]))
        return cps

    def start(req, c, slot):
        for cp in chunk_copies(req, c, slot):
            cp.start()

    def wait(req, c, slot):
        for cp in chunk_copies(req, c, slot):
            cp.wait()

    @pl.when(b == 0)
    def _():
        start(0, 0, 0)

    q = q_ref[...].reshape(QROWS, QEXT)
    qlat = q[:, :KV_LORA].astype(BF16)
    qpe = q[:, KV_LORA:KV_LORA + D_ROPE].astype(BF16)
    pad = PAGE_SIZE - DEC_SEQ
    cn = jnp.concatenate([cnew_ref[...], jnp.zeros((pad, KV_LORA), F32)], axis=0).astype(BF16)
    kn = jnp.concatenate([knew_ref[...], jnp.zeros((pad, D_ROPE), F32)], axis=0).astype(BF16)
    s = _dot_nt(qlat, cn) + _dot_nt(qpe, kn)
    tok = lax.broadcasted_iota(jnp.int32, (QROWS, PAGE_SIZE), 0) % DEC_SEQ
    key = lax.broadcasted_iota(jnp.int32, (QROWS, PAGE_SIZE), 1)
    s = jnp.where(key <= tok, s, -jnp.inf)
    m = jnp.max(s, axis=-1, keepdims=True)
    p = jnp.exp2((s - m) * EXP2_SCALE)
    l = jnp.sum(p, axis=-1, keepdims=True)
    acc = _dot(p.astype(BF16), cn)

    for c in range(N_CHUNKS):
        slot = c % 2
        if c + 1 < N_CHUNKS:
            start(b, c + 1, 1 - slot)
        else:
            @pl.when(b + 1 < DEC_BATCH)
            def _():
                start(b + 1, 0, 1 - slot)
        wait(b, c, slot)
        ck = cbuf[slot].astype(BF16)
        kt = kbuf[slot].astype(BF16)
        s = _dot_nt(qlat, ck) + _dot(qpe, kt)
        m_new = jnp.maximum(m, jnp.max(s, axis=-1, keepdims=True))
        alpha = jnp.exp2((m - m_new) * EXP2_SCALE)
        p = jnp.exp2((s - m_new) * EXP2_SCALE)
        l = alpha * l + jnp.sum(p, axis=-1, keepdims=True)
        acc = alpha * acc + _dot(p.astype(BF16), ck)
        m = m_new

    o_ref[...] = (acc / l).reshape(N_HEADS, DEC_SEQ, KV_LORA)


def attn_sample(layer, page_table, qext, ckv_all, kpe_all, cache_ckv, cache_kpet):
    n = PAGES_PER_STEP
    grid_spec = pltpu.PrefetchScalarGridSpec(
        num_scalar_prefetch=1,
        grid=(DEC_BATCH,),
        in_specs=[
            pl.BlockSpec((N_HEADS, DEC_SEQ, QEXT), lambda b, pt: (0, b, 0)),
            pl.BlockSpec((DEC_SEQ, KV_LORA), lambda b, pt: (b, 0)),
            pl.BlockSpec((DEC_SEQ, D_ROPE), lambda b, pt: (b, 0)),
            pl.BlockSpec(memory_space=pl.ANY),
            pl.BlockSpec(memory_space=pl.ANY),
        ],
        out_specs=pl.BlockSpec((N_HEADS, DEC_SEQ, KV_LORA), lambda b, pt: (0, b, 0)),
        scratch_shapes=[
            pltpu.VMEM((2, n * PAGE_SIZE, KV_LORA), F32),
            pltpu.VMEM((2, D_ROPE, n * PAGE_SIZE), F32),
            pltpu.SemaphoreType.DMA((2, 2)),
        ],
    )
    return pl.pallas_call(
        functools.partial(_sattn_body, layer),
        grid_spec=grid_spec,
        out_shape=jax.ShapeDtypeStruct((N_HEADS, NS_ROWS, KV_LORA), F32),
        compiler_params=_cparams(("arbitrary",)),
        name="attn_sample",
    )(page_table, qext, ckv_all, kpe_all, cache_ckv, cache_kpet)


def _oproj_prompt_body(x_ref, o_ref, wo_ref, out_ref):
    out_ref[...] = x_ref[...] + _dot(o_ref[...], wo_ref[...])


def oproj_prompt(x, o, wo):
    return pl.pallas_call(
        _oproj_prompt_body,
        grid=(NP_ROWS // TM,),
        in_specs=[pl.BlockSpec((TM, D_MODEL), lambda i: (i, 0)),
                  pl.BlockSpec((TM, N_HEADS * D_V), lambda i: (i, 0)),
                  _const_spec(wo.shape)],
        out_specs=pl.BlockSpec((TM, D_MODEL), lambda i: (i, 0)),
        out_shape=jax.ShapeDtypeStruct((ROWS, D_MODEL), F32),
        input_output_aliases={0: 0},
        compiler_params=_cparams(("parallel",)),
        name="oproj_prompt",
    )(x, o, wo)


def _oproj_sample_body(x_ref, ol_ref, wuv_ref, wo_ref, out_ref):
    acc = x_ref[...]
    for p in range(N_HEADS // 2):
        pair = jnp.concatenate([ol_ref[2 * p], ol_ref[2 * p + 1]], axis=1).astype(BF16)
        o_pair = _dot(pair, wuv_ref[p]).astype(BF16)
        acc = acc + _dot(o_pair, wo_ref[p * 2 * D_V:(p + 1) * 2 * D_V, :])
    out_ref[...] = acc


def oproj_sample(x, o_lat, wuv_pair, wo):
    row0 = NP_ROWS // TM
    return pl.pallas_call(
        _oproj_sample_body,
        grid=(NS_ROWS // TM,),
        in_specs=[pl.BlockSpec((TM, D_MODEL), lambda i: (i + row0, 0)),
                  pl.BlockSpec((N_HEADS, TM, KV_LORA), lambda i: (0, i, 0)),
                  _const_spec(wuv_pair.shape),
                  _const_spec(wo.shape)],
        out_specs=pl.BlockSpec((TM, D_MODEL), lambda i: (i + row0, 0)),
        out_shape=jax.ShapeDtypeStruct((ROWS, D_MODEL), F32),
        input_output_aliases={0: 0},
        compiler_params=_cparams(("parallel",)),
        name="oproj_sample",
    )(x, o_lat, wuv_pair, wo)


S5_Q = DEC_SEQ
S5_BLK = LANES // GROUP_SIZE
S5_NBLK = N_GROUPS // S5_BLK
S5_HALF = S5_BLK * STATE_DIM
S5_CT = 64
S5_CHUNKS = SEQ // S5_Q


def _rms_chunked_body(x_ref, g_ref, o_ref):
    o_ref[...] = _rms(x_ref[...], g_ref[...]).reshape(o_ref.shape)


def rmsnorm_chunked(x, g):
    nci = S5_CHUNKS // S5_CT
    out = pl.pallas_call(
        _rms_chunked_body,
        grid=(BATCH, nci),
        in_specs=[pl.BlockSpec((S5_CT * S5_Q, D_MODEL), lambda b, c: (b * nci + c, 0)),
                  _const_spec((1, D_MODEL))],
        out_specs=pl.BlockSpec((S5_CT, None, S5_Q, D_MODEL), lambda b, c: (c, b, 0, 0)),
        out_shape=jax.ShapeDtypeStruct((S5_CHUNKS, BATCH, S5_Q, D_MODEL), F32),
        compiler_params=_cparams(("parallel", "parallel")),
        name="rmsnorm_chunked",
    )(x, g.reshape(1, D_MODEL))
    return out.reshape(NP_ROWS, D_MODEL)


def _s5_body(n_chunks, nb, h_ref, t_ref, m_ref, n_ref, a_ref, h0_ref, y_ref, hf_ref,
             s_s, p_s, st_s):
    rows = n_chunks * nb

    @pl.when(pl.program_id(1) == 0)
    def _():
        st_s[...] = h0_ref[...]

    u = jnp.concatenate([h_ref[pl.ds(k, rows, stride=S5_Q), :].astype(BF16)
                         for k in range(S5_Q)], axis=1)
    s_s[...] = _dot(u, m_ref[...])
    are = a_ref[:, :S5_HALF]
    aim = a_ref[:, S5_HALF:]

    def step(c, h):
        r0 = pl.multiple_of(c * nb, nb)
        p_s[pl.ds(r0, nb), :] = h
        s = s_s[pl.ds(r0, nb), :]
        hre, him = h[:, :S5_HALF], h[:, S5_HALF:]
        nre = are * hre - aim * him + s[:, :S5_HALF]
        nim = are * him + aim * hre + s[:, S5_HALF:]
        return jnp.concatenate([nre, nim], axis=1)

    h = lax.fori_loop(0, n_chunks, step, st_s[...])
    st_s[...] = h
    hf_ref[...] = h
    y = _dot(u, t_ref[...]) + _dot(p_s[...].astype(BF16), n_ref[...])
    for l in range(S5_Q):
        y_ref[pl.ds(l, rows, stride=S5_Q), :] = y[:, l * LANES:(l + 1) * LANES]


def s5_scan(h, tabs, h0, n_chunks, nb):
    tbig, mbig, nbig, abig = tabs
    rows = n_chunks * nb
    tile = rows * S5_Q
    width = S5_Q * LANES

    def bspec(shape):
        return pl.BlockSpec((None,) + shape, lambda j, c: (j, 0, 0))

    return pl.pallas_call(
        functools.partial(_s5_body, n_chunks, nb),
        grid=(S5_NBLK, h.shape[0] // tile),
        in_specs=[pl.BlockSpec((tile, LANES), lambda j, c: (c, j)),
                  bspec((width, width)), bspec((width, 2 * S5_HALF)), bspec((2 * S5_HALF, width)),
                  bspec((1, 2 * S5_HALF)), bspec((nb, 2 * S5_HALF))],
        out_specs=[pl.BlockSpec((tile, LANES), lambda j, c: (c, j)), bspec((nb, 2 * S5_HALF))],
        out_shape=[jax.ShapeDtypeStruct(h.shape, F32),
                   jax.ShapeDtypeStruct((S5_NBLK, nb, 2 * S5_HALF), F32)],
        scratch_shapes=[pltpu.VMEM((rows, 2 * S5_HALF), F32), pltpu.VMEM((rows, 2 * S5_HALF), F32),
                        pltpu.VMEM((nb, 2 * S5_HALF), F32)],
        compiler_params=_cparams(("parallel", "arbitrary")),
        name="s5_scan",
    )(h, tbig, mbig, nbig, abig, h0)


def _glu_body(x_ref, h_ref, y_ref, d_ref, w_ref, o_ref):
    x = x_ref[...]
    y = y_ref[...].reshape(x.shape) + d_ref[...] * h_ref[...].reshape(x.shape)
    gl = jax.nn.gelu(y).astype(BF16)
    ab = _dot(gl, w_ref[...])
    o_ref[...] = x + ab[:, :D_MODEL] * jax.nn.sigmoid(ab[:, D_MODEL:])


def glu(x, h, y, d, w, chunked):
    if chunked:
        nci = S5_CHUNKS // S5_CT
        grid = (BATCH, nci)
        x_spec = pl.BlockSpec((S5_CT * S5_Q, D_MODEL), lambda b, c: (b * nci + c, 0))
        hy_spec = pl.BlockSpec((S5_CT, None, S5_Q, D_MODEL), lambda b, c: (c, b, 0, 0))
        h = h.reshape(S5_CHUNKS, BATCH, S5_Q, D_MODEL)
        y = y.reshape(S5_CHUNKS, BATCH, S5_Q, D_MODEL)
    else:
        blk0 = NP_ROWS // TM
        grid = (NS_ROWS // TM,)
        x_spec = pl.BlockSpec((TM, D_MODEL), lambda i: (i + blk0, 0))
        hy_spec = pl.BlockSpec((TM, D_MODEL), lambda i: (i, 0))
    return pl.pallas_call(
        _glu_body,
        grid=grid,
        in_specs=[x_spec, hy_spec, hy_spec, _const_spec((1, D_MODEL)), _const_spec(w.shape)],
        out_specs=x_spec,
        out_shape=jax.ShapeDtypeStruct((ROWS, D_MODEL), F32),
        input_output_aliases={0: 0},
        compiler_params=_cparams(("parallel",) * len(grid)),
        name="glu_prompt" if chunked else "glu_sample",
    )(x, h, y, d.reshape(1, D_MODEL), w)


def _prep_mla(w_in, g_q, g_kv, w_q_up, w_kv_up, w_o):
    z = lambda *s: jnp.zeros(s, F32)
    wk_pe = w_in[:, Q_LORA + KV_LORA:]
    half = D_ROPE // 2
    win = jnp.concatenate(
        [w_in[:, :Q_LORA + KV_LORA], wk_pe, z(D_MODEL, LANES - D_ROPE),
         -wk_pe[:, half:], wk_pe[:, :half], z(D_MODEL, LANES - D_ROPE)], axis=1)
    wq3 = w_q_up.reshape(Q_LORA, N_HEADS, D_NOPE + D_ROPE)
    nope, x1, x2 = wq3[..., :D_NOPE], wq3[..., D_NOPE:D_NOPE + half], wq3[..., D_NOPE + half:]
    zpad = z(Q_LORA, N_HEADS, HEAD_PAD - D_NOPE - D_ROPE)
    wq = jnp.concatenate([nope, x1, x2, zpad], axis=-1).reshape(Q_LORA, N_HEADS * HEAD_PAD)
    wqs = jnp.concatenate([z(Q_LORA, N_HEADS, D_NOPE), -x2, x1, zpad], axis=-1)
    wqs = wqs.reshape(Q_LORA, N_HEADS * HEAD_PAD)
    wkv3 = w_kv_up.reshape(KV_LORA, N_HEADS, D_NOPE + D_V)
    w_uk, w_uv = wkv3[..., :D_NOPE], wkv3[..., D_NOPE:]
    wk = jnp.concatenate([w_uk, z(KV_LORA, N_HEADS, HEAD_PAD - D_NOPE)], axis=-1)
    wk = wk.reshape(KV_LORA, N_HEADS * HEAD_PAD)
    wv = w_uv.reshape(KV_LORA, N_HEADS * D_V).T
    wuk_ext = jnp.zeros((N_HEADS, HEAD_PAD, QEXT), F32)
    wuk_ext = wuk_ext.at[:, :D_NOPE, :KV_LORA].set(jnp.transpose(w_uk, (1, 2, 0)))
    wuk_ext = wuk_ext.at[:, D_NOPE:D_NOPE + D_ROPE, KV_LORA:KV_LORA + D_ROPE].set(
        jnp.broadcast_to(jnp.eye(D_ROPE, dtype=F32), (N_HEADS, D_ROPE, D_ROPE)))
    uv = jnp.transpose(w_uv, (1, 0, 2)).reshape(N_HEADS // 2, 2, KV_LORA, D_V)
    wuv_pair = jnp.zeros((N_HEADS // 2, 2 * KV_LORA, 2 * D_V), F32)
    wuv_pair = wuv_pair.at[:, :KV_LORA, :D_V].set(uv[:, 0])
    wuv_pair = wuv_pair.at[:, KV_LORA:, D_V:].set(uv[:, 1])
    return dict(win=win.astype(BF16), gq=g_q.reshape(1, Q_LORA), gkv=g_kv.reshape(1, KV_LORA),
                wq=wq.astype(BF16), wqs=wqs.astype(BF16), wk=wk.astype(BF16), wv=wv.astype(BF16),
                wuk_ext=wuk_ext.astype(BF16), wuv_pair=wuv_pair.astype(BF16), wo=w_o.astype(BF16))


def _rope_tabs(pos, reps):
    half = D_ROPE // 2
    inv = ROPE_THETA ** (-jnp.arange(half, dtype=F32) / half)
    ang = pos.astype(F32)[:, None] * inv[None, :]
    cos, sin = jnp.cos(ang), jnp.sin(ang)
    n = pos.shape[0]
    one, zero = jnp.ones((n, D_NOPE), F32), jnp.zeros((n, D_NOPE), F32)
    zpad = jnp.zeros((n, HEAD_PAD - D_NOPE - D_ROPE), F32)
    cq = jnp.concatenate([one, cos, cos, zpad], axis=1)
    sq = jnp.concatenate([zero, sin, sin, zpad], axis=1)
    kpad = jnp.zeros((n, LANES - D_ROPE), F32)
    ck = jnp.concatenate([cos, cos, kpad], axis=1)
    sk = jnp.concatenate([sin, sin, kpad], axis=1)
    cq, sq = jnp.tile(cq, (reps, 2)), jnp.tile(sq, (reps, 2))
    return cq, sq, jnp.tile(ck, (reps, 1)), jnp.tile(sk, (reps, 1))


def _prep_s5(a_re, a_im, log_dt, b_re, b_im, c_re, c_im, q):
    hi = lax.Precision.HIGHEST
    dt = jnp.exp(log_dt)[:, None]
    dre, dim = dt * a_re, dt * a_im
    mag = jnp.exp(dre)
    abr, abi = mag * jnp.cos(dim), mag * jnp.sin(dim)
    den = a_re * a_re + a_im * a_im
    xr, xi = abr - 1.0, abi
    fr, fi = (xr * a_re + xi * a_im) / den, (xi * a_re - xr * a_im) / den
    bbr = fr[..., None] * b_re - fi[..., None] * b_im
    bbi = fr[..., None] * b_im + fi[..., None] * b_re
    j = jnp.arange(q + 1, dtype=F32)[:, None, None]
    pmag = jnp.exp(j * dre)
    pwr, pwi = pmag * jnp.cos(j * dim), pmag * jnp.sin(j * dim)
    pbr = pwr[..., None] * bbr - pwi[..., None] * bbi
    pbi = pwr[..., None] * bbi + pwi[..., None] * bbr
    kk = (jnp.einsum("gtp,jgps->jgst", c_re, pbr[:q], precision=hi)
          - jnp.einsum("gtp,jgps->jgst", c_im, pbi[:q], precision=hi))
    lag = jnp.arange(q)[None, :] - jnp.arange(q)[:, None]
    tk = kk[jnp.clip(lag, 0, q - 1)]
    tk = jnp.where((lag >= 0)[:, :, None, None, None], tk, 0.0)
    g = a_re.shape[0]
    t = jnp.transpose(tk, (2, 0, 3, 1, 4)).reshape(g, q * GROUP_SIZE, q * GROUP_SIZE)
    mre = jnp.transpose(pbr[:q][::-1], (1, 0, 3, 2)).reshape(g, q * GROUP_SIZE, STATE_DIM)
    mim = jnp.transpose(pbi[:q][::-1], (1, 0, 3, 2)).reshape(g, q * GROUP_SIZE, STATE_DIM)
    ncr = pwr[1:, :, None, :] * c_re[None] - pwi[1:, :, None, :] * c_im[None]
    nci = pwr[1:, :, None, :] * c_im[None] + pwi[1:, :, None, :] * c_re[None]
    nre = jnp.transpose(ncr, (1, 3, 0, 2)).reshape(g, STATE_DIM, q * GROUP_SIZE)
    nim = jnp.transpose(nci, (1, 3, 0, 2)).reshape(g, STATE_DIM, q * GROUP_SIZE)
    are, aim = pwr[q][:, None, :], pwi[q][:, None, :]
    return t, mre, mim, nre, nim, are, aim


def _s5_block_tables(p):
    t, mre, mim, nre, nim, are, aim = _prep_s5(*p, S5_Q)
    q = S5_Q
    eye = jnp.eye(S5_BLK, dtype=F32)
    t6 = t.reshape(S5_NBLK, S5_BLK, q, GROUP_SIZE, q, GROUP_SIZE)
    tb = (jnp.transpose(t6, (0, 2, 1, 3, 4, 5))[:, :, :, :, :, None, :]
          * eye[None, None, :, None, None, :, None])
    tbig = tb.reshape(S5_NBLK, q * LANES, q * LANES)

    def mblock(m):
        m5 = m.reshape(S5_NBLK, S5_BLK, q, GROUP_SIZE, STATE_DIM)
        mb = (jnp.transpose(m5, (0, 2, 1, 3, 4))[:, :, :, :, None, :]
              * eye[None, None, :, None, :, None])
        return mb.reshape(S5_NBLK, q * LANES, S5_HALF)

    def nblock(n):
        n5 = n.reshape(S5_NBLK, S5_BLK, STATE_DIM, q, GROUP_SIZE)
        nb_ = n5[:, :, :, :, None, :] * eye[None, :, None, None, :, None]
        return nb_.reshape(S5_NBLK, S5_HALF, q * LANES)

    mbig = jnp.concatenate([mblock(mre), mblock(mim)], axis=2)
    nbig = jnp.concatenate([nblock(nre), -nblock(nim)], axis=1)
    abig = jnp.concatenate([are.reshape(S5_NBLK, 1, S5_HALF), aim.reshape(S5_NBLK, 1, S5_HALF)], axis=2)
    return tbig.astype(BF16), mbig.astype(BF16), nbig.astype(BF16), abig


def _state_to_blocks(re, im):
    nb = re.shape[0]
    f = lambda a: jnp.transpose(a.reshape(nb, S5_NBLK, S5_HALF), (1, 0, 2))
    return jnp.concatenate([f(re), f(im)], axis=2)


def _blocks_to_state(hf):
    nb = hf.shape[1]
    f = lambda a: jnp.transpose(a, (1, 0, 2)).reshape(nb, N_GROUPS, STATE_DIM)
    return f(hf[..., :S5_HALF]), f(hf[..., S5_HALF:])


def _s5_layer(x, g_mix, p, d, w_glu, h0re, h0im):
    tabs = _s5_block_tables(p)
    hp = rmsnorm_chunked(x, g_mix)
    hs = rmsnorm(x, g_mix, F32, NP_ROWS, NS_ROWS)
    zero = jnp.zeros((S5_NBLK, BATCH, 2 * S5_HALF), F32)
    yp, hfp = s5_scan(hp, tabs, zero, S5_CT, BATCH)
    ys, hfs = s5_scan(hs, tabs, _state_to_blocks(h0re, h0im), 1, DEC_BATCH)
    x = glu(x, hp, yp, d, w_glu, True)
    x = glu(x, hs, ys, d, w_glu, False)
    return (x,) + _blocks_to_state(hfp) + _blocks_to_state(hfs)


def kernel(x_prompt, x_sample, cache_ckv, cache_kpe, state_ssm_re, state_ssm_im, page_table, norm_mix, norm_mlp, norm_final, mla_w_in, mla_q_norm, mla_kv_norm, mla_w_q_up, mla_w_kv_up, mla_w_o, ssm_a_re, ssm_a_im, ssm_log_dt, ssm_b_re, ssm_b_im, ssm_c_re, ssm_c_im, ssm_d, ssm_w_glu, mlp_w1, mlp_w2):
    x = jnp.concatenate([x_prompt.reshape(NP_ROWS, D_MODEL), x_sample.reshape(NS_ROWS, D_MODEL)], axis=0)
    tabs_p = _rope_tabs(jnp.arange(SEQ, dtype=jnp.int32), 1)
    tabs_s = _rope_tabs(PAST_LEN + jnp.arange(DEC_SEQ, dtype=jnp.int32), TM_PROJ // DEC_SEQ)
    cache_kpet = jnp.swapaxes(cache_kpe, 2, 3)
    p_ckv, p_kpe, p_re, p_im = [], [], [], []
    s_ckv, s_kpe, s_re, s_im = [], [], [], []
    for i in range(DEPTH):
        j = i // 2
        if i % 2 == 0:
            w = _prep_mla(mla_w_in[j], mla_q_norm[j], mla_kv_norm[j], mla_w_q_up[j],
                          mla_w_kv_up[j], mla_w_o[j])
            ckv_p, kpe_p, q_p, k_p, v_p = mla_proj(x, False, norm_mix[i], w, tabs_p)
            ckv_s, kpe_s, qext = mla_proj(x, True, norm_mix[i], w, tabs_s)
            o_p = flash_prompt(q_p, k_p, v_p)
            o_lat = attn_sample(j, page_table, qext, ckv_s, kpe_s, cache_ckv, cache_kpet)
            x = oproj_prompt(x, o_p, w["wo"])
            x = oproj_sample(x, o_lat, w["wuv_pair"], w["wo"])
            p_ckv.append(ckv_p.reshape(BATCH, SEQ, KV_LORA))
            p_kpe.append(kpe_p.reshape(BATCH, SEQ, D_ROPE))
            s_ckv.append(ckv_s.reshape(DEC_BATCH, DEC_SEQ, KV_LORA))
            s_kpe.append(kpe_s.reshape(DEC_BATCH, DEC_SEQ, D_ROPE))
        else:
            p = (ssm_a_re[j], ssm_a_im[j], ssm_log_dt[j], ssm_b_re[j], ssm_b_im[j],
                 ssm_c_re[j], ssm_c_im[j])
            x, pre, pim, sre, sim = _s5_layer(x, norm_mix[i], p, ssm_d[j], ssm_w_glu[j].astype(BF16),
                                              state_ssm_re[j], state_ssm_im[j])
            p_re.append(pre); p_im.append(pim); s_re.append(sre); s_im.append(sim)
        x = mlp(x, norm_mlp[i], mlp_w1[i].astype(BF16), mlp_w2[i].astype(BF16))
    y_prompt = rmsnorm(x, norm_final, F32, 0, NP_ROWS).reshape(BATCH, SEQ, D_MODEL)
    y_sample = rmsnorm(x, norm_final, F32, NP_ROWS, NS_ROWS).reshape(DEC_BATCH, DEC_SEQ, D_MODEL)
    return (y_prompt, y_sample,
            jnp.stack(p_ckv), jnp.stack(p_kpe), jnp.stack(p_re), jnp.stack(p_im),
            jnp.stack(s_ckv), jnp.stack(s_kpe), jnp.stack(s_re), jnp.stack(s_im))
```

```python
import functools
import math

import jax
import jax.numpy as jnp
from jax import lax
from jax.experimental import pallas as pl
from jax.experimental.pallas import tpu as pltpu

F32 = jnp.float32
BF16 = jnp.bfloat16

D_MODEL = 1024
BATCH = 8
SEQ = 2048
DEPTH = 4
DEC_BATCH = 128
DEC_SEQ = 8
PAST_LEN = 8192
PAGE_SIZE = 128
N_PAGES = PAST_LEN // PAGE_SIZE
N_HEADS = 16
D_NOPE = 64
D_ROPE = 32
D_V = 64
Q_LORA = 512
KV_LORA = 256
ROPE_THETA = 10000.0
GROUP_SIZE = 16
N_GROUPS = D_MODEL // GROUP_SIZE
STATE_DIM = 64
D_FF = 4 * D_MODEL
EPS = 1e-6
SCALE = (D_NOPE + D_ROPE) ** -0.5
EXP2_SCALE = SCALE * math.log2(math.e)

NP_ROWS = BATCH * SEQ
NS_ROWS = DEC_BATCH * DEC_SEQ
ROWS = NP_ROWS + NS_ROWS

LANES = 128
HEAD_PAD = 128
QEXT = KV_LORA + LANES
VMEM_LIMIT = 52 * 1024 * 1024

TM = 512
TM_PROJ = 256
TQ = 256
PAGES_PER_STEP = 32


def _cparams(sem):
    return pltpu.CompilerParams(dimension_semantics=sem, vmem_limit_bytes=VMEM_LIMIT)


def _rms(x, g):
    return x * lax.rsqrt(jnp.mean(x * x, axis=-1, keepdims=True) + EPS) * g


def _dot(a, b):
    return jnp.dot(a, b, preferred_element_type=F32)


def _dot_nt(a, b):
    return lax.dot_general(a, b, (((1,), (1,)), ((), ())), preferred_element_type=F32)


def _const_spec(shape):
    nd = len(shape)
    return pl.BlockSpec(shape, lambda *_: (0,) * nd)


def _rms_body(x_ref, g_ref, o_ref):
    o_ref[...] = _rms(x_ref[...], g_ref[...]).astype(o_ref.dtype)


def rmsnorm(x, g, dtype, row0=0, rows=ROWS):
    blk0 = row0 // TM
    return pl.pallas_call(
        _rms_body,
        grid=(rows // TM,),
        in_specs=[pl.BlockSpec((TM, D_MODEL), lambda i: (i + blk0, 0)), _const_spec((1, D_MODEL))],
        out_specs=pl.BlockSpec((TM, D_MODEL), lambda i: (i, 0)),
        out_shape=jax.ShapeDtypeStruct((rows, D_MODEL), dtype),
        compiler_params=_cparams(("parallel",)),
        name="rmsnorm",
    )(x, g.reshape(1, D_MODEL))


FF_CHUNK = 1024


def _mlp_body(x_ref, g_ref, w1_ref, w2_ref, o_ref):
    x = x_ref[...]
    h = _rms(x, g_ref[...]).astype(BF16)
    acc = x
    for c in range(D_FF // FF_CHUNK):
        sl = slice(c * FF_CHUNK, (c + 1) * FF_CHUNK)
        z = jnp.maximum(_dot(h, w1_ref[:, sl]), 0.0)
        acc = acc + _dot((z * z).astype(BF16), w2_ref[sl, :])
    o_ref[...] = acc


def mlp(x, g, w1, w2):
    rows = x.shape[0]
    return pl.pallas_call(
        _mlp_body,
        grid=(rows // TM,),
        in_specs=[
            pl.BlockSpec((TM, D_MODEL), lambda i: (i, 0)),
            _const_spec((1, D_MODEL)),
            _const_spec((D_MODEL, D_FF)),
            _const_spec((D_FF, D_MODEL)),
        ],
        out_specs=pl.BlockSpec((TM, D_MODEL), lambda i: (i, 0)),
        out_shape=jax.ShapeDtypeStruct((rows, D_MODEL), F32),
        compiler_params=_cparams(("parallel",)),
        name="mlp",
    )(x, g.reshape(1, D_MODEL), w1, w2)


def _proj_body(sample, x_ref, gm_ref, win_ref, gq_ref, gkv_ref, wq_ref, wqs_ref,
               cq_ref, sq_ref, ck_ref, sk_ref, wa_ref, wb_ref, *outs):
    h = _rms(x_ref[...], gm_ref[...]).astype(BF16)
    proj = _dot(h, win_ref[...])
    cq = _rms(proj[:, :Q_LORA], gq_ref[...]).astype(BF16)
    ckv = _rms(proj[:, Q_LORA:Q_LORA + KV_LORA], gkv_ref[...])
    kpe = (proj[:, 768:896] * ck_ref[...] + proj[:, 896:1024] * sk_ref[...])
    ckv_ref, kpe_ref = outs[0], outs[1]
    ckv_ref[...] = ckv
    kpe_ref[...] = kpe[:, :D_ROPE]
    cq_tab = cq_ref[...]
    sq_tab = sq_ref[...]
    for p in range(N_HEADS // 2):
        sl = slice(p * 2 * HEAD_PAD, (p + 1) * 2 * HEAD_PAD)
        qp = _dot(cq, wq_ref[:, sl]) * cq_tab + _dot(cq, wqs_ref[:, sl]) * sq_tab
        qp = qp.astype(BF16)
        if sample:
            qext_ref = outs[2]
            for hh in range(2):
                head = 2 * p + hh
                qh = qp[:, hh * HEAD_PAD:(hh + 1) * HEAD_PAD]
                qext_ref[head] = _dot(qh, wa_ref[head])
        else:
            outs[2][:, sl] = qp
    if not sample:
        k_ref, vt_ref = outs[3], outs[4]
        ckv_b = ckv.astype(BF16)
        kpl = pltpu.roll(kpe, D_NOPE, axis=1)
        kpl2 = jnp.concatenate([kpl, kpl], axis=1)
        for p in range(N_HEADS // 2):
            sl = slice(p * 2 * HEAD_PAD, (p + 1) * 2 * HEAD_PAD)
            k_ref[:, sl] = (_dot(ckv_b, wa_ref[:, sl]) + kpl2).astype(BF16)
        vt_ref[...] = _dot_nt(wb_ref[...], ckv_b).astype(BF16)


def mla_proj(x, sample, gm, w, tabs):
    tm = TM_PROJ
    if sample:
        rows, row0, tab_map = NS_ROWS, NP_ROWS // tm, (lambda i: (0, 0))
    else:
        rows, row0, tab_map = NP_ROWS, 0, (lambda i: (i % (SEQ // tm), 0))
    cq, sq, ck, sk = tabs
    wa = w["wuk_ext"] if sample else w["wk"]
    wb = w["wv"]
    out_shape = [jax.ShapeDtypeStruct((rows, KV_LORA), F32),
                 jax.ShapeDtypeStruct((rows, D_ROPE), F32)]
    out_specs = [pl.BlockSpec((tm, KV_LORA), lambda i: (i, 0)),
                 pl.BlockSpec((tm, D_ROPE), lambda i: (i, 0))]
    if sample:
        out_shape.append(jax.ShapeDtypeStruct((N_HEADS, rows, QEXT), F32))
        out_specs.append(pl.BlockSpec((N_HEADS, tm, QEXT), lambda i: (0, i, 0)))
    else:
        out_shape += [jax.ShapeDtypeStruct((rows, N_HEADS * HEAD_PAD), BF16),
                      jax.ShapeDtypeStruct((rows, N_HEADS * HEAD_PAD), BF16),
                      jax.ShapeDtypeStruct((N_HEADS * D_V, rows), BF16)]
        out_specs += [pl.BlockSpec((tm, N_HEADS * HEAD_PAD), lambda i: (i, 0)),
                      pl.BlockSpec((tm, N_HEADS * HEAD_PAD), lambda i: (i, 0)),
                      pl.BlockSpec((N_HEADS * D_V, tm), lambda i: (0, i))]
    return pl.pallas_call(
        functools.partial(_proj_body, sample),
        grid=(rows // tm,),
        in_specs=[
            pl.BlockSpec((tm, D_MODEL), lambda i: (i + row0, 0)),
            _const_spec((1, D_MODEL)),
            _const_spec(w["win"].shape),
            _const_spec((1, Q_LORA)),
            _const_spec((1, KV_LORA)),
            _const_spec(w["wq"].shape),
            _const_spec(w["wqs"].shape),
            pl.BlockSpec((tm, 2 * HEAD_PAD), tab_map),
            pl.BlockSpec((tm, 2 * HEAD_PAD), tab_map),
            pl.BlockSpec((tm, LANES), tab_map),
            pl.BlockSpec((tm, LANES), tab_map),
            _const_spec(wa.shape),
            _const_spec(wb.shape),
        ],
        out_specs=out_specs,
        out_shape=out_shape,
        compiler_params=_cparams(("parallel",)),
        name="mla_proj_sample" if sample else "mla_proj_prompt",
    )(x, gm.reshape(1, D_MODEL), w["win"], w["gq"], w["gkv"], w["wq"], w["wqs"],
      cq, sq, ck, sk, wa, wb)


def _flash_body(q_ref, k_ref, vt_ref, o_ref):
    key = lax.broadcasted_iota(jnp.int32, (TQ, TQ), 0)
    qry = lax.broadcasted_iota(jnp.int32, (TQ, TQ), 1)
    for qi in range(SEQ // TQ):
        q0 = qi * TQ
        ots = []
        for hh in range(2):
            hs = slice(hh * HEAD_PAD, (hh + 1) * HEAD_PAD)
            vs = slice(hh * D_V, (hh + 1) * D_V)
            q = q_ref[q0:q0 + TQ, hs]
            sd = _dot_nt(k_ref[q0:q0 + TQ, hs], q)
            sd = jnp.where(key <= qry, sd, -jnp.inf)
            m = jnp.max(sd, axis=0, keepdims=True)
            if qi > 0:
                sm = _dot_nt(k_ref[0:q0, hs], q)
                m = jnp.maximum(m, jnp.max(sm, axis=0, keepdims=True))
            pd = jnp.exp2((sd - m) * EXP2_SCALE)
            l = jnp.sum(pd, axis=0, keepdims=True)
            ot = _dot(vt_ref[vs, q0:q0 + TQ], pd.astype(BF16))
            if qi > 0:
                pm = jnp.exp2((sm - m) * EXP2_SCALE)
                l = l + jnp.sum(pm, axis=0, keepdims=True)
                ot = ot + _dot(vt_ref[vs, 0:q0], pm.astype(BF16))
            ots.append(ot / l)
        o_ref[q0:q0 + TQ, :] = jnp.concatenate(ots, axis=0).T.astype(o_ref.dtype)


def flash_prompt(q, k, vt):
    return pl.pallas_call(
        _flash_body,
        grid=(BATCH, N_HEADS // 2),
        in_specs=[
            pl.BlockSpec((SEQ, 2 * HEAD_PAD), lambda b, p: (b, p)),
            pl.BlockSpec((SEQ, 2 * HEAD_PAD), lambda b, p: (b, p)),
            pl.BlockSpec((2 * D_V, SEQ), lambda b, p: (p, b)),
        ],
        out_specs=pl.BlockSpec((SEQ, 2 * D_V), lambda b, p: (b, p)),
        out_shape=jax.ShapeDtypeStruct((NP_ROWS, N_HEADS * D_V), BF16),
        compiler_params=_cparams(("parallel", "parallel")),
        name="flash_prompt",
    )(q, k, vt)


N_CHUNKS = N_PAGES // PAGES_PER_STEP
assert N_CHUNKS % 2 == 0 and N_CHUNKS * PAGES_PER_STEP == N_PAGES
SATTN_KEYS = PAGES_PER_STEP * PAGE_SIZE
QROWS = N_HEADS * DEC_SEQ


def _sattn_body(layer, pt_ref, q_ref, cnew_ref, knew_ref, ckv_hbm, kpet_hbm, o_ref,
                cbuf, kbuf, sem):
    n = PAGES_PER_STEP
    b = pl.program_id(0)

    def chunk_copies(req, c, slot):
        cps = []
        for k in range(n):
            page = pt_ref[req, c * n + k]
            rows = pl.ds(k * PAGE_SIZE, PAGE_SIZE)
            cps.append(pltpu.make_async_copy(ckv_hbm.at[layer, page], cbuf.at[slot, rows, :],
                                             sem.at[slot, 0]))
            cps.append(pltpu.make_async_copy(kpet_hbm.at[layer, page], kbuf.at[slot, :, rows],
                                             sem.at[slot, 1]))
        return cps

    def start(req, c, slot):
        for cp in chunk_copies(req, c, slot):
            cp.start()

    def wait(req, c, slot):
        for cp in chunk_copies(req, c, slot):
            cp.wait()

    @pl.when(b == 0)
    def _():
        start(0, 0, 0)

    q = q_ref[...].reshape(QROWS, QEXT)
    qlat = q[:, :KV_LORA].astype(BF16)
    qpe = q[:, KV_LORA:KV_LORA + D_ROPE].astype(BF16)
    pad = PAGE_SIZE - DEC_SEQ
    cn = jnp.concatenate([cnew_ref[...], jnp.zeros((pad, KV_LORA), F32)], axis=0).astype(BF16)
    kn = jnp.concatenate([knew_ref[...], jnp.zeros((pad, D_ROPE), F32)], axis=0).astype(BF16)
    s = _dot_nt(qlat, cn) + _dot_nt(qpe, kn)
    tok = lax.broadcasted_iota(jnp.int32, (QROWS, PAGE_SIZE), 0) % DEC_SEQ
    key = lax.broadcasted_iota(jnp.int32, (QROWS, PAGE_SIZE), 1)
    s = jnp.where(key <= tok, s, -jnp.inf)
    m = jnp.max(s, axis=-1, keepdims=True)
    p = jnp.exp2((s - m) * EXP2_SCALE)
    l = jnp.sum(p, axis=-1, keepdims=True)
    acc = _dot(p.astype(BF16), cn)

    for c in range(N_CHUNKS):
        slot = c % 2
        if c + 1 < N_CHUNKS:
            start(b, c + 1, 1 - slot)
        else:
            @pl.when(b + 1 < DEC_BATCH)
            def _():
                start(b + 1, 0, 1 - slot)
        wait(b, c, slot)
        for sb in range(n * PAGE_SIZE // SATTN_KEYS):
            keys = slice(sb * SATTN_KEYS, (sb + 1) * SATTN_KEYS)
            ck = cbuf[slot, keys, :].astype(BF16)
            kt = kbuf[slot, :, keys].astype(BF16)
            s = _dot_nt(qlat, ck) + _dot(qpe, kt)
            m_new = jnp.maximum(m, jnp.max(s, axis=-1, keepdims=True))
            alpha = jnp.exp2((m - m_new) * EXP2_SCALE)
            p = jnp.exp2((s - m_new) * EXP2_SCALE)
            l = alpha * l + jnp.sum(p, axis=-1, keepdims=True)
            acc = alpha * acc + _dot(p.astype(BF16), ck)
            m = m_new

    o_ref[...] = (acc / l).reshape(N_HEADS, DEC_SEQ, KV_LORA)


def attn_sample(layer, page_table, qext, ckv_all, kpe_all, cache_ckv, cache_kpet):
    n = PAGES_PER_STEP
    grid_spec = pltpu.PrefetchScalarGridSpec(
        num_scalar_prefetch=1,
        grid=(DEC_BATCH,),
        in_specs=[
            pl.BlockSpec((N_HEADS, DEC_SEQ, QEXT), lambda b, pt: (0, b, 0)),
            pl.BlockSpec((DEC_SEQ, KV_LORA), lambda b, pt: (b, 0)),
            pl.BlockSpec((DEC_SEQ, D_ROPE), lambda b, pt: (b, 0)),
            pl.BlockSpec(memory_space=pl.ANY),
            pl.BlockSpec(memory_space=pl.ANY),
        ],
        out_specs=pl.BlockSpec((N_HEADS, DEC_SEQ, KV_LORA), lambda b, pt: (0, b, 0)),
        scratch_shapes=[
            pltpu.VMEM((2, n * PAGE_SIZE, KV_LORA), F32),
            pltpu.VMEM((2, D_ROPE, n * PAGE_SIZE), F32),
            pltpu.SemaphoreType.DMA((2, 2)),
        ],
    )
    return pl.pallas_call(
        functools.partial(_sattn_body, layer),
        grid_spec=grid_spec,
        out_shape=jax.ShapeDtypeStruct((N_HEADS, NS_ROWS, KV_LORA), F32),
        compiler_params=_cparams(("arbitrary",)),
        name="attn_sample",
    )(page_table, qext, ckv_all, kpe_all, cache_ckv, cache_kpet)


def _oproj_prompt_body(x_ref, o_ref, wo_ref, out_ref):
    out_ref[...] = x_ref[...] + _dot(o_ref[...], wo_ref[...])


def oproj_prompt(x, o, wo):
    return pl.pallas_call(
        _oproj_prompt_body,
        grid=(NP_ROWS // TM,),
        in_specs=[pl.BlockSpec((TM, D_MODEL), lambda i: (i, 0)),
                  pl.BlockSpec((TM, N_HEADS * D_V), lambda i: (i, 0)),
                  _const_spec(wo.shape)],
        out_specs=pl.BlockSpec((TM, D_MODEL), lambda i: (i, 0)),
        out_shape=jax.ShapeDtypeStruct((ROWS, D_MODEL), F32),
        input_output_aliases={0: 0},
        compiler_params=_cparams(("parallel",)),
        name="oproj_prompt",
    )(x, o, wo)


def _oproj_sample_body(x_ref, ol_ref, wuv_ref, wo_ref, out_ref):
    acc = x_ref[...]
    for p in range(N_HEADS // 2):
        pair = jnp.concatenate([ol_ref[2 * p], ol_ref[2 * p + 1]], axis=1).astype(BF16)
        o_pair = _dot(pair, wuv_ref[p]).astype(BF16)
        acc = acc + _dot(o_pair, wo_ref[p * 2 * D_V:(p + 1) * 2 * D_V, :])
    out_ref[...] = acc


def oproj_sample(x, o_lat, wuv_pair, wo):
    row0 = NP_ROWS // TM
    return pl.pallas_call(
        _oproj_sample_body,
        grid=(NS_ROWS // TM,),
        in_specs=[pl.BlockSpec((TM, D_MODEL), lambda i: (i + row0, 0)),
                  pl.BlockSpec((N_HEADS, TM, KV_LORA), lambda i: (0, i, 0)),
                  _const_spec(wuv_pair.shape),
                  _const_spec(wo.shape)],
        out_specs=pl.BlockSpec((TM, D_MODEL), lambda i: (i + row0, 0)),
        out_shape=jax.ShapeDtypeStruct((ROWS, D_MODEL), F32),
        input_output_aliases={0: 0},
        compiler_params=_cparams(("parallel",)),
        name="oproj_sample",
    )(x, o_lat, wuv_pair, wo)


S5_Q = DEC_SEQ
S5_BLK = LANES // GROUP_SIZE
S5_NBLK = N_GROUPS // S5_BLK
S5_HALF = S5_BLK * STATE_DIM
S5_CT = 64
S5_CHUNKS = SEQ // S5_Q


def _rms_chunked_body(x_ref, g_ref, o_ref):
    o_ref[...] = _rms(x_ref[...], g_ref[...]).reshape(o_ref.shape)


def rmsnorm_chunked(x, g):
    nci = S5_CHUNKS // S5_CT
    out = pl.pallas_call(
        _rms_chunked_body,
        grid=(BATCH, nci),
        in_specs=[pl.BlockSpec((S5_CT * S5_Q, D_MODEL), lambda b, c: (b * nci + c, 0)),
                  _const_spec((1, D_MODEL))],
        out_specs=pl.BlockSpec((S5_CT, None, S5_Q, D_MODEL), lambda b, c: (c, b, 0, 0)),
        out_shape=jax.ShapeDtypeStruct((S5_CHUNKS, BATCH, S5_Q, D_MODEL), F32),
        compiler_params=_cparams(("parallel", "parallel")),
        name="rmsnorm_chunked",
    )(x, g.reshape(1, D_MODEL))
    return out.reshape(NP_ROWS, D_MODEL)


def _s5_body(n_chunks, nb, h_ref, t_ref, m_ref, n_ref, a_ref, h0_ref, y_ref, hf_ref,
             s_s, p_s, st_s):
    rows = n_chunks * nb

    @pl.when(pl.program_id(1) == 0)
    def _():
        st_s[...] = h0_ref[...]

    u = jnp.concatenate([h_ref[pl.ds(k, rows, stride=S5_Q), :].astype(BF16)
                         for k in range(S5_Q)], axis=1)
    s_s[...] = _dot(u, m_ref[...])
    are = a_ref[:, :S5_HALF]
    aim = a_ref[:, S5_HALF:]

    def step(c, h):
        r0 = pl.multiple_of(c * nb, nb)
        p_s[pl.ds(r0, nb), :] = h
        s = s_s[pl.ds(r0, nb), :]
        hre, him = h[:, :S5_HALF], h[:, S5_HALF:]
        nre = are * hre - aim * him + s[:, :S5_HALF]
        nim = are * him + aim * hre + s[:, S5_HALF:]
        return jnp.concatenate([nre, nim], axis=1)

    h = lax.fori_loop(0, n_chunks, step, st_s[...])
    st_s[...] = h
    hf_ref[...] = h
    y = _dot(u, t_ref[...]) + _dot(p_s[...].astype(BF16), n_ref[...])
    for l in range(S5_Q):
        y_ref[pl.ds(l, rows, stride=S5_Q), :] = y[:, l * LANES:(l + 1) * LANES]


def s5_scan(h, tabs, h0, n_chunks, nb):
    tbig, mbig, nbig, abig = tabs
    rows = n_chunks * nb
    tile = rows * S5_Q
    width = S5_Q * LANES

    def bspec(shape):
        return pl.BlockSpec((None,) + shape, lambda j, c: (j, 0, 0))

    return pl.pallas_call(
        functools.partial(_s5_body, n_chunks, nb),
        grid=(S5_NBLK, h.shape[0] // tile),
        in_specs=[pl.BlockSpec((tile, LANES), lambda j, c: (c, j)),
                  bspec((width, width)), bspec((width, 2 * S5_HALF)), bspec((2 * S5_HALF, width)),
                  bspec((1, 2 * S5_HALF)), bspec((nb, 2 * S5_HALF))],
        out_specs=[pl.BlockSpec((tile, LANES), lambda j, c: (c, j)), bspec((nb, 2 * S5_HALF))],
        out_shape=[jax.ShapeDtypeStruct(h.shape, F32),
                   jax.ShapeDtypeStruct((S5_NBLK, nb, 2 * S5_HALF), F32)],
        scratch_shapes=[pltpu.VMEM((rows, 2 * S5_HALF), F32), pltpu.VMEM((rows, 2 * S5_HALF), F32),
                        pltpu.VMEM((nb, 2 * S5_HALF), F32)],
        compiler_params=_cparams(("parallel", "arbitrary")),
        name="s5_scan",
    )(h, tbig, mbig, nbig, abig, h0)


def _glu_body(x_ref, h_ref, y_ref, d_ref, w_ref, o_ref):
    x = x_ref[...]
    y = y_ref[...].reshape(x.shape) + d_ref[...] * h_ref[...].reshape(x.shape)
    gl = jax.nn.gelu(y).astype(BF16)
    ab = _dot(gl, w_ref[...])
    o_ref[...] = x + ab[:, :D_MODEL] * jax.nn.sigmoid(ab[:, D_MODEL:])


def glu(x, h, y, d, w, chunked):
    if chunked:
        nci = S5_CHUNKS // S5_CT
        grid = (BATCH, nci)
        x_spec = pl.BlockSpec((S5_CT * S5_Q, D_MODEL), lambda b, c: (b * nci + c, 0))
        hy_spec = pl.BlockSpec((S5_CT, None, S5_Q, D_MODEL), lambda b, c: (c, b, 0, 0))
        h = h.reshape(S5_CHUNKS, BATCH, S5_Q, D_MODEL)
        y = y.reshape(S5_CHUNKS, BATCH, S5_Q, D_MODEL)
    else:
        blk0 = NP_ROWS // TM
        grid = (NS_ROWS // TM,)
        x_spec = pl.BlockSpec((TM, D_MODEL), lambda i: (i + blk0, 0))
        hy_spec = pl.BlockSpec((TM, D_MODEL), lambda i: (i, 0))
    return pl.pallas_call(
        _glu_body,
        grid=grid,
        in_specs=[x_spec, hy_spec, hy_spec, _const_spec((1, D_MODEL)), _const_spec(w.shape)],
        out_specs=x_spec,
        out_shape=jax.ShapeDtypeStruct((ROWS, D_MODEL), F32),
        input_output_aliases={0: 0},
        compiler_params=_cparams(("parallel",) * len(grid)),
        name="glu_prompt" if chunked else "glu_sample",
    )(x, h, y, d.reshape(1, D_MODEL), w)


def _prep_mla(w_in, g_q, g_kv, w_q_up, w_kv_up, w_o):
    z = lambda *s: jnp.zeros(s, F32)
    wk_pe = w_in[:, Q_LORA + KV_LORA:]
    half = D_ROPE // 2
    win = jnp.concatenate(
        [w_in[:, :Q_LORA + KV_LORA], wk_pe, z(D_MODEL, LANES - D_ROPE),
         -wk_pe[:, half:], wk_pe[:, :half], z(D_MODEL, LANES - D_ROPE)], axis=1)
    wq3 = w_q_up.reshape(Q_LORA, N_HEADS, D_NOPE + D_ROPE)
    nope, x1, x2 = wq3[..., :D_NOPE], wq3[..., D_NOPE:D_NOPE + half], wq3[..., D_NOPE + half:]
    zpad = z(Q_LORA, N_HEADS, HEAD_PAD - D_NOPE - D_ROPE)
    wq = jnp.concatenate([nope, x1, x2, zpad], axis=-1).reshape(Q_LORA, N_HEADS * HEAD_PAD)
    wqs = jnp.concatenate([z(Q_LORA, N_HEADS, D_NOPE), -x2, x1, zpad], axis=-1)
    wqs = wqs.reshape(Q_LORA, N_HEADS * HEAD_PAD)
    wkv3 = w_kv_up.reshape(KV_LORA, N_HEADS, D_NOPE + D_V)
    w_uk, w_uv = wkv3[..., :D_NOPE], wkv3[..., D_NOPE:]
    wk = jnp.concatenate([w_uk, z(KV_LORA, N_HEADS, HEAD_PAD - D_NOPE)], axis=-1)
    wk = wk.reshape(KV_LORA, N_HEADS * HEAD_PAD)
    wv = w_uv.reshape(KV_LORA, N_HEADS * D_V).T
    wuk_ext = jnp.zeros((N_HEADS, HEAD_PAD, QEXT), F32)
    wuk_ext = wuk_ext.at[:, :D_NOPE, :KV_LORA].set(jnp.transpose(w_uk, (1, 2, 0)))
    wuk_ext = wuk_ext.at[:, D_NOPE:D_NOPE + D_ROPE, KV_LORA:KV_LORA + D_ROPE].set(
        jnp.broadcast_to(jnp.eye(D_ROPE, dtype=F32), (N_HEADS, D_ROPE, D_ROPE)))
    uv = jnp.transpose(w_uv, (1, 0, 2)).reshape(N_HEADS // 2, 2, KV_LORA, D_V)
    wuv_pair = jnp.zeros((N_HEADS // 2, 2 * KV_LORA, 2 * D_V), F32)
    wuv_pair = wuv_pair.at[:, :KV_LORA, :D_V].set(uv[:, 0])
    wuv_pair = wuv_pair.at[:, KV_LORA:, D_V:].set(uv[:, 1])
    return dict(win=win.astype(BF16), gq=g_q.reshape(1, Q_LORA), gkv=g_kv.reshape(1, KV_LORA),
                wq=wq.astype(BF16), wqs=wqs.astype(BF16), wk=wk.astype(BF16), wv=wv.astype(BF16),
                wuk_ext=wuk_ext.astype(BF16), wuv_pair=wuv_pair.astype(BF16), wo=w_o.astype(BF16))


def _rope_tabs(pos, reps):
    half = D_ROPE // 2
    inv = ROPE_THETA ** (-jnp.arange(half, dtype=F32) / half)
    ang = pos.astype(F32)[:, None] * inv[None, :]
    cos, sin = jnp.cos(ang), jnp.sin(ang)
    n = pos.shape[0]
    one, zero = jnp.ones((n, D_NOPE), F32), jnp.zeros((n, D_NOPE), F32)
    zpad = jnp.zeros((n, HEAD_PAD - D_NOPE - D_ROPE), F32)
    cq = jnp.concatenate([one, cos, cos, zpad], axis=1)
    sq = jnp.concatenate([zero, sin, sin, zpad], axis=1)
    kpad = jnp.zeros((n, LANES - D_ROPE), F32)
    ck = jnp.concatenate([cos, cos, kpad], axis=1)
    sk = jnp.concatenate([sin, sin, kpad], axis=1)
    cq, sq = jnp.tile(cq, (reps, 2)), jnp.tile(sq, (reps, 2))
    return cq, sq, jnp.tile(ck, (reps, 1)), jnp.tile(sk, (reps, 1))


def _prep_s5(a_re, a_im, log_dt, b_re, b_im, c_re, c_im, q):
    hi = lax.Precision.HIGHEST
    dt = jnp.exp(log_dt)[:, None]
    dre, dim = dt * a_re, dt * a_im
    mag = jnp.exp(dre)
    abr, abi = mag * jnp.cos(dim), mag * jnp.sin(dim)
    den = a_re * a_re + a_im * a_im
    xr, xi = abr - 1.0, abi
    fr, fi = (xr * a_re + xi * a_im) / den, (xi * a_re - xr * a_im) / den
    bbr = fr[..., None] * b_re - fi[..., None] * b_im
    bbi = fr[..., None] * b_im + fi[..., None] * b_re
    j = jnp.arange(q + 1, dtype=F32)[:, None, None]
    pmag = jnp.exp(j * dre)
    pwr, pwi = pmag * jnp.cos(j * dim), pmag * jnp.sin(j * dim)
    pbr = pwr[..., None] * bbr - pwi[..., None] * bbi
    pbi = pwr[..., None] * bbi + pwi[..., None] * bbr
    kk = (jnp.einsum("gtp,jgps->jgst", c_re, pbr[:q], precision=hi)
          - jnp.einsum("gtp,jgps->jgst", c_im, pbi[:q], precision=hi))
    lag = jnp.arange(q)[None, :] - jnp.arange(q)[:, None]
    tk = kk[jnp.clip(lag, 0, q - 1)]
    tk = jnp.where((lag >= 0)[:, :, None, None, None], tk, 0.0)
    g = a_re.shape[0]
    t = jnp.transpose(tk, (2, 0, 3, 1, 4)).reshape(g, q * GROUP_SIZE, q * GROUP_SIZE)
    mre = jnp.transpose(pbr[:q][::-1], (1, 0, 3, 2)).reshape(g, q * GROUP_SIZE, STATE_DIM)
    mim = jnp.transpose(pbi[:q][::-1], (1, 0, 3, 2)).reshape(g, q * GROUP_SIZE, STATE_DIM)
    ncr = pwr[1:, :, None, :] * c_re[None] - pwi[1:, :, None, :] * c_im[None]
    nci = pwr[1:, :, None, :] * c_im[None] + pwi[1:, :, None, :] * c_re[None]
    nre = jnp.transpose(ncr, (1, 3, 0, 2)).reshape(g, STATE_DIM, q * GROUP_SIZE)
    nim = jnp.transpose(nci, (1, 3, 0, 2)).reshape(g, STATE_DIM, q * GROUP_SIZE)
    are, aim = pwr[q][:, None, :], pwi[q][:, None, :]
    return t, mre, mim, nre, nim, are, aim


def _s5_tables_body(at_ref, am_ref, an_ref, et_ref, em_ref, t_ref, m_ref, n_ref):
    width = S5_Q * LANES
    row = lax.broadcasted_iota(jnp.int32, (width, width), 0)
    col = lax.broadcasted_iota(jnp.int32, (width, width), 1)

    def expand(a_ref, e_ref, row_shift, col_shift):
        full = _dot(a_ref[...].astype(BF16), e_ref[...])
        keep = ((row >> row_shift) & (S5_BLK - 1)) == ((col >> col_shift) & (S5_BLK - 1))
        return jnp.where(keep, full, 0.0).astype(BF16)

    gs, sd = GROUP_SIZE.bit_length() - 1, STATE_DIM.bit_length() - 1
    t_ref[...] = expand(at_ref, et_ref, gs, gs)
    m_ref[...] = expand(am_ref, em_ref, gs, sd)
    n_ref[...] = expand(an_ref, et_ref, sd, gs)


def _s5_block_tables(p):
    t, mre, mim, nre, nim, are, aim = _prep_s5(*p, S5_Q)
    q = S5_Q
    width = q * LANES
    assert 2 * S5_HALF == width and q * GROUP_SIZE == LANES and 2 * STATE_DIM == LANES

    def rows_kgs(a):
        a5 = a.reshape(S5_NBLK, S5_BLK, q, GROUP_SIZE, LANES)
        return jnp.transpose(a5, (0, 2, 1, 3, 4)).reshape(S5_NBLK, width, LANES)

    at = rows_kgs(t)
    am = rows_kgs(jnp.concatenate([mre, mim], axis=-1))
    an = jnp.stack([nre.reshape(S5_NBLK, S5_BLK, STATE_DIM, LANES),
                    -nim.reshape(S5_NBLK, S5_BLK, STATE_DIM, LANES)], axis=1)
    an = an.reshape(S5_NBLK, width, LANES)
    r = jnp.arange(LANES)[:, None]
    c = jnp.arange(width)[None, :]
    et = ((r // GROUP_SIZE == c // LANES) & (r % GROUP_SIZE == c % GROUP_SIZE)).astype(BF16)
    em = ((r // STATE_DIM == c // S5_HALF) & (r % STATE_DIM == c % STATE_DIM)).astype(BF16)
    cspec = pl.BlockSpec((None, width, LANES), lambda j: (j, 0, 0))
    ospec = pl.BlockSpec((None, width, width), lambda j: (j, 0, 0))
    big = jax.ShapeDtypeStruct((S5_NBLK, width, width), BF16)
    tbig, mbig, nbig = pl.pallas_call(
        _s5_tables_body,
        grid=(S5_NBLK,),
        in_specs=[cspec, cspec, cspec, _const_spec((LANES, width)), _const_spec((LANES, width))],
        out_specs=[ospec, ospec, ospec],
        out_shape=[big, big, big],
        compiler_params=_cparams(("parallel",)),
        name="s5_tables",
    )(at, am, an, et, em)
    abig = jnp.concatenate([are.reshape(S5_NBLK, 1, S5_HALF), aim.reshape(S5_NBLK, 1, S5_HALF)], axis=2)
    return tbig, mbig, nbig, abig


def _state_to_blocks(re, im):
    nb = re.shape[0]
    f = lambda a: jnp.transpose(a.reshape(nb, S5_NBLK, S5_HALF), (1, 0, 2))
    return jnp.concatenate([f(re), f(im)], axis=2)


def _blocks_to_state(hf):
    nb = hf.shape[1]
    f = lambda a: jnp.transpose(a, (1, 0, 2)).reshape(nb, N_GROUPS, STATE_DIM)
    return f(hf[..., :S5_HALF]), f(hf[..., S5_HALF:])


def _s5_layer(x, g_mix, p, d, w_glu, h0re, h0im):
    tabs = _s5_block_tables(p)
    hp = rmsnorm_chunked(x, g_mix)
    hs = rmsnorm(x, g_mix, F32, NP_ROWS, NS_ROWS)
    zero = jnp.zeros((S5_NBLK, BATCH, 2 * S5_HALF), F32)
    yp, hfp = s5_scan(hp, tabs, zero, S5_CT, BATCH)
    ys, hfs = s5_scan(hs, tabs, _state_to_blocks(h0re, h0im), 1, DEC_BATCH)
    x = glu(x, hp, yp, d, w_glu, True)
    x = glu(x, hs, ys, d, w_glu, False)
    return (x,) + _blocks_to_state(hfp) + _blocks_to_state(hfs)


def kernel(x_prompt, x_sample, cache_ckv, cache_kpe, state_ssm_re, state_ssm_im, page_table, norm_mix, norm_mlp, norm_final, mla_w_in, mla_q_norm, mla_kv_norm, mla_w_q_up, mla_w_kv_up, mla_w_o, ssm_a_re, ssm_a_im, ssm_log_dt, ssm_b_re, ssm_b_im, ssm_c_re, ssm_c_im, ssm_d, ssm_w_glu, mlp_w1, mlp_w2):
    x = jnp.concatenate([x_prompt.reshape(NP_ROWS, D_MODEL), x_sample.reshape(NS_ROWS, D_MODEL)], axis=0)
    tabs_p = _rope_tabs(jnp.arange(SEQ, dtype=jnp.int32), 1)
    tabs_s = _rope_tabs(PAST_LEN + jnp.arange(DEC_SEQ, dtype=jnp.int32), TM_PROJ // DEC_SEQ)
    cache_kpet = jnp.swapaxes(cache_kpe, 2, 3)
    p_ckv, p_kpe, p_re, p_im = [], [], [], []
    s_ckv, s_kpe, s_re, s_im = [], [], [], []
    for i in range(DEPTH):
        j = i // 2
        if i % 2 == 0:
            w = _prep_mla(mla_w_in[j], mla_q_norm[j], mla_kv_norm[j], mla_w_q_up[j],
                          mla_w_kv_up[j], mla_w_o[j])
            ckv_p, kpe_p, q_p, k_p, v_p = mla_proj(x, False, norm_mix[i], w, tabs_p)
            ckv_s, kpe_s, qext = mla_proj(x, True, norm_mix[i], w, tabs_s)
            o_p = flash_prompt(q_p, k_p, v_p)
            o_lat = attn_sample(j, page_table, qext, ckv_s, kpe_s, cache_ckv, cache_kpet)
            x = oproj_prompt(x, o_p, w["wo"])
            x = oproj_sample(x, o_lat, w["wuv_pair"], w["wo"])
            p_ckv.append(ckv_p.reshape(BATCH, SEQ, KV_LORA))
            p_kpe.append(kpe_p.reshape(BATCH, SEQ, D_ROPE))
            s_ckv.append(ckv_s.reshape(DEC_BATCH, DEC_SEQ, KV_LORA))
            s_kpe.append(kpe_s.reshape(DEC_BATCH, DEC_SEQ, D_ROPE))
        else:
            p = (ssm_a_re[j], ssm_a_im[j], ssm_log_dt[j], ssm_b_re[j], ssm_b_im[j],
                 ssm_c_re[j], ssm_c_im[j])
            x, pre, pim, sre, sim = _s5_layer(x, norm_mix[i], p, ssm_d[j], ssm_w_glu[j].astype(BF16),
                                              state_ssm_re[j], state_ssm_im[j])
            p_re.append(pre); p_im.append(pim); s_re.append(sre); s_im.append(sim)
        x = mlp(x, norm_mlp[i], mlp_w1[i].astype(BF16), mlp_w2[i].astype(BF16))
    y_prompt = rmsnorm(x, norm_final, F32, 0, NP_ROWS).reshape(BATCH, SEQ, D_MODEL)
    y_sample = rmsnorm(x, norm_final, F32, NP_ROWS, NS_ROWS).reshape(DEC_BATCH, DEC_SEQ, D_MODEL)
    return (y_prompt, y_sample,
            jnp.stack(p_ckv), jnp.stack(p_kpe), jnp.stack(p_re), jnp.stack(p_im),
            jnp.stack(s_ckv), jnp.stack(s_kpe), jnp.stack(s_re), jnp.stack(s_im))
```

```python
import functools
import math

import jax
import jax.numpy as jnp
from jax import lax
from jax.experimental import pallas as pl
from jax.experimental.pallas import tpu as pltpu

F32 = jnp.float32
BF16 = jnp.bfloat16

D_MODEL = 1024
BATCH = 8
SEQ = 2048
DEPTH = 4
DEC_BATCH = 128
DEC_SEQ = 8
PAST_LEN = 8192
PAGE_SIZE = 128
N_PAGES = PAST_LEN // PAGE_SIZE
N_HEADS = 16
D_NOPE = 64
D_ROPE = 32
D_V = 64
Q_LORA = 512
KV_LORA = 256
ROPE_THETA = 10000.0
GROUP_SIZE = 16
N_GROUPS = D_MODEL // GROUP_SIZE
STATE_DIM = 64
D_FF = 4 * D_MODEL
EPS = 1e-6
SCALE = (D_NOPE + D_ROPE) ** -0.5
EXP2_SCALE = SCALE * math.log2(math.e)

NP_ROWS = BATCH * SEQ
NS_ROWS = DEC_BATCH * DEC_SEQ
ROWS = NP_ROWS + NS_ROWS

LANES = 128
HEAD_PAD = 128
QEXT = KV_LORA + LANES
VMEM_LIMIT = 52 * 1024 * 1024

TM = 512
TM_PROJ = 256
TQ = 256
PAGES_PER_STEP = 32


def _cparams(sem):
    return pltpu.CompilerParams(dimension_semantics=sem, vmem_limit_bytes=VMEM_LIMIT)


def _rms(x, g):
    return x * lax.rsqrt(jnp.mean(x * x, axis=-1, keepdims=True) + EPS) * g


def _dot(a, b):
    return jnp.dot(a, b, preferred_element_type=F32)


def _dot_nt(a, b):
    return lax.dot_general(a, b, (((1,), (1,)), ((), ())), preferred_element_type=F32)


def _const_spec(shape):
    nd = len(shape)
    return pl.BlockSpec(shape, lambda *_: (0,) * nd)


def _rms_body(x_ref, g_ref, o_ref):
    o_ref[...] = _rms(x_ref[...], g_ref[...]).astype(o_ref.dtype)


def rmsnorm(x, g, dtype, row0=0, rows=ROWS):
    blk0 = row0 // TM
    return pl.pallas_call(
        _rms_body,
        grid=(rows // TM,),
        in_specs=[pl.BlockSpec((TM, D_MODEL), lambda i: (i + blk0, 0)), _const_spec((1, D_MODEL))],
        out_specs=pl.BlockSpec((TM, D_MODEL), lambda i: (i, 0)),
        out_shape=jax.ShapeDtypeStruct((rows, D_MODEL), dtype),
        compiler_params=_cparams(("parallel",)),
        name="rmsnorm",
    )(x, g.reshape(1, D_MODEL))


FF_CHUNK = 1024


def _mlp_body(x_ref, g_ref, w1_ref, w2_ref, o_ref):
    x = x_ref[...]
    h = _rms(x, g_ref[...]).astype(BF16)
    acc = x
    for c in range(D_FF // FF_CHUNK):
        sl = slice(c * FF_CHUNK, (c + 1) * FF_CHUNK)
        z = jnp.maximum(_dot(h, w1_ref[:, sl]), 0.0)
        acc = acc + _dot((z * z).astype(BF16), w2_ref[sl, :])
    o_ref[...] = acc


def mlp(x, g, w1, w2):
    rows = x.shape[0]
    return pl.pallas_call(
        _mlp_body,
        grid=(rows // TM,),
        in_specs=[
            pl.BlockSpec((TM, D_MODEL), lambda i: (i, 0)),
            _const_spec((1, D_MODEL)),
            _const_spec((D_MODEL, D_FF)),
            _const_spec((D_FF, D_MODEL)),
        ],
        out_specs=pl.BlockSpec((TM, D_MODEL), lambda i: (i, 0)),
        out_shape=jax.ShapeDtypeStruct((rows, D_MODEL), F32),
        compiler_params=_cparams(("parallel",)),
        name="mlp",
    )(x, g.reshape(1, D_MODEL), w1, w2)


def _proj_body(sample, x_ref, gm_ref, win_ref, gq_ref, gkv_ref, wq_ref, wqs_ref,
               cq_ref, sq_ref, ck_ref, sk_ref, wa_ref, wb_ref, *outs):
    h = _rms(x_ref[...], gm_ref[...]).astype(BF16)
    proj = _dot(h, win_ref[...])
    cq = _rms(proj[:, :Q_LORA], gq_ref[...]).astype(BF16)
    ckv = _rms(proj[:, Q_LORA:Q_LORA + KV_LORA], gkv_ref[...])
    kpe = (proj[:, 768:896] * ck_ref[...] + proj[:, 896:1024] * sk_ref[...])
    ckv_ref, kpe_ref = outs[0], outs[1]
    ckv_ref[...] = ckv
    kpe_ref[...] = kpe[:, :D_ROPE]
    cq_tab = cq_ref[...]
    sq_tab = sq_ref[...]
    for p in range(N_HEADS // 2):
        sl = slice(p * 2 * HEAD_PAD, (p + 1) * 2 * HEAD_PAD)
        qp = _dot(cq, wq_ref[:, sl]) * cq_tab + _dot(cq, wqs_ref[:, sl]) * sq_tab
        qp = qp.astype(BF16)
        if sample:
            qext_ref = outs[2]
            for hh in range(2):
                head = 2 * p + hh
                qh = qp[:, hh * HEAD_PAD:(hh + 1) * HEAD_PAD]
                qext_ref[head] = _dot(qh, wa_ref[head])
        else:
            outs[2][:, sl] = qp
    if not sample:
        k_ref, vt_ref = outs[3], outs[4]
        ckv_b = ckv.astype(BF16)
        kpl = pltpu.roll(kpe, D_NOPE, axis=1)
        kpl2 = jnp.concatenate([kpl, kpl], axis=1)
        for p in range(N_HEADS // 2):
            sl = slice(p * 2 * HEAD_PAD, (p + 1) * 2 * HEAD_PAD)
            k_ref[:, sl] = ((_dot(ckv_b, wa_ref[:, sl]) + kpl2) * EXP2_SCALE).astype(BF16)
        vt_ref[...] = _dot_nt(wb_ref[...], ckv_b).astype(BF16)


def mla_proj(x, sample, gm, w, tabs):
    tm = TM_PROJ
    if sample:
        rows, row0, tab_map = NS_ROWS, NP_ROWS // tm, (lambda i: (0, 0))
    else:
        rows, row0, tab_map = NP_ROWS, 0, (lambda i: (i % (SEQ // tm), 0))
    cq, sq, ck, sk = tabs
    wa = w["wuk_ext"] if sample else w["wk"]
    wb = w["wv"]
    out_shape = [jax.ShapeDtypeStruct((rows, KV_LORA), F32),
                 jax.ShapeDtypeStruct((rows, D_ROPE), F32)]
    out_specs = [pl.BlockSpec((tm, KV_LORA), lambda i: (i, 0)),
                 pl.BlockSpec((tm, D_ROPE), lambda i: (i, 0))]
    if sample:
        out_shape.append(jax.ShapeDtypeStruct((N_HEADS, rows, QEXT), F32))
        out_specs.append(pl.BlockSpec((N_HEADS, tm, QEXT), lambda i: (0, i, 0)))
    else:
        out_shape += [jax.ShapeDtypeStruct((rows, N_HEADS * HEAD_PAD), BF16),
                      jax.ShapeDtypeStruct((rows, N_HEADS * HEAD_PAD), BF16),
                      jax.ShapeDtypeStruct((N_HEADS * D_V, rows), BF16)]
        out_specs += [pl.BlockSpec((tm, N_HEADS * HEAD_PAD), lambda i: (i, 0)),
                      pl.BlockSpec((tm, N_HEADS * HEAD_PAD), lambda i: (i, 0)),
                      pl.BlockSpec((N_HEADS * D_V, tm), lambda i: (0, i))]
    return pl.pallas_call(
        functools.partial(_proj_body, sample),
        grid=(rows // tm,),
        in_specs=[
            pl.BlockSpec((tm, D_MODEL), lambda i: (i + row0, 0)),
            _const_spec((1, D_MODEL)),
            _const_spec(w["win"].shape),
            _const_spec((1, Q_LORA)),
            _const_spec((1, KV_LORA)),
            _const_spec(w["wq"].shape),
            _const_spec(w["wqs"].shape),
            pl.BlockSpec((tm, 2 * HEAD_PAD), tab_map),
            pl.BlockSpec((tm, 2 * HEAD_PAD), tab_map),
            pl.BlockSpec((tm, LANES), tab_map),
            pl.BlockSpec((tm, LANES), tab_map),
            _const_spec(wa.shape),
            _const_spec(wb.shape),
        ],
        out_specs=out_specs,
        out_shape=out_shape,
        compiler_params=_cparams(("parallel",)),
        name="mla_proj_sample" if sample else "mla_proj_prompt",
    )(x, gm.reshape(1, D_MODEL), w["win"], w["gq"], w["gkv"], w["wq"], w["wqs"],
      cq, sq, ck, sk, wa, wb)


def _flash_body(q_ref, k_ref, vt_ref, o_ref, s_buf):
    key = lax.broadcasted_iota(jnp.int32, (TQ, TQ), 0)
    qry = lax.broadcasted_iota(jnp.int32, (TQ, TQ), 1)
    items = [(qi, hh) for qi in range(SEQ // TQ) for hh in range(2)]
    ones = jnp.ones((16, SEQ), BF16)

    def scores(n):
        qi, hh = items[n]
        q0 = qi * TQ
        hs = slice(hh * HEAD_PAD, (hh + 1) * HEAD_PAD)
        q = q_ref[q0:q0 + TQ, hs]
        sd = _dot_nt(k_ref[q0:q0 + TQ, hs], q)
        sd = jnp.where(key <= qry, sd, -jnp.inf)
        s_buf[n % 2, q0:q0 + TQ, :] = sd
        m = jnp.max(sd, axis=0, keepdims=True)
        if qi > 0:
            sm = _dot_nt(k_ref[0:q0, hs], q)
            s_buf[n % 2, 0:q0, :] = sm
            m = jnp.maximum(m, jnp.max(sm, axis=0, keepdims=True))
        return m

    def values(n, m):
        qi, hh = items[n]
        kv_len = (qi + 1) * TQ
        p = jnp.exp2(s_buf[n % 2, 0:kv_len, :] - m).astype(BF16)
        vt1 = jnp.concatenate([vt_ref[hh * D_V:(hh + 1) * D_V, 0:kv_len], ones[:, 0:kv_len]], axis=0)
        ot = _dot(vt1, p)
        return ot[:D_V] / ot[D_V:D_V + 1]

    m_next = scores(0)
    ots = []
    for n, (qi, hh) in enumerate(items):
        m = m_next
        if n + 1 < len(items):
            m_next = scores(n + 1)
        ots.append(values(n, m))
        if hh == 1:
            q0 = qi * TQ
            o_ref[q0:q0 + TQ, :] = jnp.concatenate(ots, axis=0).T.astype(o_ref.dtype)
            ots = []


def flash_prompt(q, k, vt):
    return pl.pallas_call(
        _flash_body,
        grid=(BATCH, N_HEADS // 2),
        in_specs=[
            pl.BlockSpec((SEQ, 2 * HEAD_PAD), lambda b, p: (b, p)),
            pl.BlockSpec((SEQ, 2 * HEAD_PAD), lambda b, p: (b, p)),
            pl.BlockSpec((2 * D_V, SEQ), lambda b, p: (p, b)),
        ],
        out_specs=pl.BlockSpec((SEQ, 2 * D_V), lambda b, p: (b, p)),
        out_shape=jax.ShapeDtypeStruct((NP_ROWS, N_HEADS * D_V), BF16),
        scratch_shapes=[pltpu.VMEM((2, SEQ, TQ), F32)],
        compiler_params=_cparams(("parallel", "parallel")),
        name="flash_prompt",
    )(q, k, vt)


N_CHUNKS = N_PAGES // PAGES_PER_STEP
assert N_CHUNKS % 2 == 0 and N_CHUNKS * PAGES_PER_STEP == N_PAGES
SATTN_KEYS = 2048
QROWS = N_HEADS * DEC_SEQ


def _sattn_body(layer, pt_ref, q_ref, cnew_ref, knew_ref, ckv_hbm, kpet_hbm, o_ref,
                cbuf, kbuf, sem, s_buf, ck_b):
    n = PAGES_PER_STEP
    b = pl.program_id(0)

    def chunk_copies(req, c, slot):
        cps = []
        for k in range(n):
            page = pt_ref[req, c * n + k]
            rows = pl.ds(k * PAGE_SIZE, PAGE_SIZE)
            cps.append(pltpu.make_async_copy(ckv_hbm.at[layer, page], cbuf.at[slot, rows, :],
                                             sem.at[slot, 0]))
            cps.append(pltpu.make_async_copy(kpet_hbm.at[layer, page], kbuf.at[slot, :, rows],
                                             sem.at[slot, 1]))
        return cps

    def start(req, c, slot):
        for cp in chunk_copies(req, c, slot):
            cp.start()

    def wait(req, c, slot):
        for cp in chunk_copies(req, c, slot):
            cp.wait()

    @pl.when(b == 0)
    def _():
        for c in range(2):
            start(0, c, c)

    q = q_ref[...].reshape(QROWS, QEXT)
    qlat = q[:, :KV_LORA].astype(BF16)
    qpe = q[:, KV_LORA:KV_LORA + D_ROPE].astype(BF16)
    pad = PAGE_SIZE - DEC_SEQ
    cn = jnp.concatenate([cnew_ref[...], jnp.zeros((pad, KV_LORA), F32)], axis=0).astype(BF16)
    kn = jnp.concatenate([knew_ref[...], jnp.zeros((pad, D_ROPE), F32)], axis=0).astype(BF16)
    s = _dot_nt(qlat, cn) + _dot_nt(qpe, kn)
    tok = lax.broadcasted_iota(jnp.int32, (QROWS, PAGE_SIZE), 0) % DEC_SEQ
    key = lax.broadcasted_iota(jnp.int32, (QROWS, PAGE_SIZE), 1)
    s = jnp.where(key <= tok, s, -jnp.inf)
    m = jnp.max(s, axis=-1, keepdims=True)
    p = jnp.exp2((s - m) * EXP2_SCALE)
    l = jnp.sum(p, axis=-1, keepdims=True)
    acc = _dot(p.astype(BF16), cn)

    n_sb = n * PAGE_SIZE // SATTN_KEYS
    items = [(c, sb) for c in range(N_CHUNKS) for sb in range(n_sb)]

    def scores(i):
        c, sb = items[i]
        keys = slice(sb * SATTN_KEYS, (sb + 1) * SATTN_KEYS)
        if sb == 0:
            wait(b, c, c % 2)
        ck = cbuf[c % 2, keys, :].astype(BF16)
        kt = kbuf[c % 2, :, keys].astype(BF16)
        ck_b[i % 2] = ck
        s = _dot_nt(qlat, ck) + _dot(qpe, kt)
        s_buf[i % 2] = s
        if sb == n_sb - 1:
            if c + 2 < N_CHUNKS:
                start(b, c + 2, c % 2)
            else:
                @pl.when(b + 1 < DEC_BATCH)
                def _():
                    start(b + 1, c + 2 - N_CHUNKS, c % 2)
        return jnp.max(s, axis=-1, keepdims=True)

    bm_next = scores(0)
    for i in range(len(items)):
        bm = bm_next
        if i + 1 < len(items):
            bm_next = scores(i + 1)
        m_new = jnp.maximum(m, bm)
        alpha = jnp.exp2((m - m_new) * EXP2_SCALE)
        p = jnp.exp2((s_buf[i % 2] - m_new) * EXP2_SCALE)
        l = alpha * l + jnp.sum(p, axis=-1, keepdims=True)
        acc = alpha * acc + _dot(p.astype(BF16), ck_b[i % 2])
        m = m_new

    o_ref[...] = (acc / l).reshape(N_HEADS, DEC_SEQ, KV_LORA)


def attn_sample(layer, page_table, qext, ckv_all, kpe_all, cache_ckv, cache_kpet):
    n = PAGES_PER_STEP
    grid_spec = pltpu.PrefetchScalarGridSpec(
        num_scalar_prefetch=1,
        grid=(DEC_BATCH,),
        in_specs=[
            pl.BlockSpec((N_HEADS, DEC_SEQ, QEXT), lambda b, pt: (0, b, 0)),
            pl.BlockSpec((DEC_SEQ, KV_LORA), lambda b, pt: (b, 0)),
            pl.BlockSpec((DEC_SEQ, D_ROPE), lambda b, pt: (b, 0)),
            pl.BlockSpec(memory_space=pl.ANY),
            pl.BlockSpec(memory_space=pl.ANY),
        ],
        out_specs=pl.BlockSpec((N_HEADS, DEC_SEQ, KV_LORA), lambda b, pt: (0, b, 0)),
        scratch_shapes=[
            pltpu.VMEM((2, n * PAGE_SIZE, KV_LORA), F32),
            pltpu.VMEM((2, D_ROPE, n * PAGE_SIZE), F32),
            pltpu.SemaphoreType.DMA((2, 2)),
            pltpu.VMEM((2, QROWS, SATTN_KEYS), F32),
            pltpu.VMEM((2, SATTN_KEYS, KV_LORA), BF16),
        ],
    )
    return pl.pallas_call(
        functools.partial(_sattn_body, layer),
        grid_spec=grid_spec,
        out_shape=jax.ShapeDtypeStruct((N_HEADS, NS_ROWS, KV_LORA), F32),
        compiler_params=_cparams(("arbitrary",)),
        name="attn_sample",
    )(page_table, qext, ckv_all, kpe_all, cache_ckv, cache_kpet)


def _oproj_prompt_body(x_ref, o_ref, wo_ref, out_ref):
    out_ref[...] = x_ref[...] + _dot(o_ref[...], wo_ref[...])


def oproj_prompt(x, o, wo):
    return pl.pallas_call(
        _oproj_prompt_body,
        grid=(NP_ROWS // TM,),
        in_specs=[pl.BlockSpec((TM, D_MODEL), lambda i: (i, 0)),
                  pl.BlockSpec((TM, N_HEADS * D_V), lambda i: (i, 0)),
                  _const_spec(wo.shape)],
        out_specs=pl.BlockSpec((TM, D_MODEL), lambda i: (i, 0)),
        out_shape=jax.ShapeDtypeStruct((ROWS, D_MODEL), F32),
        input_output_aliases={0: 0},
        compiler_params=_cparams(("parallel",)),
        name="oproj_prompt",
    )(x, o, wo)


def _oproj_sample_body(x_ref, ol_ref, wuv_ref, wo_ref, out_ref):
    acc = x_ref[...]
    for p in range(N_HEADS // 2):
        pair = jnp.concatenate([ol_ref[2 * p], ol_ref[2 * p + 1]], axis=1).astype(BF16)
        o_pair = _dot(pair, wuv_ref[p]).astype(BF16)
        acc = acc + _dot(o_pair, wo_ref[p * 2 * D_V:(p + 1) * 2 * D_V, :])
    out_ref[...] = acc


def oproj_sample(x, o_lat, wuv_pair, wo):
    row0 = NP_ROWS // TM
    return pl.pallas_call(
        _oproj_sample_body,
        grid=(NS_ROWS // TM,),
        in_specs=[pl.BlockSpec((TM, D_MODEL), lambda i: (i + row0, 0)),
                  pl.BlockSpec((N_HEADS, TM, KV_LORA), lambda i: (0, i, 0)),
                  _const_spec(wuv_pair.shape),
                  _const_spec(wo.shape)],
        out_specs=pl.BlockSpec((TM, D_MODEL), lambda i: (i + row0, 0)),
        out_shape=jax.ShapeDtypeStruct((ROWS, D_MODEL), F32),
        input_output_aliases={0: 0},
        compiler_params=_cparams(("parallel",)),
        name="oproj_sample",
    )(x, o_lat, wuv_pair, wo)


S5_Q = DEC_SEQ
S5_BLK = LANES // GROUP_SIZE
S5_NBLK = N_GROUPS // S5_BLK
S5_HALF = S5_BLK * STATE_DIM
S5_CT = 64
S5_CHUNKS = SEQ // S5_Q


def _rms_chunked_body(x_ref, g_ref, o_ref):
    o_ref[...] = _rms(x_ref[...], g_ref[...]).reshape(o_ref.shape)


def rmsnorm_chunked(x, g):
    nci = S5_CHUNKS // S5_CT
    out = pl.pallas_call(
        _rms_chunked_body,
        grid=(BATCH, nci),
        in_specs=[pl.BlockSpec((S5_CT * S5_Q, D_MODEL), lambda b, c: (b * nci + c, 0)),
                  _const_spec((1, D_MODEL))],
        out_specs=pl.BlockSpec((S5_CT, None, S5_Q, D_MODEL), lambda b, c: (c, b, 0, 0)),
        out_shape=jax.ShapeDtypeStruct((S5_CHUNKS, BATCH, S5_Q, D_MODEL), F32),
        compiler_params=_cparams(("parallel", "parallel")),
        name="rmsnorm_chunked",
    )(x, g.reshape(1, D_MODEL))
    return out.reshape(NP_ROWS, D_MODEL)


def _s5_body(n_chunks, nb, h_ref, t_ref, m_ref, n_ref, a_ref, h0_ref, y_ref, hf_ref,
             s_s, p_s, st_s):
    rows = n_chunks * nb

    @pl.when(pl.program_id(1) == 0)
    def _():
        st_s[...] = h0_ref[...]

    u = jnp.concatenate([h_ref[pl.ds(k, rows, stride=S5_Q), :].astype(BF16)
                         for k in range(S5_Q)], axis=1)
    s_s[...] = _dot(u, m_ref[...])
    are = a_ref[:, :S5_HALF]
    aim = a_ref[:, S5_HALF:]

    def step(c, h):
        r0 = pl.multiple_of(c * nb, nb)
        p_s[pl.ds(r0, nb), :] = h
        s = s_s[pl.ds(r0, nb), :]
        hre, him = h[:, :S5_HALF], h[:, S5_HALF:]
        nre = are * hre - aim * him + s[:, :S5_HALF]
        nim = are * him + aim * hre + s[:, S5_HALF:]
        return jnp.concatenate([nre, nim], axis=1)

    h = lax.fori_loop(0, n_chunks, step, st_s[...])
    st_s[...] = h
    hf_ref[...] = h
    y = _dot(u, t_ref[...]) + _dot(p_s[...].astype(BF16), n_ref[...])
    for l in range(S5_Q):
        y_ref[pl.ds(l, rows, stride=S5_Q), :] = y[:, l * LANES:(l + 1) * LANES]


def s5_scan(h, tabs, h0, n_chunks, nb):
    tbig, mbig, nbig, abig = tabs
    rows = n_chunks * nb
    tile = rows * S5_Q
    width = S5_Q * LANES

    def bspec(shape):
        return pl.BlockSpec((None,) + shape, lambda j, c: (j, 0, 0))

    return pl.pallas_call(
        functools.partial(_s5_body, n_chunks, nb),
        grid=(S5_NBLK, h.shape[0] // tile),
        in_specs=[pl.BlockSpec((tile, LANES), lambda j, c: (c, j)),
                  bspec((width, width)), bspec((width, 2 * S5_HALF)), bspec((2 * S5_HALF, width)),
                  bspec((1, 2 * S5_HALF)), bspec((nb, 2 * S5_HALF))],
        out_specs=[pl.BlockSpec((tile, LANES), lambda j, c: (c, j)), bspec((nb, 2 * S5_HALF))],
        out_shape=[jax.ShapeDtypeStruct(h.shape, F32),
                   jax.ShapeDtypeStruct((S5_NBLK, nb, 2 * S5_HALF), F32)],
        scratch_shapes=[pltpu.VMEM((rows, 2 * S5_HALF), F32), pltpu.VMEM((rows, 2 * S5_HALF), F32),
                        pltpu.VMEM((nb, 2 * S5_HALF), F32)],
        compiler_params=_cparams(("parallel", "arbitrary")),
        name="s5_scan",
    )(h, tbig, mbig, nbig, abig, h0)


def _glu_body(x_ref, h_ref, y_ref, d_ref, w_ref, o_ref):
    x = x_ref[...]
    y = y_ref[...].reshape(x.shape) + d_ref[...] * h_ref[...].reshape(x.shape)
    gl = jax.nn.gelu(y).astype(BF16)
    ab = _dot(gl, w_ref[...])
    o_ref[...] = x + ab[:, :D_MODEL] * jax.nn.sigmoid(ab[:, D_MODEL:])


def glu(x, h, y, d, w, chunked):
    if chunked:
        nci = S5_CHUNKS // S5_CT
        grid = (BATCH, nci)
        x_spec = pl.BlockSpec((S5_CT * S5_Q, D_MODEL), lambda b, c: (b * nci + c, 0))
        hy_spec = pl.BlockSpec((S5_CT, None, S5_Q, D_MODEL), lambda b, c: (c, b, 0, 0))
        h = h.reshape(S5_CHUNKS, BATCH, S5_Q, D_MODEL)
        y = y.reshape(S5_CHUNKS, BATCH, S5_Q, D_MODEL)
    else:
        blk0 = NP_ROWS // TM
        grid = (NS_ROWS // TM,)
        x_spec = pl.BlockSpec((TM, D_MODEL), lambda i: (i + blk0, 0))
        hy_spec = pl.BlockSpec((TM, D_MODEL), lambda i: (i, 0))
    return pl.pallas_call(
        _glu_body,
        grid=grid,
        in_specs=[x_spec, hy_spec, hy_spec, _const_spec((1, D_MODEL)), _const_spec(w.shape)],
        out_specs=x_spec,
        out_shape=jax.ShapeDtypeStruct((ROWS, D_MODEL), F32),
        input_output_aliases={0: 0},
        compiler_params=_cparams(("parallel",) * len(grid)),
        name="glu_prompt" if chunked else "glu_sample",
    )(x, h, y, d.reshape(1, D_MODEL), w)


def _prep_mla(w_in, g_q, g_kv, w_q_up, w_kv_up, w_o):
    z = lambda *s: jnp.zeros(s, F32)
    wk_pe = w_in[:, Q_LORA + KV_LORA:]
    half = D_ROPE // 2
    win = jnp.concatenate(
        [w_in[:, :Q_LORA + KV_LORA], wk_pe, z(D_MODEL, LANES - D_ROPE),
         -wk_pe[:, half:], wk_pe[:, :half], z(D_MODEL, LANES - D_ROPE)], axis=1)
    wq3 = w_q_up.reshape(Q_LORA, N_HEADS, D_NOPE + D_ROPE)
    nope, x1, x2 = wq3[..., :D_NOPE], wq3[..., D_NOPE:D_NOPE + half], wq3[..., D_NOPE + half:]
    zpad = z(Q_LORA, N_HEADS, HEAD_PAD - D_NOPE - D_ROPE)
    wq = jnp.concatenate([nope, x1, x2, zpad], axis=-1).reshape(Q_LORA, N_HEADS * HEAD_PAD)
    wqs = jnp.concatenate([z(Q_LORA, N_HEADS, D_NOPE), -x2, x1, zpad], axis=-1)
    wqs = wqs.reshape(Q_LORA, N_HEADS * HEAD_PAD)
    wkv3 = w_kv_up.reshape(KV_LORA, N_HEADS, D_NOPE + D_V)
    w_uk, w_uv = wkv3[..., :D_NOPE], wkv3[..., D_NOPE:]
    wk = jnp.concatenate([w_uk, z(KV_LORA, N_HEADS, HEAD_PAD - D_NOPE)], axis=-1)
    wk = wk.reshape(KV_LORA, N_HEADS * HEAD_PAD)
    wv = w_uv.reshape(KV_LORA, N_HEADS * D_V).T
    wuk_ext = jnp.zeros((N_HEADS, HEAD_PAD, QEXT), F32)
    wuk_ext = wuk_ext.at[:, :D_NOPE, :KV_LORA].set(jnp.transpose(w_uk, (1, 2, 0)))
    wuk_ext = wuk_ext.at[:, D_NOPE:D_NOPE + D_ROPE, KV_LORA:KV_LORA + D_ROPE].set(
        jnp.broadcast_to(jnp.eye(D_ROPE, dtype=F32), (N_HEADS, D_ROPE, D_ROPE)))
    uv = jnp.transpose(w_uv, (1, 0, 2)).reshape(N_HEADS // 2, 2, KV_LORA, D_V)
    wuv_pair = jnp.zeros((N_HEADS // 2, 2 * KV_LORA, 2 * D_V), F32)
    wuv_pair = wuv_pair.at[:, :KV_LORA, :D_V].set(uv[:, 0])
    wuv_pair = wuv_pair.at[:, KV_LORA:, D_V:].set(uv[:, 1])
    return dict(win=win.astype(BF16), gq=g_q.reshape(1, Q_LORA), gkv=g_kv.reshape(1, KV_LORA),
                wq=wq.astype(BF16), wqs=wqs.astype(BF16), wk=wk.astype(BF16), wv=wv.astype(BF16),
                wuk_ext=wuk_ext.astype(BF16), wuv_pair=wuv_pair.astype(BF16), wo=w_o.astype(BF16))


def _rope_tabs(pos, reps):
    half = D_ROPE // 2
    inv = ROPE_THETA ** (-jnp.arange(half, dtype=F32) / half)
    ang = pos.astype(F32)[:, None] * inv[None, :]
    cos, sin = jnp.cos(ang), jnp.sin(ang)
    n = pos.shape[0]
    one, zero = jnp.ones((n, D_NOPE), F32), jnp.zeros((n, D_NOPE), F32)
    zpad = jnp.zeros((n, HEAD_PAD - D_NOPE - D_ROPE), F32)
    cq = jnp.concatenate([one, cos, cos, zpad], axis=1)
    sq = jnp.concatenate([zero, sin, sin, zpad], axis=1)
    kpad = jnp.zeros((n, LANES - D_ROPE), F32)
    ck = jnp.concatenate([cos, cos, kpad], axis=1)
    sk = jnp.concatenate([sin, sin, kpad], axis=1)
    cq, sq = jnp.tile(cq, (reps, 2)), jnp.tile(sq, (reps, 2))
    return cq, sq, jnp.tile(ck, (reps, 1)), jnp.tile(sk, (reps, 1))


def _prep_s5(a_re, a_im, log_dt, b_re, b_im, c_re, c_im, q):
    hi = lax.Precision.HIGHEST
    dt = jnp.exp(log_dt)[:, None]
    dre, dim = dt * a_re, dt * a_im
    mag = jnp.exp(dre)
    abr, abi = mag * jnp.cos(dim), mag * jnp.sin(dim)
    den = a_re * a_re + a_im * a_im
    xr, xi = abr - 1.0, abi
    fr, fi = (xr * a_re + xi * a_im) / den, (xi * a_re - xr * a_im) / den
    bbr = fr[..., None] * b_re - fi[..., None] * b_im
    bbi = fr[..., None] * b_im + fi[..., None] * b_re
    j = jnp.arange(q + 1, dtype=F32)[:, None, None]
    pmag = jnp.exp(j * dre)
    pwr, pwi = pmag * jnp.cos(j * dim), pmag * jnp.sin(j * dim)
    pbr = pwr[..., None] * bbr - pwi[..., None] * bbi
    pbi = pwr[..., None] * bbi + pwi[..., None] * bbr
    kk = (jnp.einsum("gtp,jgps->jgst", c_re, pbr[:q], precision=hi)
          - jnp.einsum("gtp,jgps->jgst", c_im, pbi[:q], precision=hi))
    lag = jnp.arange(q)[None, :] - jnp.arange(q)[:, None]
    tk = kk[jnp.clip(lag, 0, q - 1)]
    tk = jnp.where((lag >= 0)[:, :, None, None, None], tk, 0.0)
    g = a_re.shape[0]
    t = jnp.transpose(tk, (2, 0, 3, 1, 4)).reshape(g, q * GROUP_SIZE, q * GROUP_SIZE)
    mre = jnp.transpose(pbr[:q][::-1], (1, 0, 3, 2)).reshape(g, q * GROUP_SIZE, STATE_DIM)
    mim = jnp.transpose(pbi[:q][::-1], (1, 0, 3, 2)).reshape(g, q * GROUP_SIZE, STATE_DIM)
    ncr = pwr[1:, :, None, :] * c_re[None] - pwi[1:, :, None, :] * c_im[None]
    nci = pwr[1:, :, None, :] * c_im[None] + pwi[1:, :, None, :] * c_re[None]
    nre = jnp.transpose(ncr, (1, 3, 0, 2)).reshape(g, STATE_DIM, q * GROUP_SIZE)
    nim = jnp.transpose(nci, (1, 3, 0, 2)).reshape(g, STATE_DIM, q * GROUP_SIZE)
    are, aim = pwr[q][:, None, :], pwi[q][:, None, :]
    return t, mre, mim, nre, nim, are, aim


def _s5_tables_body(at_ref, am_ref, an_ref, et_ref, em_ref, t_ref, m_ref, n_ref):
    width = S5_Q * LANES
    row = lax.broadcasted_iota(jnp.int32, (width, width), 0)
    col = lax.broadcasted_iota(jnp.int32, (width, width), 1)

    def expand(a_ref, e_ref, row_shift, col_shift):
        full = _dot(a_ref[...].astype(BF16), e_ref[...])
        keep = ((row >> row_shift) & (S5_BLK - 1)) == ((col >> col_shift) & (S5_BLK - 1))
        return jnp.where(keep, full, 0.0).astype(BF16)

    gs, sd = GROUP_SIZE.bit_length() - 1, STATE_DIM.bit_length() - 1
    t_ref[...] = expand(at_ref, et_ref, gs, gs)
    m_ref[...] = expand(am_ref, em_ref, gs, sd)
    n_ref[...] = expand(an_ref, et_ref, sd, gs)


def _s5_block_tables(p):
    t, mre, mim, nre, nim, are, aim = _prep_s5(*p, S5_Q)
    q = S5_Q
    width = q * LANES
    assert 2 * S5_HALF == width and q * GROUP_SIZE == LANES and 2 * STATE_DIM == LANES

    def rows_kgs(a):
        a5 = a.reshape(S5_NBLK, S5_BLK, q, GROUP_SIZE, LANES)
        return jnp.transpose(a5, (0, 2, 1, 3, 4)).reshape(S5_NBLK, width, LANES)

    at = rows_kgs(t)
    am = rows_kgs(jnp.concatenate([mre, mim], axis=-1))
    an = jnp.stack([nre.reshape(S5_NBLK, S5_BLK, STATE_DIM, LANES),
                    -nim.reshape(S5_NBLK, S5_BLK, STATE_DIM, LANES)], axis=1)
    an = an.reshape(S5_NBLK, width, LANES)
    r = jnp.arange(LANES)[:, None]
    c = jnp.arange(width)[None, :]
    et = ((r // GROUP_SIZE == c // LANES) & (r % GROUP_SIZE == c % GROUP_SIZE)).astype(BF16)
    em = ((r // STATE_DIM == c // S5_HALF) & (r % STATE_DIM == c % STATE_DIM)).astype(BF16)
    cspec = pl.BlockSpec((None, width, LANES), lambda j: (j, 0, 0))
    ospec = pl.BlockSpec((None, width, width), lambda j: (j, 0, 0))
    big = jax.ShapeDtypeStruct((S5_NBLK, width, width), BF16)
    tbig, mbig, nbig = pl.pallas_call(
        _s5_tables_body,
        grid=(S5_NBLK,),
        in_specs=[cspec, cspec, cspec, _const_spec((LANES, width)), _const_spec((LANES, width))],
        out_specs=[ospec, ospec, ospec],
        out_shape=[big, big, big],
        compiler_params=_cparams(("parallel",)),
        name="s5_tables",
    )(at, am, an, et, em)
    abig = jnp.concatenate([are.reshape(S5_NBLK, 1, S5_HALF), aim.reshape(S5_NBLK, 1, S5_HALF)], axis=2)
    return tbig, mbig, nbig, abig


def _state_to_blocks(re, im):
    nb = re.shape[0]
    f = lambda a: jnp.transpose(a.reshape(nb, S5_NBLK, S5_HALF), (1, 0, 2))
    return jnp.concatenate([f(re), f(im)], axis=2)


def _blocks_to_state(hf):
    nb = hf.shape[1]
    f = lambda a: jnp.transpose(a, (1, 0, 2)).reshape(nb, N_GROUPS, STATE_DIM)
    return f(hf[..., :S5_HALF]), f(hf[..., S5_HALF:])


def _s5_layer(x, g_mix, p, d, w_glu, h0re, h0im):
    tabs = _s5_block_tables(p)
    hp = rmsnorm_chunked(x, g_mix)
    hs = rmsnorm(x, g_mix, F32, NP_ROWS, NS_ROWS)
    zero = jnp.zeros((S5_NBLK, BATCH, 2 * S5_HALF), F32)
    yp, hfp = s5_scan(hp, tabs, zero, S5_CT, BATCH)
    ys, hfs = s5_scan(hs, tabs, _state_to_blocks(h0re, h0im), 1, DEC_BATCH)
    x = glu(x, hp, yp, d, w_glu, True)
    x = glu(x, hs, ys, d, w_glu, False)
    return (x,) + _blocks_to_state(hfp) + _blocks_to_state(hfs)


def kernel(x_prompt, x_sample, cache_ckv, cache_kpe, state_ssm_re, state_ssm_im, page_table, norm_mix, norm_mlp, norm_final, mla_w_in, mla_q_norm, mla_kv_norm, mla_w_q_up, mla_w_kv_up, mla_w_o, ssm_a_re, ssm_a_im, ssm_log_dt, ssm_b_re, ssm_b_im, ssm_c_re, ssm_c_im, ssm_d, ssm_w_glu, mlp_w1, mlp_w2):
    x = jnp.concatenate([x_prompt.reshape(NP_ROWS, D_MODEL), x_sample.reshape(NS_ROWS, D_MODEL)], axis=0)
    tabs_p = _rope_tabs(jnp.arange(SEQ, dtype=jnp.int32), 1)
    tabs_s = _rope_tabs(PAST_LEN + jnp.arange(DEC_SEQ, dtype=jnp.int32), TM_PROJ // DEC_SEQ)
    cache_kpet = jnp.swapaxes(cache_kpe, 2, 3)
    p_ckv, p_kpe, p_re, p_im = [], [], [], []
    s_ckv, s_kpe, s_re, s_im = [], [], [], []
    for i in range(DEPTH):
        j = i // 2
        if i % 2 == 0:
            w = _prep_mla(mla_w_in[j], mla_q_norm[j], mla_kv_norm[j], mla_w_q_up[j],
                          mla_w_kv_up[j], mla_w_o[j])
            ckv_p, kpe_p, q_p, k_p, v_p = mla_proj(x, False, norm_mix[i], w, tabs_p)
            ckv_s, kpe_s, qext = mla_proj(x, True, norm_mix[i], w, tabs_s)
            o_p = flash_prompt(q_p, k_p, v_p)
            o_lat = attn_sample(j, page_table, qext, ckv_s, kpe_s, cache_ckv, cache_kpet)
            x = oproj_prompt(x, o_p, w["wo"])
            x = oproj_sample(x, o_lat, w["wuv_pair"], w["wo"])
            p_ckv.append(ckv_p.reshape(BATCH, SEQ, KV_LORA))
            p_kpe.append(kpe_p.reshape(BATCH, SEQ, D_ROPE))
            s_ckv.append(ckv_s.reshape(DEC_BATCH, DEC_SEQ, KV_LORA))
            s_kpe.append(kpe_s.reshape(DEC_BATCH, DEC_SEQ, D_ROPE))
        else:
            p = (ssm_a_re[j], ssm_a_im[j], ssm_log_dt[j], ssm_b_re[j], ssm_b_im[j],
                 ssm_c_re[j], ssm_c_im[j])
            x, pre, pim, sre, sim = _s5_layer(x, norm_mix[i], p, ssm_d[j], ssm_w_glu[j].astype(BF16),
                                              state_ssm_re[j], state_ssm_im[j])
            p_re.append(pre); p_im.append(pim); s_re.append(sre); s_im.append(sim)
        x = mlp(x, norm_mlp[i], mlp_w1[i].astype(BF16), mlp_w2[i].astype(BF16))
    y_prompt = rmsnorm(x, norm_final, F32, 0, NP_ROWS).reshape(BATCH, SEQ, D_MODEL)
    y_sample = rmsnorm(x, norm_final, F32, NP_ROWS, NS_ROWS).reshape(DEC_BATCH, DEC_SEQ, D_MODEL)
    return (y_prompt, y_sample,
            jnp.stack(p_ckv), jnp.stack(p_kpe), jnp.stack(p_re), jnp.stack(p_im),
            jnp.stack(s_ckv), jnp.stack(s_kpe), jnp.stack(s_re), jnp.stack(s_im))
```

```python
import functools
import math

import jax
import jax.numpy as jnp
from jax import lax
from jax.experimental import pallas as pl
from jax.experimental.pallas import tpu as pltpu

F32 = jnp.float32
BF16 = jnp.bfloat16

D_MODEL = 1024
BATCH = 8
SEQ = 2048
DEPTH = 4
DEC_BATCH = 128
DEC_SEQ = 8
PAST_LEN = 8192
PAGE_SIZE = 128
N_PAGES = PAST_LEN // PAGE_SIZE
N_HEADS = 16
D_NOPE = 64
D_ROPE = 32
D_V = 64
Q_LORA = 512
KV_LORA = 256
ROPE_THETA = 10000.0
GROUP_SIZE = 16
N_GROUPS = D_MODEL // GROUP_SIZE
STATE_DIM = 64
D_FF = 4 * D_MODEL
EPS = 1e-6
SCALE = (D_NOPE + D_ROPE) ** -0.5
EXP2_SCALE = SCALE * math.log2(math.e)

NP_ROWS = BATCH * SEQ
NS_ROWS = DEC_BATCH * DEC_SEQ
ROWS = NP_ROWS + NS_ROWS

LANES = 128
HEAD_PAD = 128
QEXT = KV_LORA + LANES
VMEM_LIMIT = 52 * 1024 * 1024

TM = 512
TM_PROJ = 256
TQ = 256
PAGES_PER_STEP = 32


def _cparams(sem):
    return pltpu.CompilerParams(dimension_semantics=sem, vmem_limit_bytes=VMEM_LIMIT)


def _rms(x, g):
    return x * lax.rsqrt(jnp.mean(x * x, axis=-1, keepdims=True) + EPS) * g


def _dot(a, b):
    return jnp.dot(a, b, preferred_element_type=F32)


def _dot_nt(a, b):
    return lax.dot_general(a, b, (((1,), (1,)), ((), ())), preferred_element_type=F32)


def _const_spec(shape):
    nd = len(shape)
    return pl.BlockSpec(shape, lambda *_: (0,) * nd)


def _rms_body(x_ref, g_ref, o_ref):
    o_ref[...] = _rms(x_ref[...], g_ref[...]).astype(o_ref.dtype)


def rmsnorm(x, g, dtype, row0=0, rows=ROWS):
    blk0 = row0 // TM
    return pl.pallas_call(
        _rms_body,
        grid=(rows // TM,),
        in_specs=[pl.BlockSpec((TM, D_MODEL), lambda i: (i + blk0, 0)), _const_spec((1, D_MODEL))],
        out_specs=pl.BlockSpec((TM, D_MODEL), lambda i: (i, 0)),
        out_shape=jax.ShapeDtypeStruct((rows, D_MODEL), dtype),
        compiler_params=_cparams(("parallel",)),
        name="rmsnorm",
    )(x, g.reshape(1, D_MODEL))


FF_CHUNK = 1024


def _glu_update(x, h, y, d, w_ref):
    gl = jax.nn.gelu(y + d * h).astype(BF16)
    ab = _dot(gl, w_ref[...])
    return x + ab[:, :D_MODEL] * jax.nn.sigmoid(ab[:, D_MODEL:])


def _mlp_body(pre, x_ref, *refs):
    x = x_ref[...]
    if pre == "oproj":
        a_ref, wa_ref, *refs = refs
        x = x + _dot(a_ref[...], wa_ref[...])
    elif pre == "glu":
        h_ref, y_ref, d_ref, wg_ref, *refs = refs
        x = _glu_update(x, h_ref[...].reshape(x.shape), y_ref[...].reshape(x.shape), d_ref[...], wg_ref)
    g_ref, w1_ref, w2_ref, o_ref = refs
    h = _rms(x, g_ref[...]).astype(BF16)
    acc = x
    for c in range(D_FF // FF_CHUNK):
        sl = slice(c * FF_CHUNK, (c + 1) * FF_CHUNK)
        z = jnp.maximum(_dot(h, w1_ref[:, sl]), 0.0)
        acc = acc + _dot((z * z).astype(BF16), w2_ref[sl, :])
    o_ref[...] = acc


def _layer_spec(shape, layer):
    nd = len(shape)
    return pl.BlockSpec((None,) + shape, lambda *_: (layer,) + (0,) * nd, pipeline_mode=pl.Buffered(1))


def mlp(x, layer, g, w1, w2, sample, pre=None, pre_args=()):
    if sample:
        blk0, grid = NP_ROWS // TM, (NS_ROWS // TM,)
        x_spec = pl.BlockSpec((TM, D_MODEL), lambda i: (i + blk0, 0))
    else:
        nci = S5_CHUNKS // S5_CT
        grid = (BATCH, nci)
        x_spec = pl.BlockSpec((TM, D_MODEL), lambda b, c: (b * nci + c, 0))
    pre_specs = []
    if pre == "oproj":
        o, wo = pre_args
        pre_specs = [pl.BlockSpec((TM, N_HEADS * D_V), lambda b, c: (b * nci + c, 0)),
                     pl.BlockSpec(wo.shape, lambda *_: (0, 0), pipeline_mode=pl.Buffered(1))]
    elif pre == "glu":
        h, y, d, wg, glu_layer = pre_args
        hy_spec = pl.BlockSpec((S5_CT, None, S5_Q, D_MODEL), lambda b, c: (c, b, 0, 0))
        pre_args = (h.reshape(S5_CHUNKS, BATCH, S5_Q, D_MODEL), y.reshape(S5_CHUNKS, BATCH, S5_Q, D_MODEL),
                    d.reshape(1, D_MODEL), wg)
        pre_specs = [hy_spec, hy_spec, _const_spec((1, D_MODEL)),
                     _layer_spec((D_MODEL, 2 * D_MODEL), glu_layer)]
    return pl.pallas_call(
        functools.partial(_mlp_body, pre),
        grid=grid,
        in_specs=[x_spec] + pre_specs + [_const_spec((1, D_MODEL)),
                                         _layer_spec((D_MODEL, D_FF), layer),
                                         _layer_spec((D_FF, D_MODEL), layer)],
        out_specs=x_spec,
        out_shape=jax.ShapeDtypeStruct((ROWS, D_MODEL), F32),
        input_output_aliases={0: 0},
        compiler_params=_cparams(("parallel",) * len(grid)),
        name="mlp_sample" if sample else ("mlp_" + (pre or "plain")),
    )(x, *pre_args, g.reshape(1, D_MODEL), w1, w2)


def _proj_body(sample, x_ref, gm_ref, win_ref, gq_ref, gkv_ref, wq_ref, wqs_ref,
               cq_ref, sq_ref, ck_ref, sk_ref, wa_ref, wb_ref, *outs):
    h = _rms(x_ref[...], gm_ref[...]).astype(BF16)
    proj = _dot(h, win_ref[...])
    cq = _rms(proj[:, :Q_LORA], gq_ref[...]).astype(BF16)
    ckv = _rms(proj[:, Q_LORA:Q_LORA + KV_LORA], gkv_ref[...])
    kpe = (proj[:, 768:896] * ck_ref[...] + proj[:, 896:1024] * sk_ref[...])
    ckv_ref, kpe_ref = outs[0], outs[1]
    ckv_ref[...] = ckv
    kpe_ref[...] = kpe[:, :D_ROPE]
    cq_tab = cq_ref[...]
    sq_tab = sq_ref[...]
    for p in range(N_HEADS // 2):
        sl = slice(p * 2 * HEAD_PAD, (p + 1) * 2 * HEAD_PAD)
        qp = _dot(cq, wq_ref[:, sl]) * cq_tab + _dot(cq, wqs_ref[:, sl]) * sq_tab
        qp = qp.astype(BF16)
        if sample:
            qext_ref = outs[2]
            for hh in range(2):
                head = 2 * p + hh
                qh = qp[:, hh * HEAD_PAD:(hh + 1) * HEAD_PAD]
                qext_ref[head] = _dot(qh, wa_ref[head])
        else:
            outs[2][:, sl] = qp
    if not sample:
        k_ref, vt_ref = outs[3], outs[4]
        ckv_b = ckv.astype(BF16)
        kpl = pltpu.roll(kpe, D_NOPE, axis=1)
        kpl2 = jnp.concatenate([kpl, kpl], axis=1)
        for p in range(N_HEADS // 2):
            sl = slice(p * 2 * HEAD_PAD, (p + 1) * 2 * HEAD_PAD)
            k_ref[:, sl] = ((_dot(ckv_b, wa_ref[:, sl]) + kpl2) * EXP2_SCALE).astype(BF16)
        vt_ref[...] = _dot_nt(wb_ref[...], ckv_b).astype(BF16)


def mla_proj(x, sample, gm, w, tabs):
    tm = TM_PROJ
    if sample:
        rows, row0, tab_map = NS_ROWS, NP_ROWS // tm, (lambda i: (0, 0))
    else:
        rows, row0, tab_map = NP_ROWS, 0, (lambda i: (i % (SEQ // tm), 0))
    cq, sq, ck, sk = tabs
    wa = w["wuk_ext"] if sample else w["wk"]
    wb = w["wv"]
    out_shape = [jax.ShapeDtypeStruct((rows, KV_LORA), F32),
                 jax.ShapeDtypeStruct((rows, D_ROPE), F32)]
    out_specs = [pl.BlockSpec((tm, KV_LORA), lambda i: (i, 0)),
                 pl.BlockSpec((tm, D_ROPE), lambda i: (i, 0))]
    if sample:
        out_shape.append(jax.ShapeDtypeStruct((N_HEADS, rows, QEXT), F32))
        out_specs.append(pl.BlockSpec((N_HEADS, tm, QEXT), lambda i: (0, i, 0)))
    else:
        out_shape += [jax.ShapeDtypeStruct((rows, N_HEADS * HEAD_PAD), BF16),
                      jax.ShapeDtypeStruct((rows, N_HEADS * HEAD_PAD), BF16),
                      jax.ShapeDtypeStruct((N_HEADS * D_V, rows), BF16)]
        out_specs += [pl.BlockSpec((tm, N_HEADS * HEAD_PAD), lambda i: (i, 0)),
                      pl.BlockSpec((tm, N_HEADS * HEAD_PAD), lambda i: (i, 0)),
                      pl.BlockSpec((N_HEADS * D_V, tm), lambda i: (0, i))]
    return pl.pallas_call(
        functools.partial(_proj_body, sample),
        grid=(rows // tm,),
        in_specs=[
            pl.BlockSpec((tm, D_MODEL), lambda i: (i + row0, 0)),
            _const_spec((1, D_MODEL)),
            _const_spec(w["win"].shape),
            _const_spec((1, Q_LORA)),
            _const_spec((1, KV_LORA)),
            _const_spec(w["wq"].shape),
            _const_spec(w["wqs"].shape),
            pl.BlockSpec((tm, 2 * HEAD_PAD), tab_map),
            pl.BlockSpec((tm, 2 * HEAD_PAD), tab_map),
            pl.BlockSpec((tm, LANES), tab_map),
            pl.BlockSpec((tm, LANES), tab_map),
            _const_spec(wa.shape),
            _const_spec(wb.shape),
        ],
        out_specs=out_specs,
        out_shape=out_shape,
        compiler_params=_cparams(("parallel",)),
        name="mla_proj_sample" if sample else "mla_proj_prompt",
    )(x, gm.reshape(1, D_MODEL), w["win"], w["gq"], w["gkv"], w["wq"], w["wqs"],
      cq, sq, ck, sk, wa, wb)


def _flash_body(q_ref, k_ref, vt_ref, o_ref, s_buf):
    key = lax.broadcasted_iota(jnp.int32, (TQ, TQ), 0)
    qry = lax.broadcasted_iota(jnp.int32, (TQ, TQ), 1)
    items = [(qi, hh) for qi in range(SEQ // TQ) for hh in range(2)]
    ones = jnp.ones((16, SEQ), BF16)

    def scores(n):
        qi, hh = items[n]
        q0 = qi * TQ
        hs = slice(hh * HEAD_PAD, (hh + 1) * HEAD_PAD)
        q = q_ref[q0:q0 + TQ, hs]
        sd = _dot_nt(k_ref[q0:q0 + TQ, hs], q)
        sd = jnp.where(key <= qry, sd, -jnp.inf)
        s_buf[n % 2, q0:q0 + TQ, :] = sd
        m = jnp.max(sd, axis=0, keepdims=True)
        if qi > 0:
            sm = _dot_nt(k_ref[0:q0, hs], q)
            s_buf[n % 2, 0:q0, :] = sm
            m = jnp.maximum(m, jnp.max(sm, axis=0, keepdims=True))
        return m

    def values(n, m):
        qi, hh = items[n]
        kv_len = (qi + 1) * TQ
        p = jnp.exp2(s_buf[n % 2, 0:kv_len, :] - m).astype(BF16)
        vt1 = jnp.concatenate([vt_ref[hh * D_V:(hh + 1) * D_V, 0:kv_len], ones[:, 0:kv_len]], axis=0)
        ot = _dot(vt1, p)
        return ot[:D_V] / ot[D_V:D_V + 1]

    m_next = scores(0)
    ots = []
    for n, (qi, hh) in enumerate(items):
        m = m_next
        if n + 1 < len(items):
            m_next = scores(n + 1)
        ots.append(values(n, m))
        if hh == 1:
            q0 = qi * TQ
            o_ref[q0:q0 + TQ, :] = jnp.concatenate(ots, axis=0).T.astype(o_ref.dtype)
            ots = []


def flash_prompt(q, k, vt):
    return pl.pallas_call(
        _flash_body,
        grid=(BATCH, N_HEADS // 2),
        in_specs=[
            pl.BlockSpec((SEQ, 2 * HEAD_PAD), lambda b, p: (b, p)),
            pl.BlockSpec((SEQ, 2 * HEAD_PAD), lambda b, p: (b, p)),
            pl.BlockSpec((2 * D_V, SEQ), lambda b, p: (p, b)),
        ],
        out_specs=pl.BlockSpec((SEQ, 2 * D_V), lambda b, p: (b, p)),
        out_shape=jax.ShapeDtypeStruct((NP_ROWS, N_HEADS * D_V), BF16),
        scratch_shapes=[pltpu.VMEM((2, SEQ, TQ), F32)],
        compiler_params=_cparams(("parallel", "parallel")),
        name="flash_prompt",
    )(q, k, vt)


N_CHUNKS = N_PAGES // PAGES_PER_STEP
assert N_CHUNKS % 2 == 0 and N_CHUNKS * PAGES_PER_STEP == N_PAGES
SATTN_KEYS = 2048
QROWS = N_HEADS * DEC_SEQ


def _sattn_body(layer, pt_ref, q_ref, cnew_ref, knew_ref, ckv_hbm, kpet_hbm, o_ref,
                cbuf, kbuf, sem, s_buf, ck_b):
    n = PAGES_PER_STEP
    b = pl.program_id(0)

    def chunk_copies(req, c, slot):
        cps = []
        for k in range(n):
            page = pt_ref[req, c * n + k]
            rows = pl.ds(k * PAGE_SIZE, PAGE_SIZE)
            cps.append(pltpu.make_async_copy(ckv_hbm.at[layer, page], cbuf.at[slot, rows, :],
                                             sem.at[slot, 0]))
            cps.append(pltpu.make_async_copy(kpet_hbm.at[layer, page], kbuf.at[slot, :, rows],
                                             sem.at[slot, 1]))
        return cps

    def start(req, c, slot):
        for cp in chunk_copies(req, c, slot):
            cp.start()

    def wait(req, c, slot):
        for cp in chunk_copies(req, c, slot):
            cp.wait()

    @pl.when(b == 0)
    def _():
        for c in range(2):
            start(0, c, c)

    q = q_ref[...].reshape(QROWS, QEXT)
    qlat = q[:, :KV_LORA].astype(BF16)
    qpe = q[:, KV_LORA:KV_LORA + D_ROPE].astype(BF16)

    n_sb = n * PAGE_SIZE // SATTN_KEYS
    items = [(c, sb) for c in range(N_CHUNKS) for sb in range(n_sb)]

    def scores(i):
        c, sb = items[i]
        keys = slice(sb * SATTN_KEYS, (sb + 1) * SATTN_KEYS)
        if sb == 0:
            wait(b, c, c % 2)
        ck = cbuf[c % 2, keys, :].astype(BF16)
        kt = kbuf[c % 2, :, keys].astype(BF16)
        ck_b[i % 2] = ck
        s = _dot_nt(qlat, ck) + _dot(qpe, kt)
        s_buf[i % 2] = s
        if sb == n_sb - 1:
            if c + 2 < N_CHUNKS:
                start(b, c + 2, c % 2)
            else:
                @pl.when(b + 1 < DEC_BATCH)
                def _():
                    start(b + 1, c + 2 - N_CHUNKS, c % 2)
        return jnp.max(s, axis=-1, keepdims=True)

    bm_next = scores(0)

    pad = PAGE_SIZE - DEC_SEQ
    cn = jnp.concatenate([cnew_ref[...], jnp.zeros((pad, KV_LORA), F32)], axis=0).astype(BF16)
    kn = jnp.concatenate([knew_ref[...], jnp.zeros((pad, D_ROPE), F32)], axis=0).astype(BF16)
    s = _dot_nt(qlat, cn) + _dot_nt(qpe, kn)
    tok = lax.broadcasted_iota(jnp.int32, (QROWS, PAGE_SIZE), 0) % DEC_SEQ
    key = lax.broadcasted_iota(jnp.int32, (QROWS, PAGE_SIZE), 1)
    s = jnp.where(key <= tok, s, -jnp.inf)
    m = jnp.max(s, axis=-1, keepdims=True)
    p = jnp.exp2((s - m) * EXP2_SCALE)
    l = jnp.sum(p, axis=-1, keepdims=True)
    acc = _dot(p.astype(BF16), cn)

    for i in range(len(items)):
        bm = bm_next
        if i + 1 < len(items):
            bm_next = scores(i + 1)
        m_new = jnp.maximum(m, bm)
        alpha = jnp.exp2((m - m_new) * EXP2_SCALE)
        p = jnp.exp2((s_buf[i % 2] - m_new) * EXP2_SCALE)
        l = alpha * l + jnp.sum(p, axis=-1, keepdims=True)
        acc = alpha * acc + _dot(p.astype(BF16), ck_b[i % 2])
        m = m_new

    o_ref[...] = (acc / l).reshape(N_HEADS, DEC_SEQ, KV_LORA)


def attn_sample(layer, page_table, qext, ckv_all, kpe_all, cache_ckv, cache_kpet):
    n = PAGES_PER_STEP
    grid_spec = pltpu.PrefetchScalarGridSpec(
        num_scalar_prefetch=1,
        grid=(DEC_BATCH,),
        in_specs=[
            pl.BlockSpec((N_HEADS, DEC_SEQ, QEXT), lambda b, pt: (0, b, 0)),
            pl.BlockSpec((DEC_SEQ, KV_LORA), lambda b, pt: (b, 0)),
            pl.BlockSpec((DEC_SEQ, D_ROPE), lambda b, pt: (b, 0)),
            pl.BlockSpec(memory_space=pl.ANY),
            pl.BlockSpec(memory_space=pl.ANY),
        ],
        out_specs=pl.BlockSpec((N_HEADS, DEC_SEQ, KV_LORA), lambda b, pt: (0, b, 0)),
        scratch_shapes=[
            pltpu.VMEM((2, n * PAGE_SIZE, KV_LORA), F32),
            pltpu.VMEM((2, D_ROPE, n * PAGE_SIZE), F32),
            pltpu.SemaphoreType.DMA((2, 2)),
            pltpu.VMEM((2, QROWS, SATTN_KEYS), F32),
            pltpu.VMEM((2, SATTN_KEYS, KV_LORA), BF16),
        ],
    )
    return pl.pallas_call(
        functools.partial(_sattn_body, layer),
        grid_spec=grid_spec,
        out_shape=jax.ShapeDtypeStruct((N_HEADS, NS_ROWS, KV_LORA), F32),
        compiler_params=_cparams(("arbitrary",)),
        name="attn_sample",
    )(page_table, qext, ckv_all, kpe_all, cache_ckv, cache_kpet)


def _oproj_sample_body(x_ref, ol_ref, wuv_ref, wo_ref, out_ref):
    acc = x_ref[...]
    for p in range(N_HEADS // 2):
        pair = jnp.concatenate([ol_ref[2 * p], ol_ref[2 * p + 1]], axis=1).astype(BF16)
        o_pair = _dot(pair, wuv_ref[p]).astype(BF16)
        acc = acc + _dot(o_pair, wo_ref[p * 2 * D_V:(p + 1) * 2 * D_V, :])
    out_ref[...] = acc


def oproj_sample(x, o_lat, wuv_pair, wo):
    row0 = NP_ROWS // TM
    return pl.pallas_call(
        _oproj_sample_body,
        grid=(NS_ROWS // TM,),
        in_specs=[pl.BlockSpec((TM, D_MODEL), lambda i: (i + row0, 0)),
                  pl.BlockSpec((N_HEADS, TM, KV_LORA), lambda i: (0, i, 0)),
                  _const_spec(wuv_pair.shape),
                  _const_spec(wo.shape)],
        out_specs=pl.BlockSpec((TM, D_MODEL), lambda i: (i + row0, 0)),
        out_shape=jax.ShapeDtypeStruct((ROWS, D_MODEL), F32),
        input_output_aliases={0: 0},
        compiler_params=_cparams(("parallel",)),
        name="oproj_sample",
    )(x, o_lat, wuv_pair, wo)


S5_Q = DEC_SEQ
S5_BLK = LANES // GROUP_SIZE
S5_NBLK = N_GROUPS // S5_BLK
S5_HALF = S5_BLK * STATE_DIM
S5_CT = 64
S5_CHUNKS = SEQ // S5_Q


def _rms_chunked_body(x_ref, g_ref, o_ref):
    o_ref[...] = _rms(x_ref[...], g_ref[...]).reshape(o_ref.shape)


def rmsnorm_chunked(x, g):
    nci = S5_CHUNKS // S5_CT
    out = pl.pallas_call(
        _rms_chunked_body,
        grid=(BATCH, nci),
        in_specs=[pl.BlockSpec((S5_CT * S5_Q, D_MODEL), lambda b, c: (b * nci + c, 0)),
                  _const_spec((1, D_MODEL))],
        out_specs=pl.BlockSpec((S5_CT, None, S5_Q, D_MODEL), lambda b, c: (c, b, 0, 0)),
        out_shape=jax.ShapeDtypeStruct((S5_CHUNKS, BATCH, S5_Q, D_MODEL), F32),
        compiler_params=_cparams(("parallel", "parallel")),
        name="rmsnorm_chunked",
    )(x, g.reshape(1, D_MODEL))
    return out.reshape(NP_ROWS, D_MODEL)


def _s5_body(n_chunks, nb, h_ref, t_ref, m_ref, n_ref, a_ref, h0_ref, y_ref, hf_ref,
             s_s, p_s, st_s):
    rows = n_chunks * nb

    @pl.when(pl.program_id(1) == 0)
    def _():
        st_s[...] = h0_ref[...]

    u = jnp.concatenate([h_ref[pl.ds(k, rows, stride=S5_Q), :].astype(BF16)
                         for k in range(S5_Q)], axis=1)
    s_s[...] = _dot(u, m_ref[...])
    are = a_ref[:, :S5_HALF]
    aim = a_ref[:, S5_HALF:]

    def step(c, h):
        r0 = pl.multiple_of(c * nb, nb)
        p_s[pl.ds(r0, nb), :] = h
        s = s_s[pl.ds(r0, nb), :]
        hre, him = h[:, :S5_HALF], h[:, S5_HALF:]
        nre = are * hre - aim * him + s[:, :S5_HALF]
        nim = are * him + aim * hre + s[:, S5_HALF:]
        return jnp.concatenate([nre, nim], axis=1)

    h = lax.fori_loop(0, n_chunks, step, st_s[...])
    st_s[...] = h
    hf_ref[...] = h
    y = _dot(u, t_ref[...]) + _dot(p_s[...].astype(BF16), n_ref[...])
    for l in range(S5_Q):
        y_ref[pl.ds(l, rows, stride=S5_Q), :] = y[:, l * LANES:(l + 1) * LANES]


def s5_scan(h, tabs, h0, n_chunks, nb):
    tbig, mbig, nbig, abig = tabs
    rows = n_chunks * nb
    tile = rows * S5_Q
    width = S5_Q * LANES

    def bspec(shape):
        return pl.BlockSpec((None,) + shape, lambda j, c: (j, 0, 0))

    return pl.pallas_call(
        functools.partial(_s5_body, n_chunks, nb),
        grid=(S5_NBLK, h.shape[0] // tile),
        in_specs=[pl.BlockSpec((tile, LANES), lambda j, c: (c, j)),
                  bspec((width, width)), bspec((width, 2 * S5_HALF)), bspec((2 * S5_HALF, width)),
                  bspec((1, 2 * S5_HALF)), bspec((nb, 2 * S5_HALF))],
        out_specs=[pl.BlockSpec((tile, LANES), lambda j, c: (c, j)), bspec((nb, 2 * S5_HALF))],
        out_shape=[jax.ShapeDtypeStruct(h.shape, F32),
                   jax.ShapeDtypeStruct((S5_NBLK, nb, 2 * S5_HALF), F32)],
        scratch_shapes=[pltpu.VMEM((rows, 2 * S5_HALF), F32), pltpu.VMEM((rows, 2 * S5_HALF), F32),
                        pltpu.VMEM((nb, 2 * S5_HALF), F32)],
        compiler_params=_cparams(("parallel", "arbitrary")),
        name="s5_scan",
    )(h, tbig, mbig, nbig, abig, h0)


def _glu_body(x_ref, h_ref, y_ref, d_ref, w_ref, o_ref):
    o_ref[...] = _glu_update(x_ref[...], h_ref[...], y_ref[...], d_ref[...], w_ref)


def glu_sample(x, h, y, d, w, layer):
    blk0 = NP_ROWS // TM
    x_spec = pl.BlockSpec((TM, D_MODEL), lambda i: (i + blk0, 0))
    hy_spec = pl.BlockSpec((TM, D_MODEL), lambda i: (i, 0))
    return pl.pallas_call(
        _glu_body,
        grid=(NS_ROWS // TM,),
        in_specs=[x_spec, hy_spec, hy_spec, _const_spec((1, D_MODEL)),
                  _layer_spec((D_MODEL, 2 * D_MODEL), layer)],
        out_specs=x_spec,
        out_shape=jax.ShapeDtypeStruct((ROWS, D_MODEL), F32),
        input_output_aliases={0: 0},
        compiler_params=_cparams(("parallel",)),
        name="glu_sample",
    )(x, h, y, d.reshape(1, D_MODEL), w)


def _prep_mla(w_in, g_q, g_kv, w_q_up, w_kv_up, w_o):
    z = lambda *s: jnp.zeros(s, F32)
    wk_pe = w_in[:, Q_LORA + KV_LORA:]
    half = D_ROPE // 2
    win = jnp.concatenate(
        [w_in[:, :Q_LORA + KV_LORA], wk_pe, z(D_MODEL, LANES - D_ROPE),
         -wk_pe[:, half:], wk_pe[:, :half], z(D_MODEL, LANES - D_ROPE)], axis=1)
    wq3 = w_q_up.reshape(Q_LORA, N_HEADS, D_NOPE + D_ROPE)
    nope, x1, x2 = wq3[..., :D_NOPE], wq3[..., D_NOPE:D_NOPE + half], wq3[..., D_NOPE + half:]
    zpad = z(Q_LORA, N_HEADS, HEAD_PAD - D_NOPE - D_ROPE)
    wq = jnp.concatenate([nope, x1, x2, zpad], axis=-1).reshape(Q_LORA, N_HEADS * HEAD_PAD)
    wqs = jnp.concatenate([z(Q_LORA, N_HEADS, D_NOPE), -x2, x1, zpad], axis=-1)
    wqs = wqs.reshape(Q_LORA, N_HEADS * HEAD_PAD)
    wkv3 = w_kv_up.reshape(KV_LORA, N_HEADS, D_NOPE + D_V)
    w_uk, w_uv = wkv3[..., :D_NOPE], wkv3[..., D_NOPE:]
    wk = jnp.concatenate([w_uk, z(KV_LORA, N_HEADS, HEAD_PAD - D_NOPE)], axis=-1)
    wk = wk.reshape(KV_LORA, N_HEADS * HEAD_PAD)
    wv = w_uv.reshape(KV_LORA, N_HEADS * D_V).T
    wuk_ext = jnp.zeros((N_HEADS, HEAD_PAD, QEXT), F32)
    wuk_ext = wuk_ext.at[:, :D_NOPE, :KV_LORA].set(jnp.transpose(w_uk, (1, 2, 0)))
    wuk_ext = wuk_ext.at[:, D_NOPE:D_NOPE + D_ROPE, KV_LORA:KV_LORA + D_ROPE].set(
        jnp.broadcast_to(jnp.eye(D_ROPE, dtype=F32), (N_HEADS, D_ROPE, D_ROPE)))
    uv = jnp.transpose(w_uv, (1, 0, 2)).reshape(N_HEADS // 2, 2, KV_LORA, D_V)
    wuv_pair = jnp.zeros((N_HEADS // 2, 2 * KV_LORA, 2 * D_V), F32)
    wuv_pair = wuv_pair.at[:, :KV_LORA, :D_V].set(uv[:, 0])
    wuv_pair = wuv_pair.at[:, KV_LORA:, D_V:].set(uv[:, 1])
    return dict(win=win.astype(BF16), gq=g_q.reshape(1, Q_LORA), gkv=g_kv.reshape(1, KV_LORA),
                wq=wq.astype(BF16), wqs=wqs.astype(BF16), wk=wk.astype(BF16), wv=wv.astype(BF16),
                wuk_ext=wuk_ext.astype(BF16), wuv_pair=wuv_pair.astype(BF16), wo=w_o.astype(BF16))


def _rope_tabs(pos, reps):
    half = D_ROPE // 2
    inv = ROPE_THETA ** (-jnp.arange(half, dtype=F32) / half)
    ang = pos.astype(F32)[:, None] * inv[None, :]
    cos, sin = jnp.cos(ang), jnp.sin(ang)
    n = pos.shape[0]
    one, zero = jnp.ones((n, D_NOPE), F32), jnp.zeros((n, D_NOPE), F32)
    zpad = jnp.zeros((n, HEAD_PAD - D_NOPE - D_ROPE), F32)
    cq = jnp.concatenate([one, cos, cos, zpad], axis=1)
    sq = jnp.concatenate([zero, sin, sin, zpad], axis=1)
    kpad = jnp.zeros((n, LANES - D_ROPE), F32)
    ck = jnp.concatenate([cos, cos, kpad], axis=1)
    sk = jnp.concatenate([sin, sin, kpad], axis=1)
    cq, sq = jnp.tile(cq, (reps, 2)), jnp.tile(sq, (reps, 2))
    return cq, sq, jnp.tile(ck, (reps, 1)), jnp.tile(sk, (reps, 1))


def _prep_s5(a_re, a_im, log_dt, b_re, b_im, c_re, c_im, q):
    hi = lax.Precision.HIGHEST
    dt = jnp.exp(log_dt)[:, None]
    dre, dim = dt * a_re, dt * a_im
    mag = jnp.exp(dre)
    abr, abi = mag * jnp.cos(dim), mag * jnp.sin(dim)
    den = a_re * a_re + a_im * a_im
    xr, xi = abr - 1.0, abi
    fr, fi = (xr * a_re + xi * a_im) / den, (xi * a_re - xr * a_im) / den
    bbr = fr[..., None] * b_re - fi[..., None] * b_im
    bbi = fr[..., None] * b_im + fi[..., None] * b_re
    j = jnp.arange(q + 1, dtype=F32)[:, None, None]
    pmag = jnp.exp(j * dre)
    pwr, pwi = pmag * jnp.cos(j * dim), pmag * jnp.sin(j * dim)
    pbr = pwr[..., None] * bbr - pwi[..., None] * bbi
    pbi = pwr[..., None] * bbi + pwi[..., None] * bbr
    kk = (jnp.einsum("gtp,jgps->jgst", c_re, pbr[:q], precision=hi)
          - jnp.einsum("gtp,jgps->jgst", c_im, pbi[:q], precision=hi))
    g = a_re.shape[0]
    taps = jnp.transpose(kk, (1, 2, 0, 3)).reshape(g, GROUP_SIZE, q * GROUP_SIZE)
    t = jnp.stack([jnp.pad(taps[..., :(q - k) * GROUP_SIZE], ((0, 0), (0, 0), (k * GROUP_SIZE, 0)))
                   for k in range(q)], axis=1).reshape(g, q * GROUP_SIZE, q * GROUP_SIZE)
    mre = jnp.transpose(pbr[:q][::-1], (1, 0, 3, 2)).reshape(g, q * GROUP_SIZE, STATE_DIM)
    mim = jnp.transpose(pbi[:q][::-1], (1, 0, 3, 2)).reshape(g, q * GROUP_SIZE, STATE_DIM)
    ncr = pwr[1:, :, None, :] * c_re[None] - pwi[1:, :, None, :] * c_im[None]
    nci = pwr[1:, :, None, :] * c_im[None] + pwi[1:, :, None, :] * c_re[None]
    nre = jnp.transpose(ncr, (1, 3, 0, 2)).reshape(g, STATE_DIM, q * GROUP_SIZE)
    nim = jnp.transpose(nci, (1, 3, 0, 2)).reshape(g, STATE_DIM, q * GROUP_SIZE)
    are, aim = pwr[q][:, None, :], pwi[q][:, None, :]
    return t, mre, mim, nre, nim, are, aim


def _s5_tables_body(at_ref, am_ref, an_ref, et_ref, em_ref, t_ref, m_ref, n_ref):
    width = S5_Q * LANES
    row = lax.broadcasted_iota(jnp.int32, (width, width), 0)
    col = lax.broadcasted_iota(jnp.int32, (width, width), 1)

    def expand(a_ref, e_ref, row_shift, col_shift):
        full = _dot(a_ref[...].astype(BF16), e_ref[...])
        keep = ((row >> row_shift) & (S5_BLK - 1)) == ((col >> col_shift) & (S5_BLK - 1))
        return jnp.where(keep, full, 0.0).astype(BF16)

    gs, sd = GROUP_SIZE.bit_length() - 1, STATE_DIM.bit_length() - 1
    t_ref[...] = expand(at_ref, et_ref, gs, gs)
    m_ref[...] = expand(am_ref, em_ref, gs, sd)
    n_ref[...] = expand(an_ref, et_ref, sd, gs)


def _s5_block_tables(p):
    t, mre, mim, nre, nim, are, aim = _prep_s5(*p, S5_Q)
    q = S5_Q
    width = q * LANES
    assert 2 * S5_HALF == width and q * GROUP_SIZE == LANES and 2 * STATE_DIM == LANES

    def rows_kgs(a):
        a5 = a.reshape(S5_NBLK, S5_BLK, q, GROUP_SIZE, LANES)
        return jnp.transpose(a5, (0, 2, 1, 3, 4)).reshape(S5_NBLK, width, LANES)

    at = rows_kgs(t)
    am = rows_kgs(jnp.concatenate([mre, mim], axis=-1))
    an = jnp.stack([nre.reshape(S5_NBLK, S5_BLK, STATE_DIM, LANES),
                    -nim.reshape(S5_NBLK, S5_BLK, STATE_DIM, LANES)], axis=1)
    an = an.reshape(S5_NBLK, width, LANES)
    r = jnp.arange(LANES)[:, None]
    c = jnp.arange(width)[None, :]
    et = ((r // GROUP_SIZE == c // LANES) & (r % GROUP_SIZE == c % GROUP_SIZE)).astype(BF16)
    em = ((r // STATE_DIM == c // S5_HALF) & (r % STATE_DIM == c % STATE_DIM)).astype(BF16)
    cspec = pl.BlockSpec((None, width, LANES), lambda j: (j, 0, 0))
    ospec = pl.BlockSpec((None, width, width), lambda j: (j, 0, 0))
    big = jax.ShapeDtypeStruct((S5_NBLK, width, width), BF16)
    tbig, mbig, nbig = pl.pallas_call(
        _s5_tables_body,
        grid=(S5_NBLK,),
        in_specs=[cspec, cspec, cspec, _const_spec((LANES, width)), _const_spec((LANES, width))],
        out_specs=[ospec, ospec, ospec],
        out_shape=[big, big, big],
        compiler_params=_cparams(("parallel",)),
        name="s5_tables",
    )(at, am, an, et, em)
    abig = jnp.concatenate([are.reshape(S5_NBLK, 1, S5_HALF), aim.reshape(S5_NBLK, 1, S5_HALF)], axis=2)
    return tbig, mbig, nbig, abig


def _state_to_blocks(re, im):
    nb = re.shape[0]
    f = lambda a: jnp.transpose(a.reshape(nb, S5_NBLK, S5_HALF), (1, 0, 2))
    return jnp.concatenate([f(re), f(im)], axis=2)


def _blocks_to_state(hf):
    nb = hf.shape[1]
    f = lambda a: jnp.transpose(a, (1, 0, 2)).reshape(nb, N_GROUPS, STATE_DIM)
    return f(hf[..., :S5_HALF]), f(hf[..., S5_HALF:])


def _s5_mixer(x, g_mix, p, h0re, h0im):
    tabs = _s5_block_tables(p)
    hp = rmsnorm_chunked(x, g_mix)
    hs = rmsnorm(x, g_mix, F32, NP_ROWS, NS_ROWS)
    zero = jnp.zeros((S5_NBLK, BATCH, 2 * S5_HALF), F32)
    yp, hfp = s5_scan(hp, tabs, zero, S5_CT, BATCH)
    ys, hfs = s5_scan(hs, tabs, _state_to_blocks(h0re, h0im), 1, DEC_BATCH)
    return (hp, yp, hs, ys) + _blocks_to_state(hfp) + _blocks_to_state(hfs)


def kernel(x_prompt, x_sample, cache_ckv, cache_kpe, state_ssm_re, state_ssm_im, page_table, norm_mix, norm_mlp, norm_final, mla_w_in, mla_q_norm, mla_kv_norm, mla_w_q_up, mla_w_kv_up, mla_w_o, ssm_a_re, ssm_a_im, ssm_log_dt, ssm_b_re, ssm_b_im, ssm_c_re, ssm_c_im, ssm_d, ssm_w_glu, mlp_w1, mlp_w2):
    x = jnp.concatenate([x_prompt.reshape(NP_ROWS, D_MODEL), x_sample.reshape(NS_ROWS, D_MODEL)], axis=0)
    tabs_p = _rope_tabs(jnp.arange(SEQ, dtype=jnp.int32), 1)
    tabs_s = _rope_tabs(PAST_LEN + jnp.arange(DEC_SEQ, dtype=jnp.int32), TM_PROJ // DEC_SEQ)
    cache_kpet = jnp.swapaxes(cache_kpe, 2, 3)
    w1b, w2b, wglu_b = mlp_w1.astype(BF16), mlp_w2.astype(BF16), ssm_w_glu.astype(BF16)
    p_ckv, p_kpe, p_re, p_im = [], [], [], []
    s_ckv, s_kpe, s_re, s_im = [], [], [], []
    for i in range(DEPTH):
        j = i // 2
        if i % 2 == 0:
            w = _prep_mla(mla_w_in[j], mla_q_norm[j], mla_kv_norm[j], mla_w_q_up[j],
                          mla_w_kv_up[j], mla_w_o[j])
            ckv_p, kpe_p, q_p, k_p, v_p = mla_proj(x, False, norm_mix[i], w, tabs_p)
            ckv_s, kpe_s, qext = mla_proj(x, True, norm_mix[i], w, tabs_s)
            o_p = flash_prompt(q_p, k_p, v_p)
            o_lat = attn_sample(j, page_table, qext, ckv_s, kpe_s, cache_ckv, cache_kpet)
            x = oproj_sample(x, o_lat, w["wuv_pair"], w["wo"])
            pre, pre_args = "oproj", (o_p, w["wo"])
            p_ckv.append(ckv_p.reshape(BATCH, SEQ, KV_LORA))
            p_kpe.append(kpe_p.reshape(BATCH, SEQ, D_ROPE))
            s_ckv.append(ckv_s.reshape(DEC_BATCH, DEC_SEQ, KV_LORA))
            s_kpe.append(kpe_s.reshape(DEC_BATCH, DEC_SEQ, D_ROPE))
        else:
            p = (ssm_a_re[j], ssm_a_im[j], ssm_log_dt[j], ssm_b_re[j], ssm_b_im[j],
                 ssm_c_re[j], ssm_c_im[j])
            hp, yp, hs, ys, pre_, pim, sre, sim = _s5_mixer(x, norm_mix[i], p, state_ssm_re[j],
                                                            state_ssm_im[j])
            p_re.append(pre_); p_im.append(pim); s_re.append(sre); s_im.append(sim)
            x = glu_sample(x, hs, ys, ssm_d[j], wglu_b, j)
            pre, pre_args = "glu", (hp, yp, ssm_d[j], wglu_b, j)
        x = mlp(x, i, norm_mlp[i], w1b, w2b, False, pre, pre_args)
        x = mlp(x, i, norm_mlp[i], w1b, w2b, True)
    y_prompt = rmsnorm(x, norm_final, F32, 0, NP_ROWS).reshape(BATCH, SEQ, D_MODEL)
    y_sample = rmsnorm(x, norm_final, F32, NP_ROWS, NS_ROWS).reshape(DEC_BATCH, DEC_SEQ, D_MODEL)
    return (y_prompt, y_sample,
            jnp.stack(p_ckv), jnp.stack(p_kpe), jnp.stack(p_re), jnp.stack(p_im),
            jnp.stack(s_ckv), jnp.stack(s_kpe), jnp.stack(s_re), jnp.stack(s_im))
```

```python
import functools
import math

import jax
import jax.numpy as jnp
from jax import lax
from jax.experimental import pallas as pl
from jax.experimental.pallas import tpu as pltpu

F32 = jnp.float32
BF16 = jnp.bfloat16

D_MODEL = 1024
BATCH = 8
SEQ = 2048
DEPTH = 4
DEC_BATCH = 128
DEC_SEQ = 8
PAST_LEN = 8192
PAGE_SIZE = 128
N_PAGES = PAST_LEN // PAGE_SIZE
N_HEADS = 16
D_NOPE = 64
D_ROPE = 32
D_V = 64
Q_LORA = 512
KV_LORA = 256
ROPE_THETA = 10000.0
GROUP_SIZE = 16
N_GROUPS = D_MODEL // GROUP_SIZE
STATE_DIM = 64
D_FF = 4 * D_MODEL
EPS = 1e-6
SCALE = (D_NOPE + D_ROPE) ** -0.5
EXP2_SCALE = SCALE * math.log2(math.e)

NP_ROWS = BATCH * SEQ
NS_ROWS = DEC_BATCH * DEC_SEQ
ROWS = NP_ROWS + NS_ROWS

LANES = 128
HEAD_PAD = 128
QEXT = KV_LORA + LANES
VMEM_LIMIT = 52 * 1024 * 1024

TM = 512
TM_PROJ = 256
TQ = 256
PAGES_PER_STEP = 32


def _cparams(sem):
    return pltpu.CompilerParams(dimension_semantics=sem, vmem_limit_bytes=VMEM_LIMIT)


def _rms(x, g):
    return x * lax.rsqrt(jnp.mean(x * x, axis=-1, keepdims=True) + EPS) * g


def _dot(a, b):
    return jnp.dot(a, b, preferred_element_type=F32)


def _dot_nt(a, b):
    return lax.dot_general(a, b, (((1,), (1,)), ((), ())), preferred_element_type=F32)


def _const_spec(shape):
    nd = len(shape)
    return pl.BlockSpec(shape, lambda *_: (0,) * nd)


FF_CHUNK = 1024


def _glu_update(x, h, y, d, w_ref):
    gl = jax.nn.gelu(y + d * h).astype(BF16)
    ab = _dot(gl, w_ref[...])
    return x + ab[:, :D_MODEL] * jax.nn.sigmoid(ab[:, D_MODEL:])


def _mlp_body(pre, post, x_ref, *refs):
    x = x_ref[...]
    if pre == "oproj":
        a_ref, wa_ref, *refs = refs
        x = x + _dot(a_ref[...], wa_ref[...])
    elif pre == "glu":
        h_ref, y_ref, d_ref, wg_ref, *refs = refs
        x = _glu_update(x, h_ref[...].reshape(x.shape), y_ref[...].reshape(x.shape), d_ref[...], wg_ref)
    g_ref, w1_ref, w2_ref, *refs = refs
    h = _rms(x, g_ref[...]).astype(BF16)
    acc = x
    for c in range(D_FF // FF_CHUNK):
        sl = slice(c * FF_CHUNK, (c + 1) * FF_CHUNK)
        z = jnp.maximum(_dot(h, w1_ref[:, sl]), 0.0)
        acc = acc + _dot((z * z).astype(BF16), w2_ref[sl, :])
    if post is None:
        (o_ref,) = refs
        o_ref[...] = acc
    elif post == "norm":
        gn_ref, o_ref, hn_ref = refs
        o_ref[...] = acc
        hn_ref[...] = _rms(acc, gn_ref[...]).reshape(hn_ref.shape)
    else:
        gn_ref, y_ref = refs
        y_ref[...] = _rms(acc, gn_ref[...])


def _layer_spec(shape, layer):
    nd = len(shape)
    return pl.BlockSpec((None,) + shape, lambda *_: (layer,) + (0,) * nd, pipeline_mode=pl.Buffered(1))


def mlp(x, layer, g, w1, w2, sample, pre=None, pre_args=(), post=None, g_post=None):
    if sample:
        blk0, grid, rows = NP_ROWS // TM, (NS_ROWS // TM,), NS_ROWS
        x_spec = pl.BlockSpec((TM, D_MODEL), lambda i: (i + blk0, 0))
        own_spec = pl.BlockSpec((TM, D_MODEL), lambda i: (i, 0))
        norm_spec, norm_shape = own_spec, (NS_ROWS, D_MODEL)
    else:
        nci = S5_CHUNKS // S5_CT
        grid, rows = (BATCH, nci), NP_ROWS
        x_spec = pl.BlockSpec((TM, D_MODEL), lambda b, c: (b * nci + c, 0))
        own_spec = x_spec
        norm_spec = pl.BlockSpec((S5_CT, None, S5_Q, D_MODEL), lambda b, c: (c, b, 0, 0))
        norm_shape = (S5_CHUNKS, BATCH, S5_Q, D_MODEL)
    x_out = jax.ShapeDtypeStruct((ROWS, D_MODEL), F32)
    post_args, post_specs = (), []
    if post is None:
        out_specs, out_shape, aliases = x_spec, x_out, {0: 0}
    else:
        post_args, post_specs = (g_post.reshape(1, D_MODEL),), [_const_spec((1, D_MODEL))]
        if post == "norm":
            out_specs, aliases = [x_spec, norm_spec], {0: 0}
            out_shape = [x_out, jax.ShapeDtypeStruct(norm_shape, F32)]
        else:
            out_specs, out_shape, aliases = own_spec, jax.ShapeDtypeStruct((rows, D_MODEL), F32), {}
    pre_specs = []
    if pre == "oproj":
        o, wo = pre_args
        pre_specs = [pl.BlockSpec((TM, N_HEADS * D_V), lambda b, c: (b * nci + c, 0)),
                     pl.BlockSpec(wo.shape, lambda *_: (0, 0), pipeline_mode=pl.Buffered(1))]
    elif pre == "glu":
        h, y, d, wg, glu_layer = pre_args
        hy_spec = pl.BlockSpec((S5_CT, None, S5_Q, D_MODEL), lambda b, c: (c, b, 0, 0))
        pre_args = (h.reshape(S5_CHUNKS, BATCH, S5_Q, D_MODEL), y.reshape(S5_CHUNKS, BATCH, S5_Q, D_MODEL),
                    d.reshape(1, D_MODEL), wg)
        pre_specs = [hy_spec, hy_spec, _const_spec((1, D_MODEL)),
                     _layer_spec((D_MODEL, 2 * D_MODEL), glu_layer)]
    return pl.pallas_call(
        functools.partial(_mlp_body, pre, post),
        grid=grid,
        in_specs=[x_spec] + pre_specs + [_const_spec((1, D_MODEL)),
                                         _layer_spec((D_MODEL, D_FF), layer),
                                         _layer_spec((D_FF, D_MODEL), layer)] + post_specs,
        out_specs=out_specs,
        out_shape=out_shape,
        input_output_aliases=aliases,
        compiler_params=_cparams(("parallel",) * len(grid)),
        name="mlp_sample" if sample else ("mlp_" + (pre or "plain")),
    )(x, *pre_args, g.reshape(1, D_MODEL), w1, w2, *post_args)


def _proj_body(sample, x_ref, gm_ref, win_ref, gq_ref, gkv_ref, wq_ref, wqs_ref,
               cq_ref, sq_ref, ck_ref, sk_ref, wa_ref, wb_ref, *outs):
    h = _rms(x_ref[...], gm_ref[...]).astype(BF16)
    proj = _dot(h, win_ref[...])
    cq = _rms(proj[:, :Q_LORA], gq_ref[...]).astype(BF16)
    ckv = _rms(proj[:, Q_LORA:Q_LORA + KV_LORA], gkv_ref[...])
    kpe = (proj[:, 768:896] * ck_ref[...] + proj[:, 896:1024] * sk_ref[...])
    ckv_ref, kpe_ref = outs[0], outs[1]
    ckv_ref[...] = ckv
    kpe_ref[...] = kpe[:, :D_ROPE]
    cq_tab = cq_ref[...]
    sq_tab = sq_ref[...]
    for p in range(N_HEADS // 2):
        sl = slice(p * 2 * HEAD_PAD, (p + 1) * 2 * HEAD_PAD)
        qp = _dot(cq, wq_ref[:, sl]) * cq_tab + _dot(cq, wqs_ref[:, sl]) * sq_tab
        qp = qp.astype(BF16)
        if sample:
            qext_ref = outs[2]
            for hh in range(2):
                head = 2 * p + hh
                qh = qp[:, hh * HEAD_PAD:(hh + 1) * HEAD_PAD]
                qext_ref[head] = _dot(qh, wa_ref[head])
        else:
            outs[2][:, sl] = qp
    if not sample:
        k_ref, vt_ref = outs[3], outs[4]
        ckv_b = ckv.astype(BF16)
        kpl = pltpu.roll(kpe, D_NOPE, axis=1)
        kpl2 = jnp.concatenate([kpl, kpl], axis=1)
        for p in range(N_HEADS // 2):
            sl = slice(p * 2 * HEAD_PAD, (p + 1) * 2 * HEAD_PAD)
            k_ref[:, sl] = ((_dot(ckv_b, wa_ref[:, sl]) + kpl2) * EXP2_SCALE).astype(BF16)
        vt_ref[...] = _dot_nt(wb_ref[...], ckv_b).astype(BF16)


def mla_proj(x, sample, gm, w, tabs):
    tm = TM_PROJ
    if sample:
        rows, row0, tab_map = NS_ROWS, NP_ROWS // tm, (lambda i: (0, 0))
    else:
        rows, row0, tab_map = NP_ROWS, 0, (lambda i: (i % (SEQ // tm), 0))
    cq, sq, ck, sk = tabs
    wa = w["wuk_ext"] if sample else w["wk"]
    wb = w["wv"]
    out_shape = [jax.ShapeDtypeStruct((rows, KV_LORA), F32),
                 jax.ShapeDtypeStruct((rows, D_ROPE), F32)]
    out_specs = [pl.BlockSpec((tm, KV_LORA), lambda i: (i, 0)),
                 pl.BlockSpec((tm, D_ROPE), lambda i: (i, 0))]
    if sample:
        out_shape.append(jax.ShapeDtypeStruct((N_HEADS, rows, QEXT), F32))
        out_specs.append(pl.BlockSpec((N_HEADS, tm, QEXT), lambda i: (0, i, 0)))
    else:
        out_shape += [jax.ShapeDtypeStruct((rows, N_HEADS * HEAD_PAD), BF16),
                      jax.ShapeDtypeStruct((rows, N_HEADS * HEAD_PAD), BF16),
                      jax.ShapeDtypeStruct((N_HEADS * D_V, rows), BF16)]
        out_specs += [pl.BlockSpec((tm, N_HEADS * HEAD_PAD), lambda i: (i, 0)),
                      pl.BlockSpec((tm, N_HEADS * HEAD_PAD), lambda i: (i, 0)),
                      pl.BlockSpec((N_HEADS * D_V, tm), lambda i: (0, i))]
    return pl.pallas_call(
        functools.partial(_proj_body, sample),
        grid=(rows // tm,),
        in_specs=[
            pl.BlockSpec((tm, D_MODEL), lambda i: (i + row0, 0)),
            _const_spec((1, D_MODEL)),
            _const_spec(w["win"].shape),
            _const_spec((1, Q_LORA)),
            _const_spec((1, KV_LORA)),
            _const_spec(w["wq"].shape),
            _const_spec(w["wqs"].shape),
            pl.BlockSpec((tm, 2 * HEAD_PAD), tab_map),
            pl.BlockSpec((tm, 2 * HEAD_PAD), tab_map),
            pl.BlockSpec((tm, LANES), tab_map),
            pl.BlockSpec((tm, LANES), tab_map),
            _const_spec(wa.shape),
            _const_spec(wb.shape),
        ],
        out_specs=out_specs,
        out_shape=out_shape,
        compiler_params=_cparams(("parallel",)),
        name="mla_proj_sample" if sample else "mla_proj_prompt",
    )(x, gm.reshape(1, D_MODEL), w["win"], w["gq"], w["gkv"], w["wq"], w["wqs"],
      cq, sq, ck, sk, wa, wb)


def _flash_body(q_ref, k_ref, vt_ref, o_ref, s_buf):
    key = lax.broadcasted_iota(jnp.int32, (TQ, TQ), 0)
    qry = lax.broadcasted_iota(jnp.int32, (TQ, TQ), 1)
    items = [(qi, hh) for qi in range(SEQ // TQ) for hh in range(2)]
    ones = jnp.ones((16, SEQ), BF16)

    def scores(n):
        qi, hh = items[n]
        q0 = qi * TQ
        hs = slice(hh * HEAD_PAD, (hh + 1) * HEAD_PAD)
        q = q_ref[q0:q0 + TQ, hs]
        sd = _dot_nt(k_ref[q0:q0 + TQ, hs], q)
        sd = jnp.where(key <= qry, sd, -jnp.inf)
        s_buf[n % 2, q0:q0 + TQ, :] = sd
        m = jnp.max(sd, axis=0, keepdims=True)
        if qi > 0:
            sm = _dot_nt(k_ref[0:q0, hs], q)
            s_buf[n % 2, 0:q0, :] = sm
            m = jnp.maximum(m, jnp.max(sm, axis=0, keepdims=True))
        return m

    def values(n, m):
        qi, hh = items[n]
        kv_len = (qi + 1) * TQ
        p = jnp.exp2(s_buf[n % 2, 0:kv_len, :] - m).astype(BF16)
        vt1 = jnp.concatenate([vt_ref[hh * D_V:(hh + 1) * D_V, 0:kv_len], ones[:, 0:kv_len]], axis=0)
        ot = _dot(vt1, p)
        return ot[:D_V] / ot[D_V:D_V + 1]

    m_next = scores(0)
    ots = []
    for n, (qi, hh) in enumerate(items):
        m = m_next
        if n + 1 < len(items):
            m_next = scores(n + 1)
        ots.append(values(n, m))
        if hh == 1:
            q0 = qi * TQ
            o_ref[q0:q0 + TQ, :] = jnp.concatenate(ots, axis=0).T.astype(o_ref.dtype)
            ots = []


def flash_prompt(q, k, vt):
    return pl.pallas_call(
        _flash_body,
        grid=(BATCH, N_HEADS // 2),
        in_specs=[
            pl.BlockSpec((SEQ, 2 * HEAD_PAD), lambda b, p: (b, p)),
            pl.BlockSpec((SEQ, 2 * HEAD_PAD), lambda b, p: (b, p)),
            pl.BlockSpec((2 * D_V, SEQ), lambda b, p: (p, b)),
        ],
        out_specs=pl.BlockSpec((SEQ, 2 * D_V), lambda b, p: (b, p)),
        out_shape=jax.ShapeDtypeStruct((NP_ROWS, N_HEADS * D_V), BF16),
        scratch_shapes=[pltpu.VMEM((2, SEQ, TQ), F32)],
        compiler_params=_cparams(("parallel", "parallel")),
        name="flash_prompt",
    )(q, k, vt)


N_CHUNKS = N_PAGES // PAGES_PER_STEP
assert N_CHUNKS % 2 == 0 and N_CHUNKS * PAGES_PER_STEP == N_PAGES
SATTN_KEYS = 2048
QROWS = N_HEADS * DEC_SEQ


def _sattn_body(layer, pt_ref, q_ref, cnew_ref, knew_ref, ckv_hbm, kpet_hbm, o_ref,
                cbuf, kbuf, sem, s_buf, ck_b):
    n = PAGES_PER_STEP
    b = pl.program_id(0)

    def chunk_copies(req, c, slot):
        cps = []
        for k in range(n):
            page = pt_ref[req, c * n + k]
            rows = pl.ds(k * PAGE_SIZE, PAGE_SIZE)
            cps.append(pltpu.make_async_copy(ckv_hbm.at[layer, page], cbuf.at[slot, rows, :],
                                             sem.at[slot, 0]))
            cps.append(pltpu.make_async_copy(kpet_hbm.at[layer, page], kbuf.at[slot, :, rows],
                                             sem.at[slot, 1]))
        return cps

    def start(req, c, slot):
        for cp in chunk_copies(req, c, slot):
            cp.start()

    def wait(req, c, slot):
        for cp in chunk_copies(req, c, slot):
            cp.wait()

    @pl.when(b == 0)
    def _():
        for c in range(2):
            start(0, c, c)

    q = q_ref[...].reshape(QROWS, QEXT)
    qlat = q[:, :KV_LORA].astype(BF16)
    qpe = q[:, KV_LORA:KV_LORA + D_ROPE].astype(BF16)

    n_sb = n * PAGE_SIZE // SATTN_KEYS
    items = [(c, sb) for c in range(N_CHUNKS) for sb in range(n_sb)]

    def scores(i):
        c, sb = items[i]
        keys = slice(sb * SATTN_KEYS, (sb + 1) * SATTN_KEYS)
        if sb == 0:
            wait(b, c, c % 2)
        ck = cbuf[c % 2, keys, :].astype(BF16)
        kt = kbuf[c % 2, :, keys].astype(BF16)
        ck_b[i % 2] = ck
        s = _dot_nt(qlat, ck) + _dot(qpe, kt)
        s_buf[i % 2] = s
        if sb == n_sb - 1:
            if c + 2 < N_CHUNKS:
                start(b, c + 2, c % 2)
            else:
                @pl.when(b + 1 < DEC_BATCH)
                def _():
                    start(b + 1, c + 2 - N_CHUNKS, c % 2)
        return jnp.max(s, axis=-1, keepdims=True)

    bm_next = scores(0)

    pad = PAGE_SIZE - DEC_SEQ
    cn = jnp.concatenate([cnew_ref[...], jnp.zeros((pad, KV_LORA), F32)], axis=0).astype(BF16)
    kn = jnp.concatenate([knew_ref[...], jnp.zeros((pad, D_ROPE), F32)], axis=0).astype(BF16)
    s = _dot_nt(qlat, cn) + _dot_nt(qpe, kn)
    tok = lax.broadcasted_iota(jnp.int32, (QROWS, PAGE_SIZE), 0) % DEC_SEQ
    key = lax.broadcasted_iota(jnp.int32, (QROWS, PAGE_SIZE), 1)
    s = jnp.where(key <= tok, s, -jnp.inf)
    m = jnp.max(s, axis=-1, keepdims=True)
    p = jnp.exp2((s - m) * EXP2_SCALE)
    l = jnp.sum(p, axis=-1, keepdims=True)
    acc = _dot(p.astype(BF16), cn)

    for i in range(len(items)):
        bm = bm_next
        if i + 1 < len(items):
            bm_next = scores(i + 1)
        m_new = jnp.maximum(m, bm)
        alpha = jnp.exp2((m - m_new) * EXP2_SCALE)
        p = jnp.exp2((s_buf[i % 2] - m_new) * EXP2_SCALE)
        l = alpha * l + jnp.sum(p, axis=-1, keepdims=True)
        acc = alpha * acc + _dot(p.astype(BF16), ck_b[i % 2])
        m = m_new

    o_ref[...] = (acc / l).reshape(N_HEADS, DEC_SEQ, KV_LORA)


def attn_sample(layer, page_table, qext, ckv_all, kpe_all, cache_ckv, cache_kpet):
    n = PAGES_PER_STEP
    grid_spec = pltpu.PrefetchScalarGridSpec(
        num_scalar_prefetch=1,
        grid=(DEC_BATCH,),
        in_specs=[
            pl.BlockSpec((N_HEADS, DEC_SEQ, QEXT), lambda b, pt: (0, b, 0)),
            pl.BlockSpec((DEC_SEQ, KV_LORA), lambda b, pt: (b, 0)),
            pl.BlockSpec((DEC_SEQ, D_ROPE), lambda b, pt: (b, 0)),
            pl.BlockSpec(memory_space=pl.ANY),
            pl.BlockSpec(memory_space=pl.ANY),
        ],
        out_specs=pl.BlockSpec((N_HEADS, DEC_SEQ, KV_LORA), lambda b, pt: (0, b, 0)),
        scratch_shapes=[
            pltpu.VMEM((2, n * PAGE_SIZE, KV_LORA), F32),
            pltpu.VMEM((2, D_ROPE, n * PAGE_SIZE), F32),
            pltpu.SemaphoreType.DMA((2, 2)),
            pltpu.VMEM((2, QROWS, SATTN_KEYS), F32),
            pltpu.VMEM((2, SATTN_KEYS, KV_LORA), BF16),
        ],
    )
    return pl.pallas_call(
        functools.partial(_sattn_body, layer),
        grid_spec=grid_spec,
        out_shape=jax.ShapeDtypeStruct((N_HEADS, NS_ROWS, KV_LORA), F32),
        compiler_params=_cparams(("arbitrary",)),
        name="attn_sample",
    )(page_table, qext, ckv_all, kpe_all, cache_ckv, cache_kpet)


def _oproj_sample_body(x_ref, ol_ref, wuv_ref, wo_ref, out_ref):
    acc = x_ref[...]
    for p in range(N_HEADS // 2):
        pair = jnp.concatenate([ol_ref[2 * p], ol_ref[2 * p + 1]], axis=1).astype(BF16)
        o_pair = _dot(pair, wuv_ref[p]).astype(BF16)
        acc = acc + _dot(o_pair, wo_ref[p * 2 * D_V:(p + 1) * 2 * D_V, :])
    out_ref[...] = acc


def oproj_sample(x, o_lat, wuv_pair, wo):
    row0 = NP_ROWS // TM
    return pl.pallas_call(
        _oproj_sample_body,
        grid=(NS_ROWS // TM,),
        in_specs=[pl.BlockSpec((TM, D_MODEL), lambda i: (i + row0, 0)),
                  pl.BlockSpec((N_HEADS, TM, KV_LORA), lambda i: (0, i, 0)),
                  _const_spec(wuv_pair.shape),
                  _const_spec(wo.shape)],
        out_specs=pl.BlockSpec((TM, D_MODEL), lambda i: (i + row0, 0)),
        out_shape=jax.ShapeDtypeStruct((ROWS, D_MODEL), F32),
        input_output_aliases={0: 0},
        compiler_params=_cparams(("parallel",)),
        name="oproj_sample",
    )(x, o_lat, wuv_pair, wo)


S5_Q = DEC_SEQ
S5_BLK = LANES // GROUP_SIZE
S5_NBLK = N_GROUPS // S5_BLK
S5_HALF = S5_BLK * STATE_DIM
S5_CT = 64
S5_CHUNKS = SEQ // S5_Q


def _s5_body(n_chunks, nb, h_ref, t_ref, m_ref, n_ref, a_ref, h0_ref, y_ref, hf_ref,
             s_s, p_s, st_s):
    rows = n_chunks * nb

    @pl.when(pl.program_id(1) == 0)
    def _():
        st_s[...] = h0_ref[...]

    u = jnp.concatenate([h_ref[pl.ds(k, rows, stride=S5_Q), :].astype(BF16)
                         for k in range(S5_Q)], axis=1)
    s_s[...] = _dot(u, m_ref[...])
    are = a_ref[:, :S5_HALF]
    aim = a_ref[:, S5_HALF:]

    def step(c, h):
        r0 = pl.multiple_of(c * nb, nb)
        p_s[pl.ds(r0, nb), :] = h
        s = s_s[pl.ds(r0, nb), :]
        hre, him = h[:, :S5_HALF], h[:, S5_HALF:]
        nre = are * hre - aim * him + s[:, :S5_HALF]
        nim = are * him + aim * hre + s[:, S5_HALF:]
        return jnp.concatenate([nre, nim], axis=1)

    h = lax.fori_loop(0, n_chunks, step, st_s[...])
    st_s[...] = h
    hf_ref[...] = h
    y = _dot(u, t_ref[...]) + _dot(p_s[...].astype(BF16), n_ref[...])
    for l in range(S5_Q):
        y_ref[pl.ds(l, rows, stride=S5_Q), :] = y[:, l * LANES:(l + 1) * LANES]


def s5_scan(h, tabs, h0, n_chunks, nb):
    tbig, mbig, nbig, abig = tabs
    rows = n_chunks * nb
    tile = rows * S5_Q
    width = S5_Q * LANES

    def bspec(shape):
        return pl.BlockSpec((None,) + shape, lambda j, c: (j, 0, 0))

    return pl.pallas_call(
        functools.partial(_s5_body, n_chunks, nb),
        grid=(S5_NBLK, h.shape[0] // tile),
        in_specs=[pl.BlockSpec((tile, LANES), lambda j, c: (c, j)),
                  bspec((width, width)), bspec((width, 2 * S5_HALF)), bspec((2 * S5_HALF, width)),
                  bspec((1, 2 * S5_HALF)), bspec((nb, 2 * S5_HALF))],
        out_specs=[pl.BlockSpec((tile, LANES), lambda j, c: (c, j)), bspec((nb, 2 * S5_HALF))],
        out_shape=[jax.ShapeDtypeStruct(h.shape, F32),
                   jax.ShapeDtypeStruct((S5_NBLK, nb, 2 * S5_HALF), F32)],
        scratch_shapes=[pltpu.VMEM((rows, 2 * S5_HALF), F32), pltpu.VMEM((rows, 2 * S5_HALF), F32),
                        pltpu.VMEM((nb, 2 * S5_HALF), F32)],
        compiler_params=_cparams(("parallel", "arbitrary")),
        name="s5_scan",
    )(h, tbig, mbig, nbig, abig, h0)


def _glu_body(x_ref, h_ref, y_ref, d_ref, w_ref, o_ref):
    o_ref[...] = _glu_update(x_ref[...], h_ref[...], y_ref[...], d_ref[...], w_ref)


def glu_sample(x, h, y, d, w, layer):
    blk0 = NP_ROWS // TM
    x_spec = pl.BlockSpec((TM, D_MODEL), lambda i: (i + blk0, 0))
    hy_spec = pl.BlockSpec((TM, D_MODEL), lambda i: (i, 0))
    return pl.pallas_call(
        _glu_body,
        grid=(NS_ROWS // TM,),
        in_specs=[x_spec, hy_spec, hy_spec, _const_spec((1, D_MODEL)),
                  _layer_spec((D_MODEL, 2 * D_MODEL), layer)],
        out_specs=x_spec,
        out_shape=jax.ShapeDtypeStruct((ROWS, D_MODEL), F32),
        input_output_aliases={0: 0},
        compiler_params=_cparams(("parallel",)),
        name="glu_sample",
    )(x, h, y, d.reshape(1, D_MODEL), w)


def _prep_mla(w_in, g_q, g_kv, w_q_up, w_kv_up, w_o):
    z = lambda *s: jnp.zeros(s, F32)
    wk_pe = w_in[:, Q_LORA + KV_LORA:]
    half = D_ROPE // 2
    win = jnp.concatenate(
        [w_in[:, :Q_LORA + KV_LORA], wk_pe, z(D_MODEL, LANES - D_ROPE),
         -wk_pe[:, half:], wk_pe[:, :half], z(D_MODEL, LANES - D_ROPE)], axis=1)
    wq3 = w_q_up.reshape(Q_LORA, N_HEADS, D_NOPE + D_ROPE)
    nope, x1, x2 = wq3[..., :D_NOPE], wq3[..., D_NOPE:D_NOPE + half], wq3[..., D_NOPE + half:]
    zpad = z(Q_LORA, N_HEADS, HEAD_PAD - D_NOPE - D_ROPE)
    wq = jnp.concatenate([nope, x1, x2, zpad], axis=-1).reshape(Q_LORA, N_HEADS * HEAD_PAD)
    wqs = jnp.concatenate([z(Q_LORA, N_HEADS, D_NOPE), -x2, x1, zpad], axis=-1)
    wqs = wqs.reshape(Q_LORA, N_HEADS * HEAD_PAD)
    wkv3 = w_kv_up.reshape(KV_LORA, N_HEADS, D_NOPE + D_V)
    w_uk, w_uv = wkv3[..., :D_NOPE], wkv3[..., D_NOPE:]
    wk = jnp.concatenate([w_uk, z(KV_LORA, N_HEADS, HEAD_PAD - D_NOPE)], axis=-1)
    wk = wk.reshape(KV_LORA, N_HEADS * HEAD_PAD)
    wv = w_uv.reshape(KV_LORA, N_HEADS * D_V).T
    wuk_ext = jnp.zeros((N_HEADS, HEAD_PAD, QEXT), F32)
    wuk_ext = wuk_ext.at[:, :D_NOPE, :KV_LORA].set(jnp.transpose(w_uk, (1, 2, 0)))
    wuk_ext = wuk_ext.at[:, D_NOPE:D_NOPE + D_ROPE, KV_LORA:KV_LORA + D_ROPE].set(
        jnp.broadcast_to(jnp.eye(D_ROPE, dtype=F32), (N_HEADS, D_ROPE, D_ROPE)))
    uv = jnp.transpose(w_uv, (1, 0, 2)).reshape(N_HEADS // 2, 2, KV_LORA, D_V)
    wuv_pair = jnp.zeros((N_HEADS // 2, 2 * KV_LORA, 2 * D_V), F32)
    wuv_pair = wuv_pair.at[:, :KV_LORA, :D_V].set(uv[:, 0])
    wuv_pair = wuv_pair.at[:, KV_LORA:, D_V:].set(uv[:, 1])
    return dict(win=win.astype(BF16), gq=g_q.reshape(1, Q_LORA), gkv=g_kv.reshape(1, KV_LORA),
                wq=wq.astype(BF16), wqs=wqs.astype(BF16), wk=wk.astype(BF16), wv=wv.astype(BF16),
                wuk_ext=wuk_ext.astype(BF16), wuv_pair=wuv_pair.astype(BF16), wo=w_o.astype(BF16))


def _rope_tabs(pos, reps):
    half = D_ROPE // 2
    inv = ROPE_THETA ** (-jnp.arange(half, dtype=F32) / half)
    ang = pos.astype(F32)[:, None] * inv[None, :]
    cos, sin = jnp.cos(ang), jnp.sin(ang)
    n = pos.shape[0]
    one, zero = jnp.ones((n, D_NOPE), F32), jnp.zeros((n, D_NOPE), F32)
    zpad = jnp.zeros((n, HEAD_PAD - D_NOPE - D_ROPE), F32)
    cq = jnp.concatenate([one, cos, cos, zpad], axis=1)
    sq = jnp.concatenate([zero, sin, sin, zpad], axis=1)
    kpad = jnp.zeros((n, LANES - D_ROPE), F32)
    ck = jnp.concatenate([cos, cos, kpad], axis=1)
    sk = jnp.concatenate([sin, sin, kpad], axis=1)
    cq, sq = jnp.tile(cq, (reps, 2)), jnp.tile(sq, (reps, 2))
    return cq, sq, jnp.tile(ck, (reps, 1)), jnp.tile(sk, (reps, 1))


def _prep_s5(a_re, a_im, log_dt, b_re, b_im, c_re, c_im, q):
    hi = lax.Precision.HIGHEST
    dt = jnp.exp(log_dt)[:, None]
    dre, dim = dt * a_re, dt * a_im
    mag = jnp.exp(dre)
    abr, abi = mag * jnp.cos(dim), mag * jnp.sin(dim)
    den = a_re * a_re + a_im * a_im
    xr, xi = abr - 1.0, abi
    fr, fi = (xr * a_re + xi * a_im) / den, (xi * a_re - xr * a_im) / den
    bbr = fr[..., None] * b_re - fi[..., None] * b_im
    bbi = fr[..., None] * b_im + fi[..., None] * b_re
    j = jnp.arange(q + 1, dtype=F32)[:, None, None]
    pmag = jnp.exp(j * dre)
    pwr, pwi = pmag * jnp.cos(j * dim), pmag * jnp.sin(j * dim)
    pbr = pwr[..., None] * bbr - pwi[..., None] * bbi
    pbi = pwr[..., None] * bbi + pwi[..., None] * bbr
    kk = (jnp.einsum("gtp,jgps->jgst", c_re, pbr[:q], precision=hi)
          - jnp.einsum("gtp,jgps->jgst", c_im, pbi[:q], precision=hi))
    g = a_re.shape[0]
    taps = jnp.transpose(kk, (1, 2, 0, 3)).reshape(g, GROUP_SIZE, q * GROUP_SIZE)
    t = jnp.stack([jnp.pad(taps[..., :(q - k) * GROUP_SIZE], ((0, 0), (0, 0), (k * GROUP_SIZE, 0)))
                   for k in range(q)], axis=1).reshape(g, q * GROUP_SIZE, q * GROUP_SIZE)
    mre = jnp.transpose(pbr[:q][::-1], (1, 0, 3, 2)).reshape(g, q * GROUP_SIZE, STATE_DIM)
    mim = jnp.transpose(pbi[:q][::-1], (1, 0, 3, 2)).reshape(g, q * GROUP_SIZE, STATE_DIM)
    ncr = pwr[1:, :, None, :] * c_re[None] - pwi[1:, :, None, :] * c_im[None]
    nci = pwr[1:, :, None, :] * c_im[None] + pwi[1:, :, None, :] * c_re[None]
    nre = jnp.transpose(ncr, (1, 3, 0, 2)).reshape(g, STATE_DIM, q * GROUP_SIZE)
    nim = jnp.transpose(nci, (1, 3, 0, 2)).reshape(g, STATE_DIM, q * GROUP_SIZE)
    are, aim = pwr[q][:, None, :], pwi[q][:, None, :]
    return t, mre, mim, nre, nim, are, aim


def _s5_tables_body(at_ref, am_ref, an_ref, et_ref, em_ref, t_ref, m_ref, n_ref):
    width = S5_Q * LANES
    row = lax.broadcasted_iota(jnp.int32, (width, width), 0)
    col = lax.broadcasted_iota(jnp.int32, (width, width), 1)

    def expand(a_ref, e_ref, row_shift, col_shift):
        full = _dot(a_ref[...].astype(BF16), e_ref[...])
        keep = ((row >> row_shift) & (S5_BLK - 1)) == ((col >> col_shift) & (S5_BLK - 1))
        return jnp.where(keep, full, 0.0).astype(BF16)

    gs, sd = GROUP_SIZE.bit_length() - 1, STATE_DIM.bit_length() - 1
    t_ref[...] = expand(at_ref, et_ref, gs, gs)
    m_ref[...] = expand(am_ref, em_ref, gs, sd)
    n_ref[...] = expand(an_ref, et_ref, sd, gs)


def _s5_block_tables(p):
    t, mre, mim, nre, nim, are, aim = _prep_s5(*p, S5_Q)
    q = S5_Q
    width = q * LANES
    assert 2 * S5_HALF == width and q * GROUP_SIZE == LANES and 2 * STATE_DIM == LANES

    def rows_kgs(a):
        a5 = a.reshape(S5_NBLK, S5_BLK, q, GROUP_SIZE, LANES)
        return jnp.transpose(a5, (0, 2, 1, 3, 4)).reshape(S5_NBLK, width, LANES)

    at = rows_kgs(t)
    am = rows_kgs(jnp.concatenate([mre, mim], axis=-1))
    an = jnp.stack([nre.reshape(S5_NBLK, S5_BLK, STATE_DIM, LANES),
                    -nim.reshape(S5_NBLK, S5_BLK, STATE_DIM, LANES)], axis=1)
    an = an.reshape(S5_NBLK, width, LANES)
    r = jnp.arange(LANES)[:, None]
    c = jnp.arange(width)[None, :]
    et = ((r // GROUP_SIZE == c // LANES) & (r % GROUP_SIZE == c % GROUP_SIZE)).astype(BF16)
    em = ((r // STATE_DIM == c // S5_HALF) & (r % STATE_DIM == c % STATE_DIM)).astype(BF16)
    cspec = pl.BlockSpec((None, width, LANES), lambda j: (j, 0, 0))
    ospec = pl.BlockSpec((None, width, width), lambda j: (j, 0, 0))
    big = jax.ShapeDtypeStruct((S5_NBLK, width, width), BF16)
    tbig, mbig, nbig = pl.pallas_call(
        _s5_tables_body,
        grid=(S5_NBLK,),
        in_specs=[cspec, cspec, cspec, _const_spec((LANES, width)), _const_spec((LANES, width))],
        out_specs=[ospec, ospec, ospec],
        out_shape=[big, big, big],
        compiler_params=_cparams(("parallel",)),
        name="s5_tables",
    )(at, am, an, et, em)
    abig = jnp.concatenate([are.reshape(S5_NBLK, 1, S5_HALF), aim.reshape(S5_NBLK, 1, S5_HALF)], axis=2)
    return tbig, mbig, nbig, abig


def _state_to_blocks(re, im):
    nb = re.shape[0]
    f = lambda a: jnp.transpose(a.reshape(nb, S5_NBLK, S5_HALF), (1, 0, 2))
    return jnp.concatenate([f(re), f(im)], axis=2)


def _blocks_to_state(hf):
    nb = hf.shape[1]
    f = lambda a: jnp.transpose(a, (1, 0, 2)).reshape(nb, N_GROUPS, STATE_DIM)
    return f(hf[..., :S5_HALF]), f(hf[..., S5_HALF:])


def _s5_mixer(hp, hs, p, h0re, h0im):
    tabs = _s5_block_tables(p)
    zero = jnp.zeros((S5_NBLK, BATCH, 2 * S5_HALF), F32)
    yp, hfp = s5_scan(hp, tabs, zero, S5_CT, BATCH)
    ys, hfs = s5_scan(hs, tabs, _state_to_blocks(h0re, h0im), 1, DEC_BATCH)
    return (yp, ys) + _blocks_to_state(hfp) + _blocks_to_state(hfs)


def kernel(x_prompt, x_sample, cache_ckv, cache_kpe, state_ssm_re, state_ssm_im, page_table, norm_mix, norm_mlp, norm_final, mla_w_in, mla_q_norm, mla_kv_norm, mla_w_q_up, mla_w_kv_up, mla_w_o, ssm_a_re, ssm_a_im, ssm_log_dt, ssm_b_re, ssm_b_im, ssm_c_re, ssm_c_im, ssm_d, ssm_w_glu, mlp_w1, mlp_w2):
    x = jnp.concatenate([x_prompt.reshape(NP_ROWS, D_MODEL), x_sample.reshape(NS_ROWS, D_MODEL)], axis=0)
    tabs_p = _rope_tabs(jnp.arange(SEQ, dtype=jnp.int32), 1)
    tabs_s = _rope_tabs(PAST_LEN + jnp.arange(DEC_SEQ, dtype=jnp.int32), TM_PROJ // DEC_SEQ)
    cache_kpet = jnp.swapaxes(cache_kpe, 2, 3)
    w1b, w2b, wglu_b = mlp_w1.astype(BF16), mlp_w2.astype(BF16), ssm_w_glu.astype(BF16)
    p_ckv, p_kpe, p_re, p_im = [], [], [], []
    s_ckv, s_kpe, s_re, s_im = [], [], [], []
    for i in range(DEPTH):
        j = i // 2
        if i % 2 == 0:
            w = _prep_mla(mla_w_in[j], mla_q_norm[j], mla_kv_norm[j], mla_w_q_up[j],
                          mla_w_kv_up[j], mla_w_o[j])
            ckv_p, kpe_p, q_p, k_p, v_p = mla_proj(x, False, norm_mix[i], w, tabs_p)
            ckv_s, kpe_s, qext = mla_proj(x, True, norm_mix[i], w, tabs_s)
            o_p = flash_prompt(q_p, k_p, v_p)
            o_lat = attn_sample(j, page_table, qext, ckv_s, kpe_s, cache_ckv, cache_kpet)
            x = oproj_sample(x, o_lat, w["wuv_pair"], w["wo"])
            pre, pre_args = "oproj", (o_p, w["wo"])
            p_ckv.append(ckv_p.reshape(BATCH, SEQ, KV_LORA))
            p_kpe.append(kpe_p.reshape(BATCH, SEQ, D_ROPE))
            s_ckv.append(ckv_s.reshape(DEC_BATCH, DEC_SEQ, KV_LORA))
            s_kpe.append(kpe_s.reshape(DEC_BATCH, DEC_SEQ, D_ROPE))
        else:
            p = (ssm_a_re[j], ssm_a_im[j], ssm_log_dt[j], ssm_b_re[j], ssm_b_im[j],
                 ssm_c_re[j], ssm_c_im[j])
            yp, ys, pre_, pim, sre, sim = _s5_mixer(hp, hs, p, state_ssm_re[j], state_ssm_im[j])
            p_re.append(pre_); p_im.append(pim); s_re.append(sre); s_im.append(sim)
            x = glu_sample(x, hs, ys, ssm_d[j], wglu_b, j)
            pre, pre_args = "glu", (hp, yp, ssm_d[j], wglu_b, j)
        args = (i, norm_mlp[i], w1b, w2b)
        if i == DEPTH - 1:
            y_prompt = mlp(x, *args, False, pre, pre_args, "final", norm_final)
            y_sample = mlp(x, *args, True, None, (), "final", norm_final)
        elif i % 2 == 0:
            x, hp = mlp(x, *args, False, pre, pre_args, "norm", norm_mix[i + 1])
            x, hs = mlp(x, *args, True, None, (), "norm", norm_mix[i + 1])
            hp = hp.reshape(NP_ROWS, D_MODEL)
        else:
            x = mlp(x, *args, False, pre, pre_args)
            x = mlp(x, *args, True)
    y_prompt = y_prompt.reshape(BATCH, SEQ, D_MODEL)
    y_sample = y_sample.reshape(DEC_BATCH, DEC_SEQ, D_MODEL)
    return (y_prompt, y_sample,
            jnp.stack(p_ckv), jnp.stack(p_kpe), jnp.stack(p_re), jnp.stack(p_im),
            jnp.stack(s_ckv), jnp.stack(s_kpe), jnp.stack(s_re), jnp.stack(s_im))
```

```python
import functools
import math

import jax
import jax.numpy as jnp
from jax import lax
from jax.experimental import pallas as pl
from jax.experimental.pallas import tpu as pltpu

F32 = jnp.float32
BF16 = jnp.bfloat16

D_MODEL = 1024
BATCH = 8
SEQ = 2048
DEPTH = 4
DEC_BATCH = 128
DEC_SEQ = 8
PAST_LEN = 8192
PAGE_SIZE = 128
N_PAGES = PAST_LEN // PAGE_SIZE
N_HEADS = 16
D_NOPE = 64
D_ROPE = 32
D_V = 64
Q_LORA = 512
KV_LORA = 256
ROPE_THETA = 10000.0
GROUP_SIZE = 16
N_GROUPS = D_MODEL // GROUP_SIZE
STATE_DIM = 64
D_FF = 4 * D_MODEL
EPS = 1e-6
SCALE = (D_NOPE + D_ROPE) ** -0.5
EXP2_SCALE = SCALE * math.log2(math.e)

NP_ROWS = BATCH * SEQ
NS_ROWS = DEC_BATCH * DEC_SEQ
ROWS = NP_ROWS + NS_ROWS

LANES = 128
HEAD_PAD = 128
QEXT = KV_LORA + LANES
VMEM_LIMIT = 52 * 1024 * 1024

TM = 512
TM_PROJ = 256
TM_PROJ_PROMPT = 512
TQ = 256
PAGES_PER_STEP = 32


def _cparams(sem):
    return pltpu.CompilerParams(dimension_semantics=sem, vmem_limit_bytes=VMEM_LIMIT)


def _rms(x, g):
    return x * lax.rsqrt(jnp.mean(x * x, axis=-1, keepdims=True) + EPS) * g


def _dot(a, b):
    return jnp.dot(a, b, preferred_element_type=F32)


def _dot_nt(a, b):
    return lax.dot_general(a, b, (((1,), (1,)), ((), ())), preferred_element_type=F32)


def _const_spec(shape):
    nd = len(shape)
    return pl.BlockSpec(shape, lambda *_: (0,) * nd)


FF_CHUNK = 1024


def _glu_update(x, h, y, d, w_ref):
    gl = jax.nn.gelu(y + d * h).astype(BF16)
    ab = _dot(gl, w_ref[...])
    return x + ab[:, :D_MODEL] * jax.nn.sigmoid(ab[:, D_MODEL:])


def _mlp_body(pre, post, x_ref, *refs):
    x = x_ref[...]
    if pre == "oproj":
        a_ref, wa_ref, *refs = refs
        x = x + _dot(a_ref[...], wa_ref[...])
    elif pre == "glu":
        h_ref, y_ref, d_ref, wg_ref, *refs = refs
        x = _glu_update(x, h_ref[...].reshape(x.shape), y_ref[...].reshape(x.shape), d_ref[...], wg_ref)
    g_ref, w1_ref, w2_ref, *refs = refs
    h = _rms(x, g_ref[...]).astype(BF16)
    acc = x
    for c in range(D_FF // FF_CHUNK):
        sl = slice(c * FF_CHUNK, (c + 1) * FF_CHUNK)
        z = jnp.maximum(_dot(h, w1_ref[:, sl]), 0.0)
        acc = acc + _dot((z * z).astype(BF16), w2_ref[sl, :])
    if post is None:
        (o_ref,) = refs
        o_ref[...] = acc
    elif post == "norm":
        gn_ref, o_ref, hn_ref = refs
        o_ref[...] = acc
        hn_ref[...] = _rms(acc, gn_ref[...]).reshape(hn_ref.shape)
    else:
        gn_ref, y_ref = refs
        y_ref[...] = _rms(acc, gn_ref[...])


def _layer_spec(shape, layer):
    nd = len(shape)
    return pl.BlockSpec((None,) + shape, lambda *_: (layer,) + (0,) * nd, pipeline_mode=pl.Buffered(1))


def mlp(x, layer, g, w1, w2, sample, pre=None, pre_args=(), post=None, g_post=None):
    if sample:
        blk0, grid, rows = NP_ROWS // TM, (NS_ROWS // TM,), NS_ROWS
        x_spec = pl.BlockSpec((TM, D_MODEL), lambda i: (i + blk0, 0))
        own_spec = pl.BlockSpec((TM, D_MODEL), lambda i: (i, 0))
        norm_spec, norm_shape = own_spec, (NS_ROWS, D_MODEL)
    else:
        nci = S5_CHUNKS // S5_CT
        grid, rows = (BATCH, nci), NP_ROWS
        x_spec = pl.BlockSpec((TM, D_MODEL), lambda b, c: (b * nci + c, 0))
        own_spec = x_spec
        norm_spec = pl.BlockSpec((S5_CT, None, S5_Q, D_MODEL), lambda b, c: (c, b, 0, 0))
        norm_shape = (S5_CHUNKS, BATCH, S5_Q, D_MODEL)
    x_out = jax.ShapeDtypeStruct((ROWS, D_MODEL), F32)
    in_place = {0: 0}
    post_args, post_specs = (), []
    if post is None:
        out_specs, out_shape, aliases = x_spec, x_out, in_place
    else:
        post_args, post_specs = (g_post.reshape(1, D_MODEL),), [_const_spec((1, D_MODEL))]
        if post == "norm":
            out_specs, aliases = [x_spec, norm_spec], in_place
            out_shape = [x_out, jax.ShapeDtypeStruct(norm_shape, F32)]
        else:
            out_specs, out_shape, aliases = own_spec, jax.ShapeDtypeStruct((rows, D_MODEL), F32), {}
    pre_specs = []
    if pre == "oproj":
        o, wo = pre_args
        pre_specs = [pl.BlockSpec((TM, N_HEADS * D_V), lambda b, c: (b * nci + c, 0)),
                     pl.BlockSpec(wo.shape, lambda *_: (0, 0), pipeline_mode=pl.Buffered(1))]
    elif pre == "glu":
        h, y, d, wg, glu_layer = pre_args
        hy_spec = pl.BlockSpec((S5_CT, None, S5_Q, D_MODEL), lambda b, c: (c, b, 0, 0))
        pre_args = (h.reshape(S5_CHUNKS, BATCH, S5_Q, D_MODEL), y.reshape(S5_CHUNKS, BATCH, S5_Q, D_MODEL),
                    d.reshape(1, D_MODEL), wg)
        pre_specs = [hy_spec, hy_spec, _const_spec((1, D_MODEL)),
                     _layer_spec((D_MODEL, 2 * D_MODEL), glu_layer)]
    return pl.pallas_call(
        functools.partial(_mlp_body, pre, post),
        grid=grid,
        in_specs=[x_spec] + pre_specs + [_const_spec((1, D_MODEL)),
                                         _layer_spec((D_MODEL, D_FF), layer),
                                         _layer_spec((D_FF, D_MODEL), layer)] + post_specs,
        out_specs=out_specs,
        out_shape=out_shape,
        input_output_aliases=aliases,
        compiler_params=_cparams(("parallel",) * len(grid)),
        name="mlp_sample" if sample else ("mlp_" + (pre or "plain")),
    )(x, *pre_args, g.reshape(1, D_MODEL), w1, w2, *post_args)


def _proj_body(sample, x_ref, gm_ref, win_ref, gq_ref, gkv_ref, wq_ref, wqs_ref,
               cq_ref, sq_ref, ck_ref, sk_ref, wa_ref, wb_ref, *outs):
    h = _rms(x_ref[...], gm_ref[...]).astype(BF16)
    proj = _dot(h, win_ref[...])
    cq = _rms(proj[:, :Q_LORA], gq_ref[...]).astype(BF16)
    ckv = _rms(proj[:, Q_LORA:Q_LORA + KV_LORA], gkv_ref[...])
    kpe = (proj[:, 768:896] * ck_ref[...] + proj[:, 896:1024] * sk_ref[...])
    ckv_ref, kpe_ref = outs[0], outs[1]
    ckv_ref[...] = ckv
    kpe_ref[...] = kpe[:, :D_ROPE]
    cq_tab = cq_ref[...]
    sq_tab = sq_ref[...]
    for p in range(N_HEADS // 2):
        sl = slice(p * 2 * HEAD_PAD, (p + 1) * 2 * HEAD_PAD)
        qp = _dot(cq, wq_ref[:, sl]) * cq_tab + _dot(cq, wqs_ref[:, sl]) * sq_tab
        qp = qp.astype(BF16)
        if sample:
            qext_ref = outs[2]
            for hh in range(2):
                head = 2 * p + hh
                qh = qp[:, hh * HEAD_PAD:(hh + 1) * HEAD_PAD]
                qext_ref[head] = _dot(qh, wa_ref[head])
        else:
            outs[2][:, sl] = qp
    if not sample:
        k_ref, vt_ref = outs[3], outs[4]
        ckv_b = ckv.astype(BF16)
        kpl = pltpu.roll(kpe, D_NOPE, axis=1)
        kpl2 = jnp.concatenate([kpl, kpl], axis=1)
        for p in range(N_HEADS // 2):
            sl = slice(p * 2 * HEAD_PAD, (p + 1) * 2 * HEAD_PAD)
            k_ref[:, sl] = ((_dot(ckv_b, wa_ref[:, sl]) + kpl2) * EXP2_SCALE).astype(BF16)
        vt_ref[...] = _dot_nt(wb_ref[...], ckv_b).astype(BF16)


def mla_proj(x, sample, gm, w, tabs):
    if sample:
        tm = TM_PROJ
        rows, row0, tab_map = NS_ROWS, NP_ROWS // tm, (lambda i: (0, 0))
    else:
        tm = TM_PROJ_PROMPT
        rows, row0, tab_map = NP_ROWS, 0, (lambda i: (i % (SEQ // tm), 0))
    cq, sq, ck, sk = tabs
    wa = w["wuk_ext"] if sample else w["wk"]
    wb = w["wv"]
    out_shape = [jax.ShapeDtypeStruct((rows, KV_LORA), F32),
                 jax.ShapeDtypeStruct((rows, D_ROPE), F32)]
    out_specs = [pl.BlockSpec((tm, KV_LORA), lambda i: (i, 0)),
                 pl.BlockSpec((tm, D_ROPE), lambda i: (i, 0))]
    if sample:
        out_shape.append(jax.ShapeDtypeStruct((N_HEADS, rows, QEXT), F32))
        out_specs.append(pl.BlockSpec((N_HEADS, tm, QEXT), lambda i: (0, i, 0)))
    else:
        out_shape += [jax.ShapeDtypeStruct((rows, N_HEADS * HEAD_PAD), BF16),
                      jax.ShapeDtypeStruct((rows, N_HEADS * HEAD_PAD), BF16),
                      jax.ShapeDtypeStruct((N_HEADS * D_V, rows), BF16)]
        out_specs += [pl.BlockSpec((tm, N_HEADS * HEAD_PAD), lambda i: (i, 0)),
                      pl.BlockSpec((tm, N_HEADS * HEAD_PAD), lambda i: (i, 0)),
                      pl.BlockSpec((N_HEADS * D_V, tm), lambda i: (0, i))]
    return pl.pallas_call(
        functools.partial(_proj_body, sample),
        grid=(rows // tm,),
        in_specs=[
            pl.BlockSpec((tm, D_MODEL), lambda i: (i + row0, 0)),
            _const_spec((1, D_MODEL)),
            _const_spec(w["win"].shape),
            _const_spec((1, Q_LORA)),
            _const_spec((1, KV_LORA)),
            _const_spec(w["wq"].shape),
            _const_spec(w["wqs"].shape),
            pl.BlockSpec((tm, 2 * HEAD_PAD), tab_map),
            pl.BlockSpec((tm, 2 * HEAD_PAD), tab_map),
            pl.BlockSpec((tm, LANES), tab_map),
            pl.BlockSpec((tm, LANES), tab_map),
            _const_spec(wa.shape),
            _const_spec(wb.shape),
        ],
        out_specs=out_specs,
        out_shape=out_shape,
        compiler_params=_cparams(("parallel",)),
        name="mla_proj_sample" if sample else "mla_proj_prompt",
    )(x, gm.reshape(1, D_MODEL), w["win"], w["gq"], w["gkv"], w["wq"], w["wqs"],
      cq, sq, ck, sk, wa, wb)


def _flash_body(q_ref, k_ref, vt_ref, o_ref, s_buf):
    key = lax.broadcasted_iota(jnp.int32, (TQ, TQ), 0)
    qry = lax.broadcasted_iota(jnp.int32, (TQ, TQ), 1)
    items = [(qi, hh) for qi in range(SEQ // TQ) for hh in range(2)]
    ones = jnp.ones((16, SEQ), BF16)

    def scores(n):
        qi, hh = items[n]
        q0 = qi * TQ
        hs = slice(hh * HEAD_PAD, (hh + 1) * HEAD_PAD)
        q = q_ref[q0:q0 + TQ, hs]
        sd = _dot_nt(k_ref[q0:q0 + TQ, hs], q)
        sd = jnp.where(key <= qry, sd, -jnp.inf)
        s_buf[n % 2, q0:q0 + TQ, :] = sd
        m = jnp.max(sd, axis=0, keepdims=True)
        if qi > 0:
            sm = _dot_nt(k_ref[0:q0, hs], q)
            s_buf[n % 2, 0:q0, :] = sm
            m = jnp.maximum(m, jnp.max(sm, axis=0, keepdims=True))
        return m

    def values(n, m):
        qi, hh = items[n]
        kv_len = (qi + 1) * TQ
        p = jnp.exp2(s_buf[n % 2, 0:kv_len, :] - m).astype(BF16)
        vt1 = jnp.concatenate([vt_ref[hh * D_V:(hh + 1) * D_V, 0:kv_len], ones[:, 0:kv_len]], axis=0)
        ot = _dot(vt1, p)
        return ot[:D_V] / ot[D_V:D_V + 1]

    m_next = scores(0)
    ots = []
    for n, (qi, hh) in enumerate(items):
        m = m_next
        if n + 1 < len(items):
            m_next = scores(n + 1)
        ots.append(values(n, m))
        if hh == 1:
            q0 = qi * TQ
            o_ref[q0:q0 + TQ, :] = jnp.concatenate(ots, axis=0).T.astype(o_ref.dtype)
            ots = []


def flash_prompt(q, k, vt):
    return pl.pallas_call(
        _flash_body,
        grid=(BATCH, N_HEADS // 2),
        in_specs=[
            pl.BlockSpec((SEQ, 2 * HEAD_PAD), lambda b, p: (b, p)),
            pl.BlockSpec((SEQ, 2 * HEAD_PAD), lambda b, p: (b, p)),
            pl.BlockSpec((2 * D_V, SEQ), lambda b, p: (p, b)),
        ],
        out_specs=pl.BlockSpec((SEQ, 2 * D_V), lambda b, p: (b, p)),
        out_shape=jax.ShapeDtypeStruct((NP_ROWS, N_HEADS * D_V), BF16),
        scratch_shapes=[pltpu.VMEM((2, SEQ, TQ), F32)],
        compiler_params=_cparams(("parallel", "parallel")),
        name="flash_prompt",
    )(q, k, vt)


N_CHUNKS = N_PAGES // PAGES_PER_STEP
assert N_CHUNKS % 2 == 0 and N_CHUNKS * PAGES_PER_STEP == N_PAGES
SATTN_KEYS = 2048
QROWS = N_HEADS * DEC_SEQ


def _sattn_body(layer, pt_ref, q_ref, cnew_ref, knew_ref, ckv_hbm, kpet_hbm, o_ref,
                cbuf, kbuf, sem, s_buf, ck_b):
    n = PAGES_PER_STEP
    b = pl.program_id(0)

    def chunk_copies(req, c, slot):
        cps = []
        for k in range(n):
            page = pt_ref[req, c * n + k]
            rows = pl.ds(k * PAGE_SIZE, PAGE_SIZE)
            cps.append(pltpu.make_async_copy(ckv_hbm.at[layer, page], cbuf.at[slot, rows, :],
                                             sem.at[slot, 0]))
            cps.append(pltpu.make_async_copy(kpet_hbm.at[layer, page], kbuf.at[slot, :, rows],
                                             sem.at[slot, 1]))
        return cps

    def start(req, c, slot):
        for cp in chunk_copies(req, c, slot):
            cp.start()

    def wait(req, c, slot):
        for cp in chunk_copies(req, c, slot):
            cp.wait()

    @pl.when(b == 0)
    def _():
        for c in range(2):
            start(0, c, c)

    q = q_ref[...].reshape(QROWS, QEXT)
    qlat = q[:, :KV_LORA].astype(BF16)
    qpe = q[:, KV_LORA:KV_LORA + D_ROPE].astype(BF16)

    n_sb = n * PAGE_SIZE // SATTN_KEYS
    items = [(c, sb) for c in range(N_CHUNKS) for sb in range(n_sb)]

    def scores(i):
        c, sb = items[i]
        keys = slice(sb * SATTN_KEYS, (sb + 1) * SATTN_KEYS)
        if sb == 0:
            wait(b, c, c % 2)
        ck = cbuf[c % 2, keys, :].astype(BF16)
        kt = kbuf[c % 2, :, keys].astype(BF16)
        ck_b[i % 2] = ck
        s = _dot_nt(qlat, ck) + _dot(qpe, kt)
        s_buf[i % 2] = s
        if sb == n_sb - 1:
            if c + 2 < N_CHUNKS:
                start(b, c + 2, c % 2)
            else:
                @pl.when(b + 1 < DEC_BATCH)
                def _():
                    start(b + 1, c + 2 - N_CHUNKS, c % 2)
        return jnp.max(s, axis=-1, keepdims=True)

    bm_next = scores(0)

    pad = PAGE_SIZE - DEC_SEQ
    cn = jnp.concatenate([cnew_ref[...], jnp.zeros((pad, KV_LORA), F32)], axis=0).astype(BF16)
    kn = jnp.concatenate([knew_ref[...], jnp.zeros((pad, D_ROPE), F32)], axis=0).astype(BF16)
    s = _dot_nt(qlat, cn) + _dot_nt(qpe, kn)
    tok = lax.broadcasted_iota(jnp.int32, (QROWS, PAGE_SIZE), 0) % DEC_SEQ
    key = lax.broadcasted_iota(jnp.int32, (QROWS, PAGE_SIZE), 1)
    s = jnp.where(key <= tok, s, -jnp.inf)
    m = jnp.max(s, axis=-1, keepdims=True)
    p = jnp.exp2((s - m) * EXP2_SCALE)
    l = jnp.sum(p, axis=-1, keepdims=True)
    acc = _dot(p.astype(BF16), cn)

    for i in range(len(items)):
        bm = bm_next
        if i + 1 < len(items):
            bm_next = scores(i + 1)
        m_new = jnp.maximum(m, bm)
        alpha = jnp.exp2((m - m_new) * EXP2_SCALE)
        p = jnp.exp2((s_buf[i % 2] - m_new) * EXP2_SCALE)
        l = alpha * l + jnp.sum(p, axis=-1, keepdims=True)
        acc = alpha * acc + _dot(p.astype(BF16), ck_b[i % 2])
        m = m_new

    o_ref[...] = (acc / l).reshape(N_HEADS, DEC_SEQ, KV_LORA)


def attn_sample(layer, page_table, qext, ckv_all, kpe_all, cache_ckv, cache_kpet):
    n = PAGES_PER_STEP
    grid_spec = pltpu.PrefetchScalarGridSpec(
        num_scalar_prefetch=1,
        grid=(DEC_BATCH,),
        in_specs=[
            pl.BlockSpec((N_HEADS, DEC_SEQ, QEXT), lambda b, pt: (0, b, 0)),
            pl.BlockSpec((DEC_SEQ, KV_LORA), lambda b, pt: (b, 0)),
            pl.BlockSpec((DEC_SEQ, D_ROPE), lambda b, pt: (b, 0)),
            pl.BlockSpec(memory_space=pl.ANY),
            pl.BlockSpec(memory_space=pl.ANY),
        ],
        out_specs=pl.BlockSpec((N_HEADS, DEC_SEQ, KV_LORA), lambda b, pt: (0, b, 0)),
        scratch_shapes=[
            pltpu.VMEM((2, n * PAGE_SIZE, KV_LORA), F32),
            pltpu.VMEM((2, D_ROPE, n * PAGE_SIZE), F32),
            pltpu.SemaphoreType.DMA((2, 2)),
            pltpu.VMEM((2, QROWS, SATTN_KEYS), F32),
            pltpu.VMEM((2, SATTN_KEYS, KV_LORA), BF16),
        ],
    )
    return pl.pallas_call(
        functools.partial(_sattn_body, layer),
        grid_spec=grid_spec,
        out_shape=jax.ShapeDtypeStruct((N_HEADS, NS_ROWS, KV_LORA), F32),
        compiler_params=_cparams(("arbitrary",)),
        name="attn_sample",
    )(page_table, qext, ckv_all, kpe_all, cache_ckv, cache_kpet)


def _oproj_sample_body(x_ref, ol_ref, wuv_ref, wo_ref, out_ref):
    acc = x_ref[...]
    for p in range(N_HEADS // 2):
        pair = jnp.concatenate([ol_ref[2 * p], ol_ref[2 * p + 1]], axis=1).astype(BF16)
        o_pair = _dot(pair, wuv_ref[p]).astype(BF16)
        acc = acc + _dot(o_pair, wo_ref[p * 2 * D_V:(p + 1) * 2 * D_V, :])
    out_ref[...] = acc


def oproj_sample(x, o_lat, wuv_pair, wo):
    row0 = NP_ROWS // TM
    return pl.pallas_call(
        _oproj_sample_body,
        grid=(NS_ROWS // TM,),
        in_specs=[pl.BlockSpec((TM, D_MODEL), lambda i: (i + row0, 0)),
                  pl.BlockSpec((N_HEADS, TM, KV_LORA), lambda i: (0, i, 0)),
                  _const_spec(wuv_pair.shape),
                  _const_spec(wo.shape)],
        out_specs=pl.BlockSpec((TM, D_MODEL), lambda i: (i + row0, 0)),
        out_shape=jax.ShapeDtypeStruct((ROWS, D_MODEL), F32),
        input_output_aliases={0: 0},
        compiler_params=_cparams(("parallel",)),
        name="oproj_sample",
    )(x, o_lat, wuv_pair, wo)


S5_Q = DEC_SEQ
S5_BLK = LANES // GROUP_SIZE
S5_NBLK = N_GROUPS // S5_BLK
S5_HALF = S5_BLK * STATE_DIM
S5_CT = 64
S5_CHUNKS = SEQ // S5_Q


def _s5_body(n_chunks, nb, h_ref, t_ref, m_ref, n_ref, a_ref, h0_ref, y_ref, hf_ref,
             s_s, p_s, st_s):
    rows = n_chunks * nb

    @pl.when(pl.program_id(1) == 0)
    def _():
        st_s[...] = h0_ref[...]

    u = jnp.concatenate([h_ref[pl.ds(k, rows, stride=S5_Q), :].astype(BF16)
                         for k in range(S5_Q)], axis=1)
    s_s[...] = _dot(u, m_ref[...])
    are = a_ref[:, :S5_HALF]
    aim = a_ref[:, S5_HALF:]

    def step(c, h):
        r0 = pl.multiple_of(c * nb, nb)
        p_s[pl.ds(r0, nb), :] = h
        s = s_s[pl.ds(r0, nb), :]
        hre, him = h[:, :S5_HALF], h[:, S5_HALF:]
        nre = are * hre - aim * him + s[:, :S5_HALF]
        nim = are * him + aim * hre + s[:, S5_HALF:]
        return jnp.concatenate([nre, nim], axis=1)

    h = lax.fori_loop(0, n_chunks, step, st_s[...])
    st_s[...] = h
    hf_ref[...] = h
    y = _dot(u, t_ref[...]) + _dot(p_s[...].astype(BF16), n_ref[...])
    for l in range(S5_Q):
        y_ref[pl.ds(l, rows, stride=S5_Q), :] = y[:, l * LANES:(l + 1) * LANES]


def s5_scan(h, tabs, h0, n_chunks, nb):
    tbig, mbig, nbig, abig = tabs
    rows = n_chunks * nb
    tile = rows * S5_Q
    width = S5_Q * LANES

    def bspec(shape):
        return pl.BlockSpec((None,) + shape, lambda j, c: (j, 0, 0))

    return pl.pallas_call(
        functools.partial(_s5_body, n_chunks, nb),
        grid=(S5_NBLK, h.shape[0] // tile),
        in_specs=[pl.BlockSpec((tile, LANES), lambda j, c: (c, j)),
                  bspec((width, width)), bspec((width, 2 * S5_HALF)), bspec((2 * S5_HALF, width)),
                  bspec((1, 2 * S5_HALF)), bspec((nb, 2 * S5_HALF))],
        out_specs=[pl.BlockSpec((tile, LANES), lambda j, c: (c, j)), bspec((nb, 2 * S5_HALF))],
        out_shape=[jax.ShapeDtypeStruct(h.shape, F32),
                   jax.ShapeDtypeStruct((S5_NBLK, nb, 2 * S5_HALF), F32)],
        scratch_shapes=[pltpu.VMEM((rows, 2 * S5_HALF), F32), pltpu.VMEM((rows, 2 * S5_HALF), F32),
                        pltpu.VMEM((nb, 2 * S5_HALF), F32)],
        compiler_params=_cparams(("parallel", "arbitrary")),
        name="s5_scan",
    )(h, tbig, mbig, nbig, abig, h0)


def _glu_body(x_ref, h_ref, y_ref, d_ref, w_ref, o_ref):
    o_ref[...] = _glu_update(x_ref[...], h_ref[...], y_ref[...], d_ref[...], w_ref)


def glu_sample(x, h, y, d, w, layer):
    blk0 = NP_ROWS // TM
    x_spec = pl.BlockSpec((TM, D_MODEL), lambda i: (i + blk0, 0))
    hy_spec = pl.BlockSpec((TM, D_MODEL), lambda i: (i, 0))
    return pl.pallas_call(
        _glu_body,
        grid=(NS_ROWS // TM,),
        in_specs=[x_spec, hy_spec, hy_spec, _const_spec((1, D_MODEL)),
                  _layer_spec((D_MODEL, 2 * D_MODEL), layer)],
        out_specs=x_spec,
        out_shape=jax.ShapeDtypeStruct((ROWS, D_MODEL), F32),
        input_output_aliases={0: 0},
        compiler_params=_cparams(("parallel",)),
        name="glu_sample",
    )(x, h, y, d.reshape(1, D_MODEL), w)


def _prep_mla(w_in, g_q, g_kv, w_q_up, w_kv_up, w_o):
    z = lambda *s: jnp.zeros(s, F32)
    wk_pe = w_in[:, Q_LORA + KV_LORA:]
    half = D_ROPE // 2
    win = jnp.concatenate(
        [w_in[:, :Q_LORA + KV_LORA], wk_pe, z(D_MODEL, LANES - D_ROPE),
         -wk_pe[:, half:], wk_pe[:, :half], z(D_MODEL, LANES - D_ROPE)], axis=1)
    wq3 = w_q_up.reshape(Q_LORA, N_HEADS, D_NOPE + D_ROPE)
    nope, x1, x2 = wq3[..., :D_NOPE], wq3[..., D_NOPE:D_NOPE + half], wq3[..., D_NOPE + half:]
    zpad = z(Q_LORA, N_HEADS, HEAD_PAD - D_NOPE - D_ROPE)
    wq = jnp.concatenate([nope, x1, x2, zpad], axis=-1).reshape(Q_LORA, N_HEADS * HEAD_PAD)
    wqs = jnp.concatenate([z(Q_LORA, N_HEADS, D_NOPE), -x2, x1, zpad], axis=-1)
    wqs = wqs.reshape(Q_LORA, N_HEADS * HEAD_PAD)
    wkv3 = w_kv_up.reshape(KV_LORA, N_HEADS, D_NOPE + D_V)
    w_uk, w_uv = wkv3[..., :D_NOPE], wkv3[..., D_NOPE:]
    wk = jnp.concatenate([w_uk, z(KV_LORA, N_HEADS, HEAD_PAD - D_NOPE)], axis=-1)
    wk = wk.reshape(KV_LORA, N_HEADS * HEAD_PAD)
    wv = w_uv.reshape(KV_LORA, N_HEADS * D_V).T
    wuk_ext = jnp.zeros((N_HEADS, HEAD_PAD, QEXT), F32)
    wuk_ext = wuk_ext.at[:, :D_NOPE, :KV_LORA].set(jnp.transpose(w_uk, (1, 2, 0)))
    wuk_ext = wuk_ext.at[:, D_NOPE:D_NOPE + D_ROPE, KV_LORA:KV_LORA + D_ROPE].set(
        jnp.broadcast_to(jnp.eye(D_ROPE, dtype=F32), (N_HEADS, D_ROPE, D_ROPE)))
    uv = jnp.transpose(w_uv, (1, 0, 2)).reshape(N_HEADS // 2, 2, KV_LORA, D_V)
    wuv_pair = jnp.zeros((N_HEADS // 2, 2 * KV_LORA, 2 * D_V), F32)
    wuv_pair = wuv_pair.at[:, :KV_LORA, :D_V].set(uv[:, 0])
    wuv_pair = wuv_pair.at[:, KV_LORA:, D_V:].set(uv[:, 1])
    return dict(win=win.astype(BF16), gq=g_q.reshape(1, Q_LORA), gkv=g_kv.reshape(1, KV_LORA),
                wq=wq.astype(BF16), wqs=wqs.astype(BF16), wk=wk.astype(BF16), wv=wv.astype(BF16),
                wuk_ext=wuk_ext.astype(BF16), wuv_pair=wuv_pair.astype(BF16), wo=w_o.astype(BF16))


def _rope_tabs(pos, reps):
    half = D_ROPE // 2
    inv = ROPE_THETA ** (-jnp.arange(half, dtype=F32) / half)
    ang = pos.astype(F32)[:, None] * inv[None, :]
    cos, sin = jnp.cos(ang), jnp.sin(ang)
    n = pos.shape[0]
    one, zero = jnp.ones((n, D_NOPE), F32), jnp.zeros((n, D_NOPE), F32)
    zpad = jnp.zeros((n, HEAD_PAD - D_NOPE - D_ROPE), F32)
    cq = jnp.concatenate([one, cos, cos, zpad], axis=1)
    sq = jnp.concatenate([zero, sin, sin, zpad], axis=1)
    kpad = jnp.zeros((n, LANES - D_ROPE), F32)
    ck = jnp.concatenate([cos, cos, kpad], axis=1)
    sk = jnp.concatenate([sin, sin, kpad], axis=1)
    cq, sq = jnp.tile(cq, (reps, 2)), jnp.tile(sq, (reps, 2))
    return cq, sq, jnp.tile(ck, (reps, 1)), jnp.tile(sk, (reps, 1))


def _prep_s5(a_re, a_im, log_dt, b_re, b_im, c_re, c_im, q):
    hi = lax.Precision.HIGHEST
    dt = jnp.exp(log_dt)[:, None]
    dre, dim = dt * a_re, dt * a_im
    mag = jnp.exp(dre)
    abr, abi = mag * jnp.cos(dim), mag * jnp.sin(dim)
    den = a_re * a_re + a_im * a_im
    xr, xi = abr - 1.0, abi
    fr, fi = (xr * a_re + xi * a_im) / den, (xi * a_re - xr * a_im) / den
    bbr = fr[..., None] * b_re - fi[..., None] * b_im
    bbi = fr[..., None] * b_im + fi[..., None] * b_re
    j = jnp.arange(q + 1, dtype=F32)[:, None, None]
    pmag = jnp.exp(j * dre)
    pwr, pwi = pmag * jnp.cos(j * dim), pmag * jnp.sin(j * dim)
    pbr = pwr[..., None] * bbr - pwi[..., None] * bbi
    pbi = pwr[..., None] * bbi + pwi[..., None] * bbr
    kk = (jnp.einsum("gtp,jgps->jgst", c_re, pbr[:q], precision=hi)
          - jnp.einsum("gtp,jgps->jgst", c_im, pbi[:q], precision=hi))
    g = a_re.shape[0]
    taps = jnp.transpose(kk, (1, 2, 0, 3)).reshape(g, GROUP_SIZE, q * GROUP_SIZE)
    t = jnp.stack([jnp.pad(taps[..., :(q - k) * GROUP_SIZE], ((0, 0), (0, 0), (k * GROUP_SIZE, 0)))
                   for k in range(q)], axis=1).reshape(g, q * GROUP_SIZE, q * GROUP_SIZE)
    mre = jnp.transpose(pbr[:q][::-1], (1, 0, 3, 2)).reshape(g, q * GROUP_SIZE, STATE_DIM)
    mim = jnp.transpose(pbi[:q][::-1], (1, 0, 3, 2)).reshape(g, q * GROUP_SIZE, STATE_DIM)
    ncr = pwr[1:, :, None, :] * c_re[None] - pwi[1:, :, None, :] * c_im[None]
    nci = pwr[1:, :, None, :] * c_im[None] + pwi[1:, :, None, :] * c_re[None]
    nre = jnp.transpose(ncr, (1, 3, 0, 2)).reshape(g, STATE_DIM, q * GROUP_SIZE)
    nim = jnp.transpose(nci, (1, 3, 0, 2)).reshape(g, STATE_DIM, q * GROUP_SIZE)
    are, aim = pwr[q][:, None, :], pwi[q][:, None, :]
    return t, mre, mim, nre, nim, are, aim


def _s5_tables_body(at_ref, am_ref, an_ref, et_ref, em_ref, t_ref, m_ref, n_ref):
    width = S5_Q * LANES
    row = lax.broadcasted_iota(jnp.int32, (width, width), 0)
    col = lax.broadcasted_iota(jnp.int32, (width, width), 1)

    def expand(a_ref, e_ref, row_shift, col_shift):
        full = _dot(a_ref[...].astype(BF16), e_ref[...])
        keep = ((row >> row_shift) & (S5_BLK - 1)) == ((col >> col_shift) & (S5_BLK - 1))
        return jnp.where(keep, full, 0.0).astype(BF16)

    gs, sd = GROUP_SIZE.bit_length() - 1, STATE_DIM.bit_length() - 1
    t_ref[...] = expand(at_ref, et_ref, gs, gs)
    m_ref[...] = expand(am_ref, em_ref, gs, sd)
    n_ref[...] = expand(an_ref, et_ref, sd, gs)


def _s5_block_tables(p):
    t, mre, mim, nre, nim, are, aim = _prep_s5(*p, S5_Q)
    q = S5_Q
    width = q * LANES
    assert 2 * S5_HALF == width and q * GROUP_SIZE == LANES and 2 * STATE_DIM == LANES

    def rows_kgs(a):
        a5 = a.reshape(S5_NBLK, S5_BLK, q, GROUP_SIZE, LANES)
        return jnp.transpose(a5, (0, 2, 1, 3, 4)).reshape(S5_NBLK, width, LANES)

    at = rows_kgs(t)
    am = rows_kgs(jnp.concatenate([mre, mim], axis=-1))
    an = jnp.stack([nre.reshape(S5_NBLK, S5_BLK, STATE_DIM, LANES),
                    -nim.reshape(S5_NBLK, S5_BLK, STATE_DIM, LANES)], axis=1)
    an = an.reshape(S5_NBLK, width, LANES)
    r = jnp.arange(LANES)[:, None]
    c = jnp.arange(width)[None, :]
    et = ((r // GROUP_SIZE == c // LANES) & (r % GROUP_SIZE == c % GROUP_SIZE)).astype(BF16)
    em = ((r // STATE_DIM == c // S5_HALF) & (r % STATE_DIM == c % STATE_DIM)).astype(BF16)
    cspec = pl.BlockSpec((None, width, LANES), lambda j: (j, 0, 0))
    ospec = pl.BlockSpec((None, width, width), lambda j: (j, 0, 0))
    big = jax.ShapeDtypeStruct((S5_NBLK, width, width), BF16)
    tbig, mbig, nbig = pl.pallas_call(
        _s5_tables_body,
        grid=(S5_NBLK,),
        in_specs=[cspec, cspec, cspec, _const_spec((LANES, width)), _const_spec((LANES, width))],
        out_specs=[ospec, ospec, ospec],
        out_shape=[big, big, big],
        compiler_params=_cparams(("parallel",)),
        name="s5_tables",
    )(at, am, an, et, em)
    abig = jnp.concatenate([are.reshape(S5_NBLK, 1, S5_HALF), aim.reshape(S5_NBLK, 1, S5_HALF)], axis=2)
    return tbig, mbig, nbig, abig


def _state_to_blocks(re, im):
    nb = re.shape[0]
    f = lambda a: jnp.transpose(a.reshape(nb, S5_NBLK, S5_HALF), (1, 0, 2))
    return jnp.concatenate([f(re), f(im)], axis=2)


def _blocks_to_state(hf):
    nb = hf.shape[1]
    f = lambda a: jnp.transpose(a, (1, 0, 2)).reshape(nb, N_GROUPS, STATE_DIM)
    return f(hf[..., :S5_HALF]), f(hf[..., S5_HALF:])


def _s5_mixer(hp, hs, p, h0re, h0im):
    tabs = _s5_block_tables(p)
    zero = jnp.zeros((S5_NBLK, BATCH, 2 * S5_HALF), F32)
    yp, hfp = s5_scan(hp, tabs, zero, S5_CT, BATCH)
    ys, hfs = s5_scan(hs, tabs, _state_to_blocks(h0re, h0im), 1, DEC_BATCH)
    return (yp, ys) + _blocks_to_state(hfp) + _blocks_to_state(hfs)


def kernel(x_prompt, x_sample, cache_ckv, cache_kpe, state_ssm_re, state_ssm_im, page_table, norm_mix, norm_mlp, norm_final, mla_w_in, mla_q_norm, mla_kv_norm, mla_w_q_up, mla_w_kv_up, mla_w_o, ssm_a_re, ssm_a_im, ssm_log_dt, ssm_b_re, ssm_b_im, ssm_c_re, ssm_c_im, ssm_d, ssm_w_glu, mlp_w1, mlp_w2):
    x = jnp.concatenate([x_prompt.reshape(NP_ROWS, D_MODEL), x_sample.reshape(NS_ROWS, D_MODEL)], axis=0)
    tabs_p = _rope_tabs(jnp.arange(SEQ, dtype=jnp.int32), 1)
    tabs_s = _rope_tabs(PAST_LEN + jnp.arange(DEC_SEQ, dtype=jnp.int32), TM_PROJ // DEC_SEQ)
    cache_kpet = jnp.swapaxes(cache_kpe, 2, 3)
    w1b, w2b, wglu_b = mlp_w1.astype(BF16), mlp_w2.astype(BF16), ssm_w_glu.astype(BF16)
    p_ckv, p_kpe, p_re, p_im = [], [], [], []
    s_ckv, s_kpe, s_re, s_im = [], [], [], []
    for i in range(DEPTH):
        j = i // 2
        if i % 2 == 0:
            w = _prep_mla(mla_w_in[j], mla_q_norm[j], mla_kv_norm[j], mla_w_q_up[j],
                          mla_w_kv_up[j], mla_w_o[j])
            ckv_p, kpe_p, q_p, k_p, v_p = mla_proj(x, False, norm_mix[i], w, tabs_p)
            ckv_s, kpe_s, qext = mla_proj(x, True, norm_mix[i], w, tabs_s)
            o_p = flash_prompt(q_p, k_p, v_p)
            o_lat = attn_sample(j, page_table, qext, ckv_s, kpe_s, cache_ckv, cache_kpet)
            pre, pre_args = "oproj", (o_p, w["wo"])
            sample_mixer = functools.partial(oproj_sample, o_lat=o_lat, wuv_pair=w["wuv_pair"],
                                             wo=w["wo"])
            p_ckv.append(ckv_p.reshape(BATCH, SEQ, KV_LORA))
            p_kpe.append(kpe_p.reshape(BATCH, SEQ, D_ROPE))
            s_ckv.append(ckv_s.reshape(DEC_BATCH, DEC_SEQ, KV_LORA))
            s_kpe.append(kpe_s.reshape(DEC_BATCH, DEC_SEQ, D_ROPE))
        else:
            p = (ssm_a_re[j], ssm_a_im[j], ssm_log_dt[j], ssm_b_re[j], ssm_b_im[j],
                 ssm_c_re[j], ssm_c_im[j])
            yp, ys, pre_, pim, sre, sim = _s5_mixer(hp, hs, p, state_ssm_re[j], state_ssm_im[j])
            p_re.append(pre_); p_im.append(pim); s_re.append(sre); s_im.append(sim)
            pre, pre_args = "glu", (hp, yp, ssm_d[j], wglu_b, j)
            sample_mixer = functools.partial(glu_sample, h=hs, y=ys, d=ssm_d[j], w=wglu_b, layer=j)
        args = (i, norm_mlp[i], w1b, w2b)
        if i == DEPTH - 1:
            y_prompt = mlp(x, *args, False, pre, pre_args, "final", norm_final)
            y_sample = mlp(sample_mixer(x), *args, True, None, (), "final", norm_final)
        elif i % 2 == 0:
            x, hp = mlp(x, *args, False, pre, pre_args, "norm", norm_mix[i + 1])
            x, hs = mlp(sample_mixer(x), *args, True, None, (), "norm", norm_mix[i + 1])
            hp = hp.reshape(NP_ROWS, D_MODEL)
        else:
            x = mlp(x, *args, False, pre, pre_args)
            x = mlp(sample_mixer(x), *args, True)
    y_prompt = y_prompt.reshape(BATCH, SEQ, D_MODEL)
    y_sample = y_sample.reshape(DEC_BATCH, DEC_SEQ, D_MODEL)
    return (y_prompt, y_sample,
            jnp.stack(p_ckv), jnp.stack(p_kpe), jnp.stack(p_re), jnp.stack(p_im),
            jnp.stack(s_ckv), jnp.stack(s_kpe), jnp.stack(s_re), jnp.stack(s_im))
```

```python
import functools
import math

import jax
import jax.numpy as jnp
from jax import lax
from jax.experimental import pallas as pl
from jax.experimental.pallas import tpu as pltpu

F32 = jnp.float32
BF16 = jnp.bfloat16

D_MODEL = 1024
BATCH = 8
SEQ = 2048
DEPTH = 4
DEC_BATCH = 128
DEC_SEQ = 8
PAST_LEN = 8192
PAGE_SIZE = 128
N_PAGES = PAST_LEN // PAGE_SIZE
N_HEADS = 16
D_NOPE = 64
D_ROPE = 32
D_V = 64
Q_LORA = 512
KV_LORA = 256
ROPE_THETA = 10000.0
GROUP_SIZE = 16
N_GROUPS = D_MODEL // GROUP_SIZE
STATE_DIM = 64
D_FF = 4 * D_MODEL
EPS = 1e-6
SCALE = (D_NOPE + D_ROPE) ** -0.5
EXP2_SCALE = SCALE * math.log2(math.e)

NP_ROWS = BATCH * SEQ
NS_ROWS = DEC_BATCH * DEC_SEQ
ROWS = NP_ROWS + NS_ROWS

LANES = 128
HEAD_PAD = 128
QEXT = KV_LORA + LANES
VMEM_LIMIT = 52 * 1024 * 1024

TM = 512
TM_PROJ = 256
TM_PROJ_PROMPT = 512
TQ = 256
PAGES_PER_STEP = 32


def _cparams(sem):
    return pltpu.CompilerParams(dimension_semantics=sem, vmem_limit_bytes=VMEM_LIMIT)


def _rms(x, g):
    return x * lax.rsqrt(jnp.mean(x * x, axis=-1, keepdims=True) + EPS) * g


def _dot(a, b):
    return jnp.dot(a, b, preferred_element_type=F32)


def _dot_nt(a, b):
    return lax.dot_general(a, b, (((1,), (1,)), ((), ())), preferred_element_type=F32)


def _const_spec(shape):
    nd = len(shape)
    return pl.BlockSpec(shape, lambda *_: (0,) * nd)


FF_CHUNK = 1024


def _glu_update(x, h, y, d, w_ref):
    gl = jax.nn.gelu(y + d * h).astype(BF16)
    ab = _dot(gl, w_ref[...])
    return x + ab[:, :D_MODEL] * jax.nn.sigmoid(ab[:, D_MODEL:])


def _mlp_body(pre, post, x_ref, *refs):
    x = x_ref[...]
    if pre == "oproj":
        a_ref, wa_ref, *refs = refs
        x = x + _dot(a_ref[...], wa_ref[...])
    elif pre == "glu":
        h_ref, y_ref, d_ref, wg_ref, *refs = refs
        x = _glu_update(x, h_ref[...].reshape(x.shape), y_ref[...].reshape(x.shape), d_ref[...], wg_ref)
    g_ref, w1_ref, w2_ref, *refs = refs
    h = _rms(x, g_ref[...]).astype(BF16)
    acc = x
    for c in range(D_FF // FF_CHUNK):
        sl = slice(c * FF_CHUNK, (c + 1) * FF_CHUNK)
        z = jnp.maximum(_dot(h, w1_ref[:, sl]), 0.0)
        acc = acc + _dot((z * z).astype(BF16), w2_ref[sl, :])
    if post is None:
        (o_ref,) = refs
        o_ref[...] = acc
    elif post == "norm":
        gn_ref, o_ref, hn_ref = refs
        o_ref[...] = acc
        hn_ref[...] = _rms(acc, gn_ref[...]).reshape(hn_ref.shape)
    else:
        gn_ref, y_ref = refs
        y_ref[...] = _rms(acc, gn_ref[...])


def _layer_spec(shape, layer):
    nd = len(shape)
    return pl.BlockSpec((None,) + shape, lambda *_: (layer,) + (0,) * nd, pipeline_mode=pl.Buffered(1))


def mlp(x, layer, g, w1, w2, sample, pre=None, pre_args=(), post=None, g_post=None):
    if sample:
        blk0, grid, rows = NP_ROWS // TM, (NS_ROWS // TM,), NS_ROWS
        x_spec = pl.BlockSpec((TM, D_MODEL), lambda i: (i + blk0, 0))
        own_spec = pl.BlockSpec((TM, D_MODEL), lambda i: (i, 0))
        norm_spec, norm_shape = own_spec, (NS_ROWS, D_MODEL)
    else:
        nci = S5_CHUNKS // S5_CT
        grid, rows = (BATCH, nci), NP_ROWS
        x_spec = pl.BlockSpec((TM, D_MODEL), lambda b, c: (b * nci + c, 0))
        own_spec = x_spec
        norm_spec = pl.BlockSpec((S5_CT, None, S5_Q, D_MODEL), lambda b, c: (c, b, 0, 0))
        norm_shape = (S5_CHUNKS, BATCH, S5_Q, D_MODEL)
    x_out = jax.ShapeDtypeStruct((ROWS, D_MODEL), F32)
    in_place = {0: 0}
    post_args, post_specs = (), []
    if post is None:
        out_specs, out_shape, aliases = x_spec, x_out, in_place
    else:
        post_args, post_specs = (g_post.reshape(1, D_MODEL),), [_const_spec((1, D_MODEL))]
        if post == "norm":
            out_specs, aliases = [x_spec, norm_spec], in_place
            out_shape = [x_out, jax.ShapeDtypeStruct(norm_shape, F32)]
        else:
            out_specs, out_shape, aliases = own_spec, jax.ShapeDtypeStruct((rows, D_MODEL), F32), {}
    pre_specs = []
    if pre == "oproj":
        o, wo = pre_args
        pre_specs = [pl.BlockSpec((TM, N_HEADS * D_V), lambda b, c: (b * nci + c, 0)),
                     pl.BlockSpec(wo.shape, lambda *_: (0, 0), pipeline_mode=pl.Buffered(1))]
    elif pre == "glu":
        h, y, d, wg, glu_layer = pre_args
        hy_spec = pl.BlockSpec((S5_CT, None, S5_Q, D_MODEL), lambda b, c: (c, b, 0, 0))
        pre_args = (h.reshape(S5_CHUNKS, BATCH, S5_Q, D_MODEL), y.reshape(S5_CHUNKS, BATCH, S5_Q, D_MODEL),
                    d.reshape(1, D_MODEL), wg)
        pre_specs = [hy_spec, hy_spec, _const_spec((1, D_MODEL)),
                     _layer_spec((D_MODEL, 2 * D_MODEL), glu_layer)]
    return pl.pallas_call(
        functools.partial(_mlp_body, pre, post),
        grid=grid,
        in_specs=[x_spec] + pre_specs + [_const_spec((1, D_MODEL)),
                                         _layer_spec((D_MODEL, D_FF), layer),
                                         _layer_spec((D_FF, D_MODEL), layer)] + post_specs,
        out_specs=out_specs,
        out_shape=out_shape,
        input_output_aliases=aliases,
        compiler_params=_cparams(("parallel",) * len(grid)),
        name="mlp_sample" if sample else ("mlp_" + (pre or "plain")),
    )(x, *pre_args, g.reshape(1, D_MODEL), w1, w2, *post_args)


def _proj_body(sample, x_ref, gm_ref, win_ref, gq_ref, gkv_ref, wq_ref, wqs_ref,
               cq_ref, sq_ref, ck_ref, sk_ref, wa_ref, wb_ref, *outs):
    h = _rms(x_ref[...], gm_ref[...]).astype(BF16)
    proj = _dot(h, win_ref[...])
    cq = _rms(proj[:, :Q_LORA], gq_ref[...]).astype(BF16)
    ckv = _rms(proj[:, Q_LORA:Q_LORA + KV_LORA], gkv_ref[...])
    kpe = (proj[:, 768:896] * ck_ref[...] + proj[:, 896:1024] * sk_ref[...])
    ckv_ref, kpe_ref = outs[0], outs[1]
    ckv_ref[...] = ckv
    kpe_ref[...] = kpe[:, :D_ROPE]
    cq_tab = cq_ref[...]
    sq_tab = sq_ref[...]
    for p in range(N_HEADS // 2):
        sl = slice(p * 2 * HEAD_PAD, (p + 1) * 2 * HEAD_PAD)
        qp = _dot(cq, wq_ref[:, sl]) * cq_tab + _dot(cq, wqs_ref[:, sl]) * sq_tab
        qp = qp.astype(BF16)
        if sample:
            qext_ref = outs[2]
            for hh in range(2):
                head = 2 * p + hh
                qh = qp[:, hh * HEAD_PAD:(hh + 1) * HEAD_PAD]
                qext_ref[head] = _dot(qh, wa_ref[head])
        else:
            outs[2][:, sl] = qp
    if not sample:
        k_ref, vt_ref = outs[3], outs[4]
        ckv_b = ckv.astype(BF16)
        kpl = pltpu.roll(kpe, D_NOPE, axis=1)
        kpl2 = jnp.concatenate([kpl, kpl], axis=1)
        for p in range(N_HEADS // 2):
            sl = slice(p * 2 * HEAD_PAD, (p + 1) * 2 * HEAD_PAD)
            k_ref[:, sl] = ((_dot(ckv_b, wa_ref[:, sl]) + kpl2) * EXP2_SCALE).astype(BF16)
        vt_ref[...] = _dot_nt(wb_ref[...], ckv_b).astype(BF16)


def mla_proj(x, sample, gm, w, tabs):
    if sample:
        tm = TM_PROJ
        rows, row0, tab_map = NS_ROWS, NP_ROWS // tm, (lambda i: (0, 0))
    else:
        tm = TM_PROJ_PROMPT
        rows, row0, tab_map = NP_ROWS, 0, (lambda i: (i % (SEQ // tm), 0))
    cq, sq, ck, sk = tabs
    wa = w["wuk_ext"] if sample else w["wk"]
    wb = w["wv"]
    out_shape = [jax.ShapeDtypeStruct((rows, KV_LORA), F32),
                 jax.ShapeDtypeStruct((rows, D_ROPE), F32)]
    out_specs = [pl.BlockSpec((tm, KV_LORA), lambda i: (i, 0)),
                 pl.BlockSpec((tm, D_ROPE), lambda i: (i, 0))]
    if sample:
        out_shape.append(jax.ShapeDtypeStruct((N_HEADS, rows, QEXT), F32))
        out_specs.append(pl.BlockSpec((N_HEADS, tm, QEXT), lambda i: (0, i, 0)))
    else:
        out_shape += [jax.ShapeDtypeStruct((rows, N_HEADS * HEAD_PAD), BF16),
                      jax.ShapeDtypeStruct((rows, N_HEADS * HEAD_PAD), BF16),
                      jax.ShapeDtypeStruct((N_HEADS * D_V, rows), BF16)]
        out_specs += [pl.BlockSpec((tm, N_HEADS * HEAD_PAD), lambda i: (i, 0)),
                      pl.BlockSpec((tm, N_HEADS * HEAD_PAD), lambda i: (i, 0)),
                      pl.BlockSpec((N_HEADS * D_V, tm), lambda i: (0, i))]
    return pl.pallas_call(
        functools.partial(_proj_body, sample),
        grid=(rows // tm,),
        in_specs=[
            pl.BlockSpec((tm, D_MODEL), lambda i: (i + row0, 0)),
            _const_spec((1, D_MODEL)),
            _const_spec(w["win"].shape),
            _const_spec((1, Q_LORA)),
            _const_spec((1, KV_LORA)),
            _const_spec(w["wq"].shape),
            _const_spec(w["wqs"].shape),
            pl.BlockSpec((tm, 2 * HEAD_PAD), tab_map),
            pl.BlockSpec((tm, 2 * HEAD_PAD), tab_map),
            pl.BlockSpec((tm, LANES), tab_map),
            pl.BlockSpec((tm, LANES), tab_map),
            _const_spec(wa.shape),
            _const_spec(wb.shape),
        ],
        out_specs=out_specs,
        out_shape=out_shape,
        compiler_params=_cparams(("parallel",)),
        name="mla_proj_sample" if sample else "mla_proj_prompt",
    )(x, gm.reshape(1, D_MODEL), w["win"], w["gq"], w["gkv"], w["wq"], w["wqs"],
      cq, sq, ck, sk, wa, wb)


def _flash_body(q_ref, k_ref, vt_ref, o_ref, s_buf):
    key = lax.broadcasted_iota(jnp.int32, (TQ, TQ), 0)
    qry = lax.broadcasted_iota(jnp.int32, (TQ, TQ), 1)
    items = [(qi, hh) for qi in range(SEQ // TQ) for hh in range(2)]
    ones = jnp.ones((16, SEQ), BF16)

    def scores(n):
        qi, hh = items[n]
        q0 = qi * TQ
        hs = slice(hh * HEAD_PAD, (hh + 1) * HEAD_PAD)
        q = q_ref[q0:q0 + TQ, hs]
        sd = _dot_nt(k_ref[q0:q0 + TQ, hs], q)
        sd = jnp.where(key <= qry, sd, -jnp.inf)
        s_buf[n % 2, q0:q0 + TQ, :] = sd
        m = jnp.max(sd, axis=0, keepdims=True)
        if qi > 0:
            sm = _dot_nt(k_ref[0:q0, hs], q)
            s_buf[n % 2, 0:q0, :] = sm
            m = jnp.maximum(m, jnp.max(sm, axis=0, keepdims=True))
        return m

    def values(n, m):
        qi, hh = items[n]
        kv_len = (qi + 1) * TQ
        p = jnp.exp2(s_buf[n % 2, 0:kv_len, :] - m).astype(BF16)
        vt1 = jnp.concatenate([vt_ref[hh * D_V:(hh + 1) * D_V, 0:kv_len], ones[:, 0:kv_len]], axis=0)
        ot = _dot(vt1, p)
        return ot[:D_V] / ot[D_V:D_V + 1]

    m_next = scores(0)
    ots = []
    for n, (qi, hh) in enumerate(items):
        m = m_next
        if n + 1 < len(items):
            m_next = scores(n + 1)
        ots.append(values(n, m))
        if hh == 1:
            q0 = qi * TQ
            o_ref[q0:q0 + TQ, :] = jnp.concatenate(ots, axis=0).T.astype(o_ref.dtype)
            ots = []


def flash_prompt(q, k, vt):
    return pl.pallas_call(
        _flash_body,
        grid=(BATCH, N_HEADS // 2),
        in_specs=[
            pl.BlockSpec((SEQ, 2 * HEAD_PAD), lambda b, p: (b, p)),
            pl.BlockSpec((SEQ, 2 * HEAD_PAD), lambda b, p: (b, p)),
            pl.BlockSpec((2 * D_V, SEQ), lambda b, p: (p, b)),
        ],
        out_specs=pl.BlockSpec((SEQ, 2 * D_V), lambda b, p: (b, p)),
        out_shape=jax.ShapeDtypeStruct((NP_ROWS, N_HEADS * D_V), BF16),
        scratch_shapes=[pltpu.VMEM((2, SEQ, TQ), F32)],
        compiler_params=_cparams(("parallel", "parallel")),
        name="flash_prompt",
    )(q, k, vt)


N_CHUNKS = N_PAGES // PAGES_PER_STEP
assert N_CHUNKS % 2 == 0 and N_CHUNKS * PAGES_PER_STEP == N_PAGES
SATTN_KEYS = 2048
QROWS = N_HEADS * DEC_SEQ


def _sattn_body(layer, pt_ref, q_ref, cnew_ref, knew_ref, ckv_hbm, kpet_hbm, o_ref,
                cbuf, kbuf, sem, s_buf, ck_b):
    n = PAGES_PER_STEP
    b = pl.program_id(0)

    def chunk_copies(req, c, slot):
        cps = []
        for k in range(n):
            page = pt_ref[req, c * n + k]
            rows = pl.ds(k * PAGE_SIZE, PAGE_SIZE)
            cps.append(pltpu.make_async_copy(ckv_hbm.at[layer, page], cbuf.at[slot, rows, :],
                                             sem.at[slot, 0]))
            cps.append(pltpu.make_async_copy(kpet_hbm.at[layer, page], kbuf.at[slot, :, rows],
                                             sem.at[slot, 1]))
        return cps

    def start(req, c, slot):
        for cp in chunk_copies(req, c, slot):
            cp.start()

    def wait(req, c, slot):
        for cp in chunk_copies(req, c, slot):
            cp.wait()

    @pl.when(b == 0)
    def _():
        for c in range(2):
            start(0, c, c)

    q = q_ref[...].reshape(QROWS, QEXT)
    qlat = q[:, :KV_LORA].astype(BF16)
    qpe = q[:, KV_LORA:KV_LORA + D_ROPE].astype(BF16)

    n_sb = n * PAGE_SIZE // SATTN_KEYS
    items = [(c, sb) for c in range(N_CHUNKS) for sb in range(n_sb)]

    def scores(i):
        c, sb = items[i]
        keys = slice(sb * SATTN_KEYS, (sb + 1) * SATTN_KEYS)
        if sb == 0:
            wait(b, c, c % 2)
        ck = cbuf[c % 2, keys, :].astype(BF16)
        kt = kbuf[c % 2, :, keys].astype(BF16)
        ck_b[i % 2] = ck
        s = _dot_nt(qlat, ck) + _dot(qpe, kt)
        s_buf[i % 2] = s
        if sb == n_sb - 1:
            if c + 2 < N_CHUNKS:
                start(b, c + 2, c % 2)
            else:
                @pl.when(b + 1 < DEC_BATCH)
                def _():
                    start(b + 1, c + 2 - N_CHUNKS, c % 2)
        return jnp.max(s, axis=-1, keepdims=True)

    bm_next = scores(0)

    pad = PAGE_SIZE - DEC_SEQ
    cn = jnp.concatenate([cnew_ref[...], jnp.zeros((pad, KV_LORA), F32)], axis=0).astype(BF16)
    kn = jnp.concatenate([knew_ref[...], jnp.zeros((pad, D_ROPE), F32)], axis=0).astype(BF16)
    s = _dot_nt(qlat, cn) + _dot_nt(qpe, kn)
    tok = lax.broadcasted_iota(jnp.int32, (QROWS, PAGE_SIZE), 0) % DEC_SEQ
    key = lax.broadcasted_iota(jnp.int32, (QROWS, PAGE_SIZE), 1)
    s = jnp.where(key <= tok, s, -jnp.inf)
    m = jnp.max(s, axis=-1, keepdims=True)
    p = jnp.exp2((s - m) * EXP2_SCALE)
    l = jnp.sum(p, axis=-1, keepdims=True)
    acc = _dot(p.astype(BF16), cn)

    for i in range(len(items)):
        bm = bm_next
        if i + 1 < len(items):
            bm_next = scores(i + 1)
        m_new = jnp.maximum(m, bm)
        alpha = jnp.exp2((m - m_new) * EXP2_SCALE)
        p = jnp.exp2((s_buf[i % 2] - m_new) * EXP2_SCALE)
        l = alpha * l + jnp.sum(p, axis=-1, keepdims=True)
        acc = alpha * acc + _dot(p.astype(BF16), ck_b[i % 2])
        m = m_new

    o_ref[...] = (acc / l).reshape(N_HEADS, DEC_SEQ, KV_LORA)


def attn_sample(layer, page_table, qext, ckv_all, kpe_all, cache_ckv, cache_kpet):
    n = PAGES_PER_STEP
    grid_spec = pltpu.PrefetchScalarGridSpec(
        num_scalar_prefetch=1,
        grid=(DEC_BATCH,),
        in_specs=[
            pl.BlockSpec((N_HEADS, DEC_SEQ, QEXT), lambda b, pt: (0, b, 0)),
            pl.BlockSpec((DEC_SEQ, KV_LORA), lambda b, pt: (b, 0)),
            pl.BlockSpec((DEC_SEQ, D_ROPE), lambda b, pt: (b, 0)),
            pl.BlockSpec(memory_space=pl.ANY),
            pl.BlockSpec(memory_space=pl.ANY),
        ],
        out_specs=pl.BlockSpec((N_HEADS, DEC_SEQ, KV_LORA), lambda b, pt: (0, b, 0)),
        scratch_shapes=[
            pltpu.VMEM((2, n * PAGE_SIZE, KV_LORA), F32),
            pltpu.VMEM((2, D_ROPE, n * PAGE_SIZE), F32),
            pltpu.SemaphoreType.DMA((2, 2)),
            pltpu.VMEM((2, QROWS, SATTN_KEYS), F32),
            pltpu.VMEM((2, SATTN_KEYS, KV_LORA), BF16),
        ],
    )
    return pl.pallas_call(
        functools.partial(_sattn_body, layer),
        grid_spec=grid_spec,
        out_shape=jax.ShapeDtypeStruct((N_HEADS, NS_ROWS, KV_LORA), F32),
        compiler_params=_cparams(("arbitrary",)),
        name="attn_sample",
    )(page_table, qext, ckv_all, kpe_all, cache_ckv, cache_kpet)


def _oproj_sample_body(x_ref, ol_ref, wuv_ref, wo_ref, out_ref):
    acc = x_ref[...]
    for p in range(N_HEADS // 2):
        pair = jnp.concatenate([ol_ref[2 * p], ol_ref[2 * p + 1]], axis=1).astype(BF16)
        o_pair = _dot(pair, wuv_ref[p]).astype(BF16)
        acc = acc + _dot(o_pair, wo_ref[p * 2 * D_V:(p + 1) * 2 * D_V, :])
    out_ref[...] = acc


def oproj_sample(x, o_lat, wuv_pair, wo):
    row0 = NP_ROWS // TM
    return pl.pallas_call(
        _oproj_sample_body,
        grid=(NS_ROWS // TM,),
        in_specs=[pl.BlockSpec((TM, D_MODEL), lambda i: (i + row0, 0)),
                  pl.BlockSpec((N_HEADS, TM, KV_LORA), lambda i: (0, i, 0)),
                  _const_spec(wuv_pair.shape),
                  _const_spec(wo.shape)],
        out_specs=pl.BlockSpec((TM, D_MODEL), lambda i: (i + row0, 0)),
        out_shape=jax.ShapeDtypeStruct((ROWS, D_MODEL), F32),
        input_output_aliases={0: 0},
        compiler_params=_cparams(("parallel",)),
        name="oproj_sample",
    )(x, o_lat, wuv_pair, wo)


S5_Q = DEC_SEQ
S5_BLK = LANES // GROUP_SIZE
S5_NBLK = N_GROUPS // S5_BLK
S5_HALF = S5_BLK * STATE_DIM
S5_CT = 64
S5_CHUNKS = SEQ // S5_Q


def _s5_body(n_chunks, nb, h_ref, t_ref, m_ref, n_ref, a_ref, h0_ref, y_ref, hf_ref,
             s_s, p_s, st_s):
    rows = n_chunks * nb

    @pl.when(pl.program_id(1) == 0)
    def _():
        st_s[...] = h0_ref[...]

    u = jnp.concatenate([h_ref[pl.ds(k, rows, stride=S5_Q), :].astype(BF16)
                         for k in range(S5_Q)], axis=1)
    s_s[...] = _dot(u, m_ref[...])
    are = a_ref[:, :S5_HALF]
    aim = a_ref[:, S5_HALF:]

    def step(c, h):
        r0 = pl.multiple_of(c * nb, nb)
        p_s[pl.ds(r0, nb), :] = h
        s = s_s[pl.ds(r0, nb), :]
        hre, him = h[:, :S5_HALF], h[:, S5_HALF:]
        nre = are * hre - aim * him + s[:, :S5_HALF]
        nim = are * him + aim * hre + s[:, S5_HALF:]
        return jnp.concatenate([nre, nim], axis=1)

    h = lax.fori_loop(0, n_chunks, step, st_s[...])
    st_s[...] = h
    hf_ref[...] = h
    pair = 2 * LANES
    yt = [_dot(u[:, :(c + 1) * pair], t_ref[:(c + 1) * pair, c * pair:(c + 1) * pair])
          for c in range(S5_Q // 2)]
    y = jnp.concatenate(yt, axis=1) + _dot(p_s[...].astype(BF16), n_ref[...])
    for l in range(S5_Q):
        y_ref[pl.ds(l, rows, stride=S5_Q), :] = y[:, l * LANES:(l + 1) * LANES]


def s5_scan(h, tabs, h0, n_chunks, nb):
    tbig, mbig, nbig, abig = tabs
    rows = n_chunks * nb
    tile = rows * S5_Q
    width = S5_Q * LANES

    def bspec(shape):
        return pl.BlockSpec((None,) + shape, lambda j, c: (j, 0, 0))

    return pl.pallas_call(
        functools.partial(_s5_body, n_chunks, nb),
        grid=(S5_NBLK, h.shape[0] // tile),
        in_specs=[pl.BlockSpec((tile, LANES), lambda j, c: (c, j)),
                  bspec((width, width)), bspec((width, 2 * S5_HALF)), bspec((2 * S5_HALF, width)),
                  bspec((1, 2 * S5_HALF)), bspec((nb, 2 * S5_HALF))],
        out_specs=[pl.BlockSpec((tile, LANES), lambda j, c: (c, j)), bspec((nb, 2 * S5_HALF))],
        out_shape=[jax.ShapeDtypeStruct(h.shape, F32),
                   jax.ShapeDtypeStruct((S5_NBLK, nb, 2 * S5_HALF), F32)],
        scratch_shapes=[pltpu.VMEM((rows, 2 * S5_HALF), F32), pltpu.VMEM((rows, 2 * S5_HALF), F32),
                        pltpu.VMEM((nb, 2 * S5_HALF), F32)],
        compiler_params=_cparams(("parallel", "arbitrary")),
        name="s5_scan",
    )(h, tbig, mbig, nbig, abig, h0)


def _glu_body(x_ref, h_ref, y_ref, d_ref, w_ref, o_ref):
    o_ref[...] = _glu_update(x_ref[...], h_ref[...], y_ref[...], d_ref[...], w_ref)


def glu_sample(x, h, y, d, w, layer):
    blk0 = NP_ROWS // TM
    x_spec = pl.BlockSpec((TM, D_MODEL), lambda i: (i + blk0, 0))
    hy_spec = pl.BlockSpec((TM, D_MODEL), lambda i: (i, 0))
    return pl.pallas_call(
        _glu_body,
        grid=(NS_ROWS // TM,),
        in_specs=[x_spec, hy_spec, hy_spec, _const_spec((1, D_MODEL)),
                  _layer_spec((D_MODEL, 2 * D_MODEL), layer)],
        out_specs=x_spec,
        out_shape=jax.ShapeDtypeStruct((ROWS, D_MODEL), F32),
        input_output_aliases={0: 0},
        compiler_params=_cparams(("parallel",)),
        name="glu_sample",
    )(x, h, y, d.reshape(1, D_MODEL), w)


def _prep_mla(w_in, g_q, g_kv, w_q_up, w_kv_up, w_o):
    z = lambda *s: jnp.zeros(s, F32)
    wk_pe = w_in[:, Q_LORA + KV_LORA:]
    half = D_ROPE // 2
    win = jnp.concatenate(
        [w_in[:, :Q_LORA + KV_LORA], wk_pe, z(D_MODEL, LANES - D_ROPE),
         -wk_pe[:, half:], wk_pe[:, :half], z(D_MODEL, LANES - D_ROPE)], axis=1)
    wq3 = w_q_up.reshape(Q_LORA, N_HEADS, D_NOPE + D_ROPE)
    nope, x1, x2 = wq3[..., :D_NOPE], wq3[..., D_NOPE:D_NOPE + half], wq3[..., D_NOPE + half:]
    zpad = z(Q_LORA, N_HEADS, HEAD_PAD - D_NOPE - D_ROPE)
    wq = jnp.concatenate([nope, x1, x2, zpad], axis=-1).reshape(Q_LORA, N_HEADS * HEAD_PAD)
    wqs = jnp.concatenate([z(Q_LORA, N_HEADS, D_NOPE), -x2, x1, zpad], axis=-1)
    wqs = wqs.reshape(Q_LORA, N_HEADS * HEAD_PAD)
    wkv3 = w_kv_up.reshape(KV_LORA, N_HEADS, D_NOPE + D_V)
    w_uk, w_uv = wkv3[..., :D_NOPE], wkv3[..., D_NOPE:]
    wk = jnp.concatenate([w_uk, z(KV_LORA, N_HEADS, HEAD_PAD - D_NOPE)], axis=-1)
    wk = wk.reshape(KV_LORA, N_HEADS * HEAD_PAD)
    wv = w_uv.reshape(KV_LORA, N_HEADS * D_V).T
    wuk_ext = jnp.zeros((N_HEADS, HEAD_PAD, QEXT), F32)
    wuk_ext = wuk_ext.at[:, :D_NOPE, :KV_LORA].set(jnp.transpose(w_uk, (1, 2, 0)))
    wuk_ext = wuk_ext.at[:, D_NOPE:D_NOPE + D_ROPE, KV_LORA:KV_LORA + D_ROPE].set(
        jnp.broadcast_to(jnp.eye(D_ROPE, dtype=F32), (N_HEADS, D_ROPE, D_ROPE)))
    uv = jnp.transpose(w_uv, (1, 0, 2)).reshape(N_HEADS // 2, 2, KV_LORA, D_V)
    wuv_pair = jnp.zeros((N_HEADS // 2, 2 * KV_LORA, 2 * D_V), F32)
    wuv_pair = wuv_pair.at[:, :KV_LORA, :D_V].set(uv[:, 0])
    wuv_pair = wuv_pair.at[:, KV_LORA:, D_V:].set(uv[:, 1])
    return dict(win=win.astype(BF16), gq=g_q.reshape(1, Q_LORA), gkv=g_kv.reshape(1, KV_LORA),
                wq=wq.astype(BF16), wqs=wqs.astype(BF16), wk=wk.astype(BF16), wv=wv.astype(BF16),
                wuk_ext=wuk_ext.astype(BF16), wuv_pair=wuv_pair.astype(BF16), wo=w_o.astype(BF16))


def _rope_tabs(pos, reps):
    half = D_ROPE // 2
    inv = ROPE_THETA ** (-jnp.arange(half, dtype=F32) / half)
    ang = pos.astype(F32)[:, None] * inv[None, :]
    cos, sin = jnp.cos(ang), jnp.sin(ang)
    n = pos.shape[0]
    one, zero = jnp.ones((n, D_NOPE), F32), jnp.zeros((n, D_NOPE), F32)
    zpad = jnp.zeros((n, HEAD_PAD - D_NOPE - D_ROPE), F32)
    cq = jnp.concatenate([one, cos, cos, zpad], axis=1)
    sq = jnp.concatenate([zero, sin, sin, zpad], axis=1)
    kpad = jnp.zeros((n, LANES - D_ROPE), F32)
    ck = jnp.concatenate([cos, cos, kpad], axis=1)
    sk = jnp.concatenate([sin, sin, kpad], axis=1)
    cq, sq = jnp.tile(cq, (reps, 2)), jnp.tile(sq, (reps, 2))
    return cq, sq, jnp.tile(ck, (reps, 1)), jnp.tile(sk, (reps, 1))


def _prep_s5(a_re, a_im, log_dt, b_re, b_im, c_re, c_im, q):
    hi = lax.Precision.HIGHEST
    dt = jnp.exp(log_dt)[:, None]
    dre, dim = dt * a_re, dt * a_im
    mag = jnp.exp(dre)
    abr, abi = mag * jnp.cos(dim), mag * jnp.sin(dim)
    den = a_re * a_re + a_im * a_im
    xr, xi = abr - 1.0, abi
    fr, fi = (xr * a_re + xi * a_im) / den, (xi * a_re - xr * a_im) / den
    bbr = fr[..., None] * b_re - fi[..., None] * b_im
    bbi = fr[..., None] * b_im + fi[..., None] * b_re
    j = jnp.arange(q + 1, dtype=F32)[:, None, None]
    pmag = jnp.exp(j * dre)
    pwr, pwi = pmag * jnp.cos(j * dim), pmag * jnp.sin(j * dim)
    pbr = pwr[..., None] * bbr - pwi[..., None] * bbi
    pbi = pwr[..., None] * bbi + pwi[..., None] * bbr
    kk = (jnp.einsum("gtp,jgps->jgst", c_re, pbr[:q], precision=hi)
          - jnp.einsum("gtp,jgps->jgst", c_im, pbi[:q], precision=hi))
    g = a_re.shape[0]
    taps = jnp.transpose(kk, (1, 2, 0, 3)).reshape(g, GROUP_SIZE, q * GROUP_SIZE)
    t = jnp.stack([jnp.pad(taps[..., :(q - k) * GROUP_SIZE], ((0, 0), (0, 0), (k * GROUP_SIZE, 0)))
                   for k in range(q)], axis=1).reshape(g, q * GROUP_SIZE, q * GROUP_SIZE)
    mre = jnp.transpose(pbr[:q][::-1], (1, 0, 3, 2)).reshape(g, q * GROUP_SIZE, STATE_DIM)
    mim = jnp.transpose(pbi[:q][::-1], (1, 0, 3, 2)).reshape(g, q * GROUP_SIZE, STATE_DIM)
    ncr = pwr[1:, :, None, :] * c_re[None] - pwi[1:, :, None, :] * c_im[None]
    nci = pwr[1:, :, None, :] * c_im[None] + pwi[1:, :, None, :] * c_re[None]
    nre = jnp.transpose(ncr, (1, 3, 0, 2)).reshape(g, STATE_DIM, q * GROUP_SIZE)
    nim = jnp.transpose(nci, (1, 3, 0, 2)).reshape(g, STATE_DIM, q * GROUP_SIZE)
    are, aim = pwr[q][:, None, :], pwi[q][:, None, :]
    return t, mre, mim, nre, nim, are, aim


def _s5_tables_body(at_ref, am_ref, an_ref, et_ref, em_ref, t_ref, m_ref, n_ref):
    width = S5_Q * LANES
    row = lax.broadcasted_iota(jnp.int32, (width, width), 0)
    col = lax.broadcasted_iota(jnp.int32, (width, width), 1)

    def expand(a_ref, e_ref, row_shift, col_shift):
        full = _dot(a_ref[...].astype(BF16), e_ref[...])
        keep = ((row >> row_shift) & (S5_BLK - 1)) == ((col >> col_shift) & (S5_BLK - 1))
        return jnp.where(keep, full, 0.0).astype(BF16)

    gs, sd = GROUP_SIZE.bit_length() - 1, STATE_DIM.bit_length() - 1
    t_ref[...] = expand(at_ref, et_ref, gs, gs)
    m_ref[...] = expand(am_ref, em_ref, gs, sd)
    n_ref[...] = expand(an_ref, et_ref, sd, gs)


def _s5_block_tables(p):
    t, mre, mim, nre, nim, are, aim = _prep_s5(*p, S5_Q)
    q = S5_Q
    width = q * LANES
    assert 2 * S5_HALF == width and q * GROUP_SIZE == LANES and 2 * STATE_DIM == LANES

    def rows_kgs(a):
        a5 = a.reshape(S5_NBLK, S5_BLK, q, GROUP_SIZE, LANES)
        return jnp.transpose(a5, (0, 2, 1, 3, 4)).reshape(S5_NBLK, width, LANES)

    at = rows_kgs(t)
    am = rows_kgs(jnp.concatenate([mre, mim], axis=-1))
    an = jnp.stack([nre.reshape(S5_NBLK, S5_BLK, STATE_DIM, LANES),
                    -nim.reshape(S5_NBLK, S5_BLK, STATE_DIM, LANES)], axis=1)
    an = an.reshape(S5_NBLK, width, LANES)
    r = jnp.arange(LANES)[:, None]
    c = jnp.arange(width)[None, :]
    et = ((r // GROUP_SIZE == c // LANES) & (r % GROUP_SIZE == c % GROUP_SIZE)).astype(BF16)
    em = ((r // STATE_DIM == c // S5_HALF) & (r % STATE_DIM == c % STATE_DIM)).astype(BF16)
    cspec = pl.BlockSpec((None, width, LANES), lambda j: (j, 0, 0))
    ospec = pl.BlockSpec((None, width, width), lambda j: (j, 0, 0))
    big = jax.ShapeDtypeStruct((S5_NBLK, width, width), BF16)
    tbig, mbig, nbig = pl.pallas_call(
        _s5_tables_body,
        grid=(S5_NBLK,),
        in_specs=[cspec, cspec, cspec, _const_spec((LANES, width)), _const_spec((LANES, width))],
        out_specs=[ospec, ospec, ospec],
        out_shape=[big, big, big],
        compiler_params=_cparams(("parallel",)),
        name="s5_tables",
    )(at, am, an, et, em)
    abig = jnp.concatenate([are.reshape(S5_NBLK, 1, S5_HALF), aim.reshape(S5_NBLK, 1, S5_HALF)], axis=2)
    return tbig, mbig, nbig, abig


def _state_to_blocks(re, im):
    nb = re.shape[0]
    f = lambda a: jnp.transpose(a.reshape(nb, S5_NBLK, S5_HALF), (1, 0, 2))
    return jnp.concatenate([f(re), f(im)], axis=2)


def _blocks_to_state(hf):
    nb = hf.shape[1]
    f = lambda a: jnp.transpose(a, (1, 0, 2)).reshape(nb, N_GROUPS, STATE_DIM)
    return f(hf[..., :S5_HALF]), f(hf[..., S5_HALF:])


def _s5_mixer(hp, hs, p, h0re, h0im):
    tabs = _s5_block_tables(p)
    zero = jnp.zeros((S5_NBLK, BATCH, 2 * S5_HALF), F32)
    yp, hfp = s5_scan(hp, tabs, zero, S5_CT, BATCH)
    ys, hfs = s5_scan(hs, tabs, _state_to_blocks(h0re, h0im), 1, DEC_BATCH)
    return (yp, ys) + _blocks_to_state(hfp) + _blocks_to_state(hfs)


def kernel(x_prompt, x_sample, cache_ckv, cache_kpe, state_ssm_re, state_ssm_im, page_table, norm_mix, norm_mlp, norm_final, mla_w_in, mla_q_norm, mla_kv_norm, mla_w_q_up, mla_w_kv_up, mla_w_o, ssm_a_re, ssm_a_im, ssm_log_dt, ssm_b_re, ssm_b_im, ssm_c_re, ssm_c_im, ssm_d, ssm_w_glu, mlp_w1, mlp_w2):
    x = jnp.concatenate([x_prompt.reshape(NP_ROWS, D_MODEL), x_sample.reshape(NS_ROWS, D_MODEL)], axis=0)
    tabs_p = _rope_tabs(jnp.arange(SEQ, dtype=jnp.int32), 1)
    tabs_s = _rope_tabs(PAST_LEN + jnp.arange(DEC_SEQ, dtype=jnp.int32), TM_PROJ // DEC_SEQ)
    cache_kpet = jnp.swapaxes(cache_kpe, 2, 3)
    w1b, w2b, wglu_b = mlp_w1.astype(BF16), mlp_w2.astype(BF16), ssm_w_glu.astype(BF16)
    p_ckv, p_kpe, p_re, p_im = [], [], [], []
    s_ckv, s_kpe, s_re, s_im = [], [], [], []
    for i in range(DEPTH):
        j = i // 2
        if i % 2 == 0:
            w = _prep_mla(mla_w_in[j], mla_q_norm[j], mla_kv_norm[j], mla_w_q_up[j],
                          mla_w_kv_up[j], mla_w_o[j])
            ckv_p, kpe_p, q_p, k_p, v_p = mla_proj(x, False, norm_mix[i], w, tabs_p)
            ckv_s, kpe_s, qext = mla_proj(x, True, norm_mix[i], w, tabs_s)
            o_p = flash_prompt(q_p, k_p, v_p)
            o_lat = attn_sample(j, page_table, qext, ckv_s, kpe_s, cache_ckv, cache_kpet)
            pre, pre_args = "oproj", (o_p, w["wo"])
            sample_mixer = functools.partial(oproj_sample, o_lat=o_lat, wuv_pair=w["wuv_pair"],
                                             wo=w["wo"])
            p_ckv.append(ckv_p.reshape(BATCH, SEQ, KV_LORA))
            p_kpe.append(kpe_p.reshape(BATCH, SEQ, D_ROPE))
            s_ckv.append(ckv_s.reshape(DEC_BATCH, DEC_SEQ, KV_LORA))
            s_kpe.append(kpe_s.reshape(DEC_BATCH, DEC_SEQ, D_ROPE))
        else:
            p = (ssm_a_re[j], ssm_a_im[j], ssm_log_dt[j], ssm_b_re[j], ssm_b_im[j],
                 ssm_c_re[j], ssm_c_im[j])
            yp, ys, pre_, pim, sre, sim = _s5_mixer(hp, hs, p, state_ssm_re[j], state_ssm_im[j])
            p_re.append(pre_); p_im.append(pim); s_re.append(sre); s_im.append(sim)
            pre, pre_args = "glu", (hp, yp, ssm_d[j], wglu_b, j)
            sample_mixer = functools.partial(glu_sample, h=hs, y=ys, d=ssm_d[j], w=wglu_b, layer=j)
        args = (i, norm_mlp[i], w1b, w2b)
        if i == DEPTH - 1:
            y_prompt = mlp(x, *args, False, pre, pre_args, "final", norm_final)
            y_sample = mlp(sample_mixer(x), *args, True, None, (), "final", norm_final)
        elif i % 2 == 0:
            x, hp = mlp(x, *args, False, pre, pre_args, "norm", norm_mix[i + 1])
            x, hs = mlp(sample_mixer(x), *args, True, None, (), "norm", norm_mix[i + 1])
            hp = hp.reshape(NP_ROWS, D_MODEL)
        else:
            x = mlp(x, *args, False, pre, pre_args)
            x = mlp(sample_mixer(x), *args, True)
    y_prompt = y_prompt.reshape(BATCH, SEQ, D_MODEL)
    y_sample = y_sample.reshape(DEC_BATCH, DEC_SEQ, D_MODEL)
    return (y_prompt, y_sample,
            jnp.stack(p_ckv), jnp.stack(p_kpe), jnp.stack(p_re), jnp.stack(p_im),
            jnp.stack(s_ckv), jnp.stack(s_kpe), jnp.stack(s_re), jnp.stack(s_im))
```

```python
import functools
import math

import jax
import jax.numpy as jnp
from jax import lax
from jax.experimental import pallas as pl
from jax.experimental.pallas import tpu as pltpu

F32 = jnp.float32
BF16 = jnp.bfloat16

D_MODEL = 1024
BATCH = 8
SEQ = 2048
DEPTH = 4
DEC_BATCH = 128
DEC_SEQ = 8
PAST_LEN = 8192
PAGE_SIZE = 128
N_PAGES = PAST_LEN // PAGE_SIZE
N_HEADS = 16
D_NOPE = 64
D_ROPE = 32
D_V = 64
Q_LORA = 512
KV_LORA = 256
ROPE_THETA = 10000.0
GROUP_SIZE = 16
N_GROUPS = D_MODEL // GROUP_SIZE
STATE_DIM = 64
D_FF = 4 * D_MODEL
EPS = 1e-6
SCALE = (D_NOPE + D_ROPE) ** -0.5
EXP2_SCALE = SCALE * math.log2(math.e)

NP_ROWS = BATCH * SEQ
NS_ROWS = DEC_BATCH * DEC_SEQ
ROWS = NP_ROWS + NS_ROWS

LANES = 128
HEAD_PAD = 128
QEXT = KV_LORA + LANES
VMEM_LIMIT = 52 * 1024 * 1024

TM = 512
TM_PROJ = 256
TM_PROJ_PROMPT = 512
TQ = 256
PAGES_PER_STEP = 32


def _cparams(sem):
    return pltpu.CompilerParams(dimension_semantics=sem, vmem_limit_bytes=VMEM_LIMIT)


def _rms(x, g):
    return x * lax.rsqrt(jnp.mean(x * x, axis=-1, keepdims=True) + EPS) * g


def _dot(a, b):
    return jnp.dot(a, b, preferred_element_type=F32)


def _dot_nt(a, b):
    return lax.dot_general(a, b, (((1,), (1,)), ((), ())), preferred_element_type=F32)


def _const_spec(shape):
    nd = len(shape)
    return pl.BlockSpec(shape, lambda *_: (0,) * nd)


FF_CHUNK = 1024


def _glu_update(x, h, y, d, w_ref):
    gl = jax.nn.gelu(y + d * h).astype(BF16)
    ab = _dot(gl, w_ref[...])
    return x + ab[:, :D_MODEL] * jax.nn.sigmoid(ab[:, D_MODEL:])


def _mlp_body(pre, post, x_ref, *refs):
    x = x_ref[...]
    if pre == "oproj":
        a_ref, wa_ref, *refs = refs
        x = x + _dot(a_ref[...], wa_ref[...])
    elif pre == "glu":
        h_ref, y_ref, d_ref, wg_ref, *refs = refs
        x = _glu_update(x, h_ref[...].reshape(x.shape), y_ref[...].reshape(x.shape), d_ref[...], wg_ref)
    g_ref, w1_ref, w2_ref, *refs = refs
    h = _rms(x, g_ref[...]).astype(BF16)
    acc = x
    for c in range(D_FF // FF_CHUNK):
        sl = slice(c * FF_CHUNK, (c + 1) * FF_CHUNK)
        z = jnp.maximum(_dot(h, w1_ref[:, sl]), 0.0)
        acc = acc + _dot((z * z).astype(BF16), w2_ref[sl, :])
    if post is None:
        (o_ref,) = refs
        o_ref[...] = acc
    elif post == "norm":
        gn_ref, o_ref, hn_ref = refs
        o_ref[...] = acc
        hn_ref[...] = _rms(acc, gn_ref[...]).reshape(hn_ref.shape)
    else:
        gn_ref, y_ref = refs
        y_ref[...] = _rms(acc, gn_ref[...])


def _layer_spec(shape, layer):
    nd = len(shape)
    return pl.BlockSpec((None,) + shape, lambda *_: (layer,) + (0,) * nd, pipeline_mode=pl.Buffered(1))


def mlp(x, layer, g, w1, w2, sample, pre=None, pre_args=(), post=None, g_post=None):
    if sample:
        blk0, grid, rows = NP_ROWS // TM, (NS_ROWS // TM,), NS_ROWS
        x_spec = pl.BlockSpec((TM, D_MODEL), lambda i: (i + blk0, 0))
        own_spec = pl.BlockSpec((TM, D_MODEL), lambda i: (i, 0))
        norm_spec, norm_shape = own_spec, (NS_ROWS, D_MODEL)
    else:
        nci = S5_CHUNKS // S5_CT
        grid, rows = (BATCH, nci), NP_ROWS
        x_spec = pl.BlockSpec((TM, D_MODEL), lambda b, c: (b * nci + c, 0))
        own_spec = x_spec
        norm_spec = pl.BlockSpec((S5_CT, None, S5_Q, D_MODEL), lambda b, c: (c, b, 0, 0))
        norm_shape = (S5_CHUNKS, BATCH, S5_Q, D_MODEL)
    x_out = jax.ShapeDtypeStruct((ROWS, D_MODEL), F32)
    in_place = {0: 0}
    post_args, post_specs = (), []
    if post is None:
        out_specs, out_shape, aliases = x_spec, x_out, in_place
    else:
        post_args, post_specs = (g_post.reshape(1, D_MODEL),), [_const_spec((1, D_MODEL))]
        if post == "norm":
            out_specs, aliases = [x_spec, norm_spec], in_place
            out_shape = [x_out, jax.ShapeDtypeStruct(norm_shape, F32)]
        else:
            out_specs, out_shape, aliases = own_spec, jax.ShapeDtypeStruct((rows, D_MODEL), F32), {}
    pre_specs = []
    if pre == "oproj":
        o, wo = pre_args
        pre_specs = [pl.BlockSpec((TM, N_HEADS * D_V), lambda b, c: (b * nci + c, 0)),
                     pl.BlockSpec(wo.shape, lambda *_: (0, 0), pipeline_mode=pl.Buffered(1))]
    elif pre == "glu":
        h, y, d, wg, glu_layer = pre_args
        hy_spec = pl.BlockSpec((S5_CT, None, S5_Q, D_MODEL), lambda b, c: (c, b, 0, 0))
        pre_args = (h.reshape(S5_CHUNKS, BATCH, S5_Q, D_MODEL), y.reshape(S5_CHUNKS, BATCH, S5_Q, D_MODEL),
                    d.reshape(1, D_MODEL), wg)
        pre_specs = [hy_spec, hy_spec, _const_spec((1, D_MODEL)),
                     _layer_spec((D_MODEL, 2 * D_MODEL), glu_layer)]
    return pl.pallas_call(
        functools.partial(_mlp_body, pre, post),
        grid=grid,
        in_specs=[x_spec] + pre_specs + [_const_spec((1, D_MODEL)),
                                         _layer_spec((D_MODEL, D_FF), layer),
                                         _layer_spec((D_FF, D_MODEL), layer)] + post_specs,
        out_specs=out_specs,
        out_shape=out_shape,
        input_output_aliases=aliases,
        compiler_params=_cparams(("parallel",) * len(grid)),
        name="mlp_sample" if sample else ("mlp_" + (pre or "plain")),
    )(x, *pre_args, g.reshape(1, D_MODEL), w1, w2, *post_args)


def _proj_body(sample, x_ref, gm_ref, win_ref, gq_ref, gkv_ref, wq_ref, wqs_ref,
               cq_ref, sq_ref, ck_ref, sk_ref, wa_ref, wb_ref, *outs):
    h = _rms(x_ref[...], gm_ref[...]).astype(BF16)
    proj = _dot(h, win_ref[...])
    cq = _rms(proj[:, :Q_LORA], gq_ref[...]).astype(BF16)
    ckv = _rms(proj[:, Q_LORA:Q_LORA + KV_LORA], gkv_ref[...])
    kpe = (proj[:, 768:896] * ck_ref[...] + proj[:, 896:1024] * sk_ref[...])
    ckv_ref, kpe_ref = outs[0], outs[1]
    ckv_ref[...] = ckv
    kpe_ref[...] = kpe[:, :D_ROPE]
    cq_tab = cq_ref[...]
    sq_tab = sq_ref[...]
    for p in range(N_HEADS // 2):
        sl = slice(p * 2 * HEAD_PAD, (p + 1) * 2 * HEAD_PAD)
        qp = _dot(cq, wq_ref[:, sl]) * cq_tab + _dot(cq, wqs_ref[:, sl]) * sq_tab
        qp = qp.astype(BF16)
        if sample:
            qext_ref = outs[2]
            for hh in range(2):
                head = 2 * p + hh
                qh = qp[:, hh * HEAD_PAD:(hh + 1) * HEAD_PAD]
                qext_ref[head] = _dot(qh, wa_ref[head])
        else:
            outs[2][:, sl] = qp
    if not sample:
        k_ref, vt_ref = outs[3], outs[4]
        ckv_b = ckv.astype(BF16)
        kpl = pltpu.roll(kpe, D_NOPE, axis=1)
        kpl2 = jnp.concatenate([kpl, kpl], axis=1)
        for p in range(N_HEADS // 2):
            sl = slice(p * 2 * HEAD_PAD, (p + 1) * 2 * HEAD_PAD)
            k_ref[:, sl] = ((_dot(ckv_b, wa_ref[:, sl]) + kpl2) * EXP2_SCALE).astype(BF16)
        vt_ref[...] = _dot_nt(wb_ref[...], ckv_b).astype(BF16)


def mla_proj(x, sample, gm, w, tabs):
    if sample:
        tm = TM_PROJ
        rows, row0, tab_map = NS_ROWS, NP_ROWS // tm, (lambda i: (0, 0))
    else:
        tm = TM_PROJ_PROMPT
        rows, row0, tab_map = NP_ROWS, 0, (lambda i: (i % (SEQ // tm), 0))
    cq, sq, ck, sk = tabs
    wa = w["wuk_ext"] if sample else w["wk"]
    wb = w["wv"]
    out_shape = [jax.ShapeDtypeStruct((rows, KV_LORA), F32),
                 jax.ShapeDtypeStruct((rows, D_ROPE), F32)]
    out_specs = [pl.BlockSpec((tm, KV_LORA), lambda i: (i, 0)),
                 pl.BlockSpec((tm, D_ROPE), lambda i: (i, 0))]
    if sample:
        out_shape.append(jax.ShapeDtypeStruct((N_HEADS, rows, QEXT), F32))
        out_specs.append(pl.BlockSpec((N_HEADS, tm, QEXT), lambda i: (0, i, 0)))
    else:
        out_shape += [jax.ShapeDtypeStruct((rows, N_HEADS * HEAD_PAD), BF16),
                      jax.ShapeDtypeStruct((rows, N_HEADS * HEAD_PAD), BF16),
                      jax.ShapeDtypeStruct((N_HEADS * D_V, rows), BF16)]
        out_specs += [pl.BlockSpec((tm, N_HEADS * HEAD_PAD), lambda i: (i, 0)),
                      pl.BlockSpec((tm, N_HEADS * HEAD_PAD), lambda i: (i, 0)),
                      pl.BlockSpec((N_HEADS * D_V, tm), lambda i: (0, i))]
    return pl.pallas_call(
        functools.partial(_proj_body, sample),
        grid=(rows // tm,),
        in_specs=[
            pl.BlockSpec((tm, D_MODEL), lambda i: (i + row0, 0)),
            _const_spec((1, D_MODEL)),
            _const_spec(w["win"].shape),
            _const_spec((1, Q_LORA)),
            _const_spec((1, KV_LORA)),
            _const_spec(w["wq"].shape),
            _const_spec(w["wqs"].shape),
            pl.BlockSpec((tm, 2 * HEAD_PAD), tab_map),
            pl.BlockSpec((tm, 2 * HEAD_PAD), tab_map),
            pl.BlockSpec((tm, LANES), tab_map),
            pl.BlockSpec((tm, LANES), tab_map),
            _const_spec(wa.shape),
            _const_spec(wb.shape),
        ],
        out_specs=out_specs,
        out_shape=out_shape,
        compiler_params=_cparams(("parallel",)),
        name="mla_proj_sample" if sample else "mla_proj_prompt",
    )(x, gm.reshape(1, D_MODEL), w["win"], w["gq"], w["gkv"], w["wq"], w["wqs"],
      cq, sq, ck, sk, wa, wb)


def _flash_body(q_ref, k_ref, vt_ref, o_ref, s_buf):
    key = lax.broadcasted_iota(jnp.int32, (TQ, TQ), 0)
    qry = lax.broadcasted_iota(jnp.int32, (TQ, TQ), 1)
    items = [(qi, hh) for qi in range(SEQ // TQ) for hh in range(2)]
    ones = jnp.ones((16, SEQ), BF16)

    def scores(n):
        qi, hh = items[n]
        q0 = qi * TQ
        hs = slice(hh * HEAD_PAD, (hh + 1) * HEAD_PAD)
        q = q_ref[q0:q0 + TQ, hs]
        sd = _dot_nt(k_ref[q0:q0 + TQ, hs], q)
        sd = jnp.where(key <= qry, sd, -jnp.inf)
        s_buf[n % 2, q0:q0 + TQ, :] = sd
        m = jnp.max(sd, axis=0, keepdims=True)
        if qi > 0:
            sm = _dot_nt(k_ref[0:q0, hs], q)
            s_buf[n % 2, 0:q0, :] = sm
            m = jnp.maximum(m, jnp.max(sm, axis=0, keepdims=True))
        return m

    def values(n, m):
        qi, hh = items[n]
        kv_len = (qi + 1) * TQ
        p = jnp.exp2(s_buf[n % 2, 0:kv_len, :] - m).astype(BF16)
        vt1 = jnp.concatenate([vt_ref[hh * D_V:(hh + 1) * D_V, 0:kv_len], ones[:, 0:kv_len]], axis=0)
        ot = _dot(vt1, p)
        return ot[:D_V] / ot[D_V:D_V + 1]

    m_next = scores(0)
    ots = []
    for n, (qi, hh) in enumerate(items):
        m = m_next
        if n + 1 < len(items):
            m_next = scores(n + 1)
        ots.append(values(n, m))
        if hh == 1:
            q0 = qi * TQ
            o_ref[q0:q0 + TQ, :] = jnp.concatenate(ots, axis=0).T.astype(o_ref.dtype)
            ots = []


def flash_prompt(q, k, vt):
    return pl.pallas_call(
        _flash_body,
        grid=(BATCH, N_HEADS // 2),
        in_specs=[
            pl.BlockSpec((SEQ, 2 * HEAD_PAD), lambda b, p: (b, p)),
            pl.BlockSpec((SEQ, 2 * HEAD_PAD), lambda b, p: (b, p)),
            pl.BlockSpec((2 * D_V, SEQ), lambda b, p: (p, b)),
        ],
        out_specs=pl.BlockSpec((SEQ, 2 * D_V), lambda b, p: (b, p)),
        out_shape=jax.ShapeDtypeStruct((NP_ROWS, N_HEADS * D_V), BF16),
        scratch_shapes=[pltpu.VMEM((2, SEQ, TQ), F32)],
        compiler_params=_cparams(("parallel", "parallel")),
        name="flash_prompt",
    )(q, k, vt)


N_CHUNKS = N_PAGES // PAGES_PER_STEP
assert N_CHUNKS % 2 == 0 and N_CHUNKS * PAGES_PER_STEP == N_PAGES
SATTN_KEYS = 2048
QROWS = N_HEADS * DEC_SEQ


def _sattn_body(layer, pt_ref, q_ref, cnew_ref, knew_ref, ckv_hbm, kpet_hbm, o_ref,
                cbuf, kbuf, sem, s_buf, ck_b):
    n = PAGES_PER_STEP
    b = pl.program_id(0)

    def chunk_copies(req, c, slot):
        cps = []
        for k in range(n):
            page = pt_ref[req, c * n + k]
            rows = pl.ds(k * PAGE_SIZE, PAGE_SIZE)
            cps.append(pltpu.make_async_copy(ckv_hbm.at[layer, page], cbuf.at[slot, rows, :],
                                             sem.at[slot, 0]))
            cps.append(pltpu.make_async_copy(kpet_hbm.at[layer, page], kbuf.at[slot, :, rows],
                                             sem.at[slot, 1]))
        return cps

    def start(req, c, slot):
        for n_cp, cp in enumerate(chunk_copies(req, c, slot)):
            cp.start(priority=(n_cp // 2) % 2)

    def wait(req, c, slot):
        for cp in chunk_copies(req, c, slot):
            cp.wait()

    @pl.when(b == 0)
    def _():
        for c in range(2):
            start(0, c, c)

    q = q_ref[...].reshape(QROWS, QEXT)
    qlat = q[:, :KV_LORA].astype(BF16)
    qpe = q[:, KV_LORA:KV_LORA + D_ROPE].astype(BF16)

    n_sb = n * PAGE_SIZE // SATTN_KEYS
    items = [(c, sb) for c in range(N_CHUNKS) for sb in range(n_sb)]

    def scores(i):
        c, sb = items[i]
        keys = slice(sb * SATTN_KEYS, (sb + 1) * SATTN_KEYS)
        if sb == 0:
            wait(b, c, c % 2)
        ck = cbuf[c % 2, keys, :].astype(BF16)
        kt = kbuf[c % 2, :, keys].astype(BF16)
        ck_b[i % 2] = ck
        s = _dot_nt(qlat, ck) + _dot(qpe, kt)
        s_buf[i % 2] = s
        if sb == n_sb - 1:
            if c + 2 < N_CHUNKS:
                start(b, c + 2, c % 2)
            else:
                @pl.when(b + 1 < DEC_BATCH)
                def _():
                    start(b + 1, c + 2 - N_CHUNKS, c % 2)
        return jnp.max(s, axis=-1, keepdims=True)

    bm_next = scores(0)

    pad = PAGE_SIZE - DEC_SEQ
    cn = jnp.concatenate([cnew_ref[...], jnp.zeros((pad, KV_LORA), F32)], axis=0).astype(BF16)
    kn = jnp.concatenate([knew_ref[...], jnp.zeros((pad, D_ROPE), F32)], axis=0).astype(BF16)
    s = _dot_nt(qlat, cn) + _dot_nt(qpe, kn)
    tok = lax.broadcasted_iota(jnp.int32, (QROWS, PAGE_SIZE), 0) % DEC_SEQ
    key = lax.broadcasted_iota(jnp.int32, (QROWS, PAGE_SIZE), 1)
    s = jnp.where(key <= tok, s, -jnp.inf)
    m = jnp.max(s, axis=-1, keepdims=True)
    p = jnp.exp2((s - m) * EXP2_SCALE)
    l = jnp.sum(p, axis=-1, keepdims=True)
    acc = _dot(p.astype(BF16), cn)

    for i in range(len(items)):
        bm = bm_next
        if i + 1 < len(items):
            bm_next = scores(i + 1)
        m_new = jnp.maximum(m, bm)
        alpha = jnp.exp2((m - m_new) * EXP2_SCALE)
        p = jnp.exp2((s_buf[i % 2] - m_new) * EXP2_SCALE)
        l = alpha * l + jnp.sum(p, axis=-1, keepdims=True)
        acc = alpha * acc + _dot(p.astype(BF16), ck_b[i % 2])
        m = m_new

    o_ref[...] = (acc / l).reshape(N_HEADS, DEC_SEQ, KV_LORA)


def attn_sample(layer, page_table, qext, ckv_all, kpe_all, cache_ckv, cache_kpet):
    n = PAGES_PER_STEP
    grid_spec = pltpu.PrefetchScalarGridSpec(
        num_scalar_prefetch=1,
        grid=(DEC_BATCH,),
        in_specs=[
            pl.BlockSpec((N_HEADS, DEC_SEQ, QEXT), lambda b, pt: (0, b, 0)),
            pl.BlockSpec((DEC_SEQ, KV_LORA), lambda b, pt: (b, 0)),
            pl.BlockSpec((DEC_SEQ, D_ROPE), lambda b, pt: (b, 0)),
            pl.BlockSpec(memory_space=pl.ANY),
            pl.BlockSpec(memory_space=pl.ANY),
        ],
        out_specs=pl.BlockSpec((N_HEADS, DEC_SEQ, KV_LORA), lambda b, pt: (0, b, 0)),
        scratch_shapes=[
            pltpu.VMEM((2, n * PAGE_SIZE, KV_LORA), F32),
            pltpu.VMEM((2, D_ROPE, n * PAGE_SIZE), F32),
            pltpu.SemaphoreType.DMA((2, 2)),
            pltpu.VMEM((2, QROWS, SATTN_KEYS), F32),
            pltpu.VMEM((2, SATTN_KEYS, KV_LORA), BF16),
        ],
    )
    return pl.pallas_call(
        functools.partial(_sattn_body, layer),
        grid_spec=grid_spec,
        out_shape=jax.ShapeDtypeStruct((N_HEADS, NS_ROWS, KV_LORA), F32),
        compiler_params=_cparams(("arbitrary",)),
        name="attn_sample",
    )(page_table, qext, ckv_all, kpe_all, cache_ckv, cache_kpet)


def _oproj_sample_body(x_ref, ol_ref, wuv_ref, wo_ref, out_ref):
    acc = x_ref[...]
    for p in range(N_HEADS // 2):
        pair = jnp.concatenate([ol_ref[2 * p], ol_ref[2 * p + 1]], axis=1).astype(BF16)
        o_pair = _dot(pair, wuv_ref[p]).astype(BF16)
        acc = acc + _dot(o_pair, wo_ref[p * 2 * D_V:(p + 1) * 2 * D_V, :])
    out_ref[...] = acc


def oproj_sample(x, o_lat, wuv_pair, wo):
    row0 = NP_ROWS // TM
    return pl.pallas_call(
        _oproj_sample_body,
        grid=(NS_ROWS // TM,),
        in_specs=[pl.BlockSpec((TM, D_MODEL), lambda i: (i + row0, 0)),
                  pl.BlockSpec((N_HEADS, TM, KV_LORA), lambda i: (0, i, 0)),
                  _const_spec(wuv_pair.shape),
                  _const_spec(wo.shape)],
        out_specs=pl.BlockSpec((TM, D_MODEL), lambda i: (i + row0, 0)),
        out_shape=jax.ShapeDtypeStruct((ROWS, D_MODEL), F32),
        input_output_aliases={0: 0},
        compiler_params=_cparams(("parallel",)),
        name="oproj_sample",
    )(x, o_lat, wuv_pair, wo)


S5_Q = DEC_SEQ
S5_BLK = LANES // GROUP_SIZE
S5_NBLK = N_GROUPS // S5_BLK
S5_HALF = S5_BLK * STATE_DIM
S5_CT = 64
S5_CHUNKS = SEQ // S5_Q


def _s5_body(n_chunks, nb, h_ref, t_ref, m_ref, n_ref, a_ref, h0_ref, y_ref, hf_ref,
             s_s, p_s, st_s):
    rows = n_chunks * nb

    @pl.when(pl.program_id(1) == 0)
    def _():
        st_s[...] = h0_ref[...]

    u = jnp.concatenate([h_ref[pl.ds(k, rows, stride=S5_Q), :].astype(BF16)
                         for k in range(S5_Q)], axis=1)
    s_s[...] = _dot(u, m_ref[...])
    are = a_ref[:, :S5_HALF]
    aim = a_ref[:, S5_HALF:]

    def step(c, h):
        r0 = pl.multiple_of(c * nb, nb)
        p_s[pl.ds(r0, nb), :] = h
        s = s_s[pl.ds(r0, nb), :]
        hre, him = h[:, :S5_HALF], h[:, S5_HALF:]
        nre = are * hre - aim * him + s[:, :S5_HALF]
        nim = are * him + aim * hre + s[:, S5_HALF:]
        return jnp.concatenate([nre, nim], axis=1)

    h = lax.fori_loop(0, n_chunks, step, st_s[...])
    st_s[...] = h
    hf_ref[...] = h
    pair = 2 * LANES
    yt = [_dot(u[:, :(c + 1) * pair], t_ref[:(c + 1) * pair, c * pair:(c + 1) * pair])
          for c in range(S5_Q // 2)]
    y = jnp.concatenate(yt, axis=1) + _dot(p_s[...].astype(BF16), n_ref[...])
    for l in range(S5_Q):
        y_ref[pl.ds(l, rows, stride=S5_Q), :] = y[:, l * LANES:(l + 1) * LANES]


def s5_scan(h, tabs, h0, n_chunks, nb):
    tbig, mbig, nbig, abig = tabs
    rows = n_chunks * nb
    tile = rows * S5_Q
    width = S5_Q * LANES

    def bspec(shape):
        return pl.BlockSpec((None,) + shape, lambda j, c: (j, 0, 0))

    return pl.pallas_call(
        functools.partial(_s5_body, n_chunks, nb),
        grid=(S5_NBLK, h.shape[0] // tile),
        in_specs=[pl.BlockSpec((tile, LANES), lambda j, c: (c, j)),
                  bspec((width, width)), bspec((width, 2 * S5_HALF)), bspec((2 * S5_HALF, width)),
                  bspec((1, 2 * S5_HALF)), bspec((nb, 2 * S5_HALF))],
        out_specs=[pl.BlockSpec((tile, LANES), lambda j, c: (c, j)), bspec((nb, 2 * S5_HALF))],
        out_shape=[jax.ShapeDtypeStruct(h.shape, F32),
                   jax.ShapeDtypeStruct((S5_NBLK, nb, 2 * S5_HALF), F32)],
        scratch_shapes=[pltpu.VMEM((rows, 2 * S5_HALF), F32), pltpu.VMEM((rows, 2 * S5_HALF), F32),
                        pltpu.VMEM((nb, 2 * S5_HALF), F32)],
        compiler_params=_cparams(("parallel", "arbitrary")),
        name="s5_scan",
    )(h, tbig, mbig, nbig, abig, h0)


def _glu_body(x_ref, h_ref, y_ref, d_ref, w_ref, o_ref):
    o_ref[...] = _glu_update(x_ref[...], h_ref[...], y_ref[...], d_ref[...], w_ref)


def glu_sample(x, h, y, d, w, layer):
    blk0 = NP_ROWS // TM
    x_spec = pl.BlockSpec((TM, D_MODEL), lambda i: (i + blk0, 0))
    hy_spec = pl.BlockSpec((TM, D_MODEL), lambda i: (i, 0))
    return pl.pallas_call(
        _glu_body,
        grid=(NS_ROWS // TM,),
        in_specs=[x_spec, hy_spec, hy_spec, _const_spec((1, D_MODEL)),
                  _layer_spec((D_MODEL, 2 * D_MODEL), layer)],
        out_specs=x_spec,
        out_shape=jax.ShapeDtypeStruct((ROWS, D_MODEL), F32),
        input_output_aliases={0: 0},
        compiler_params=_cparams(("parallel",)),
        name="glu_sample",
    )(x, h, y, d.reshape(1, D_MODEL), w)


def _prep_mla(w_in, g_q, g_kv, w_q_up, w_kv_up, w_o):
    z = lambda *s: jnp.zeros(s, F32)
    wk_pe = w_in[:, Q_LORA + KV_LORA:]
    half = D_ROPE // 2
    win = jnp.concatenate(
        [w_in[:, :Q_LORA + KV_LORA], wk_pe, z(D_MODEL, LANES - D_ROPE),
         -wk_pe[:, half:], wk_pe[:, :half], z(D_MODEL, LANES - D_ROPE)], axis=1)
    wq3 = w_q_up.reshape(Q_LORA, N_HEADS, D_NOPE + D_ROPE)
    nope, x1, x2 = wq3[..., :D_NOPE], wq3[..., D_NOPE:D_NOPE + half], wq3[..., D_NOPE + half:]
    zpad = z(Q_LORA, N_HEADS, HEAD_PAD - D_NOPE - D_ROPE)
    wq = jnp.concatenate([nope, x1, x2, zpad], axis=-1).reshape(Q_LORA, N_HEADS * HEAD_PAD)
    wqs = jnp.concatenate([z(Q_LORA, N_HEADS, D_NOPE), -x2, x1, zpad], axis=-1)
    wqs = wqs.reshape(Q_LORA, N_HEADS * HEAD_PAD)
    wkv3 = w_kv_up.reshape(KV_LORA, N_HEADS, D_NOPE + D_V)
    w_uk, w_uv = wkv3[..., :D_NOPE], wkv3[..., D_NOPE:]
    wk = jnp.concatenate([w_uk, z(KV_LORA, N_HEADS, HEAD_PAD - D_NOPE)], axis=-1)
    wk = wk.reshape(KV_LORA, N_HEADS * HEAD_PAD)
    wv = w_uv.reshape(KV_LORA, N_HEADS * D_V).T
    wuk_ext = jnp.zeros((N_HEADS, HEAD_PAD, QEXT), F32)
    wuk_ext = wuk_ext.at[:, :D_NOPE, :KV_LORA].set(jnp.transpose(w_uk, (1, 2, 0)))
    wuk_ext = wuk_ext.at[:, D_NOPE:D_NOPE + D_ROPE, KV_LORA:KV_LORA + D_ROPE].set(
        jnp.broadcast_to(jnp.eye(D_ROPE, dtype=F32), (N_HEADS, D_ROPE, D_ROPE)))
    uv = jnp.transpose(w_uv, (1, 0, 2)).reshape(N_HEADS // 2, 2, KV_LORA, D_V)
    wuv_pair = jnp.zeros((N_HEADS // 2, 2 * KV_LORA, 2 * D_V), F32)
    wuv_pair = wuv_pair.at[:, :KV_LORA, :D_V].set(uv[:, 0])
    wuv_pair = wuv_pair.at[:, KV_LORA:, D_V:].set(uv[:, 1])
    return dict(win=win.astype(BF16), gq=g_q.reshape(1, Q_LORA), gkv=g_kv.reshape(1, KV_LORA),
                wq=wq.astype(BF16), wqs=wqs.astype(BF16), wk=wk.astype(BF16), wv=wv.astype(BF16),
                wuk_ext=wuk_ext.astype(BF16), wuv_pair=wuv_pair.astype(BF16), wo=w_o.astype(BF16))


def _rope_tabs(pos, reps):
    half = D_ROPE // 2
    inv = ROPE_THETA ** (-jnp.arange(half, dtype=F32) / half)
    ang = pos.astype(F32)[:, None] * inv[None, :]
    cos, sin = jnp.cos(ang), jnp.sin(ang)
    n = pos.shape[0]
    one, zero = jnp.ones((n, D_NOPE), F32), jnp.zeros((n, D_NOPE), F32)
    zpad = jnp.zeros((n, HEAD_PAD - D_NOPE - D_ROPE), F32)
    cq = jnp.concatenate([one, cos, cos, zpad], axis=1)
    sq = jnp.concatenate([zero, sin, sin, zpad], axis=1)
    kpad = jnp.zeros((n, LANES - D_ROPE), F32)
    ck = jnp.concatenate([cos, cos, kpad], axis=1)
    sk = jnp.concatenate([sin, sin, kpad], axis=1)
    cq, sq = jnp.tile(cq, (reps, 2)), jnp.tile(sq, (reps, 2))
    return cq, sq, jnp.tile(ck, (reps, 1)), jnp.tile(sk, (reps, 1))


def _prep_s5(a_re, a_im, log_dt, b_re, b_im, c_re, c_im, q):
    hi = lax.Precision.HIGHEST
    dt = jnp.exp(log_dt)[:, None]
    dre, dim = dt * a_re, dt * a_im
    mag = jnp.exp(dre)
    abr, abi = mag * jnp.cos(dim), mag * jnp.sin(dim)
    den = a_re * a_re + a_im * a_im
    xr, xi = abr - 1.0, abi
    fr, fi = (xr * a_re + xi * a_im) / den, (xi * a_re - xr * a_im) / den
    bbr = fr[..., None] * b_re - fi[..., None] * b_im
    bbi = fr[..., None] * b_im + fi[..., None] * b_re
    j = jnp.arange(q + 1, dtype=F32)[:, None, None]
    pmag = jnp.exp(j * dre)
    pwr, pwi = pmag * jnp.cos(j * dim), pmag * jnp.sin(j * dim)
    pbr = pwr[..., None] * bbr - pwi[..., None] * bbi
    pbi = pwr[..., None] * bbi + pwi[..., None] * bbr
    kk = (jnp.einsum("gtp,jgps->jgst", c_re, pbr[:q], precision=hi)
          - jnp.einsum("gtp,jgps->jgst", c_im, pbi[:q], precision=hi))
    g = a_re.shape[0]
    taps = jnp.transpose(kk, (1, 2, 0, 3)).reshape(g, GROUP_SIZE, q * GROUP_SIZE)
    t = jnp.stack([jnp.pad(taps[..., :(q - k) * GROUP_SIZE], ((0, 0), (0, 0), (k * GROUP_SIZE, 0)))
                   for k in range(q)], axis=1).reshape(g, q * GROUP_SIZE, q * GROUP_SIZE)
    mre = jnp.transpose(pbr[:q][::-1], (1, 0, 3, 2)).reshape(g, q * GROUP_SIZE, STATE_DIM)
    mim = jnp.transpose(pbi[:q][::-1], (1, 0, 3, 2)).reshape(g, q * GROUP_SIZE, STATE_DIM)
    ncr = pwr[1:, :, None, :] * c_re[None] - pwi[1:, :, None, :] * c_im[None]
    nci = pwr[1:, :, None, :] * c_im[None] + pwi[1:, :, None, :] * c_re[None]
    nre = jnp.transpose(ncr, (1, 3, 0, 2)).reshape(g, STATE_DIM, q * GROUP_SIZE)
    nim = jnp.transpose(nci, (1, 3, 0, 2)).reshape(g, STATE_DIM, q * GROUP_SIZE)
    are, aim = pwr[q][:, None, :], pwi[q][:, None, :]
    return t, mre, mim, nre, nim, are, aim


def _s5_tables_body(at_ref, am_ref, an_ref, et_ref, em_ref, t_ref, m_ref, n_ref):
    width = S5_Q * LANES
    row = lax.broadcasted_iota(jnp.int32, (width, width), 0)
    col = lax.broadcasted_iota(jnp.int32, (width, width), 1)

    def expand(a_ref, e_ref, row_shift, col_shift):
        full = _dot(a_ref[...].astype(BF16), e_ref[...])
        keep = ((row >> row_shift) & (S5_BLK - 1)) == ((col >> col_shift) & (S5_BLK - 1))
        return jnp.where(keep, full, 0.0).astype(BF16)

    gs, sd = GROUP_SIZE.bit_length() - 1, STATE_DIM.bit_length() - 1
    t_ref[...] = expand(at_ref, et_ref, gs, gs)
    m_ref[...] = expand(am_ref, em_ref, gs, sd)
    n_ref[...] = expand(an_ref, et_ref, sd, gs)


def _s5_block_tables(p):
    t, mre, mim, nre, nim, are, aim = _prep_s5(*p, S5_Q)
    q = S5_Q
    width = q * LANES
    assert 2 * S5_HALF == width and q * GROUP_SIZE == LANES and 2 * STATE_DIM == LANES

    def rows_kgs(a):
        a5 = a.reshape(S5_NBLK, S5_BLK, q, GROUP_SIZE, LANES)
        return jnp.transpose(a5, (0, 2, 1, 3, 4)).reshape(S5_NBLK, width, LANES)

    at = rows_kgs(t)
    am = rows_kgs(jnp.concatenate([mre, mim], axis=-1))
    an = jnp.stack([nre.reshape(S5_NBLK, S5_BLK, STATE_DIM, LANES),
                    -nim.reshape(S5_NBLK, S5_BLK, STATE_DIM, LANES)], axis=1)
    an = an.reshape(S5_NBLK, width, LANES)
    r = jnp.arange(LANES)[:, None]
    c = jnp.arange(width)[None, :]
    et = ((r // GROUP_SIZE == c // LANES) & (r % GROUP_SIZE == c % GROUP_SIZE)).astype(BF16)
    em = ((r // STATE_DIM == c // S5_HALF) & (r % STATE_DIM == c % STATE_DIM)).astype(BF16)
    cspec = pl.BlockSpec((None, width, LANES), lambda j: (j, 0, 0))
    ospec = pl.BlockSpec((None, width, width), lambda j: (j, 0, 0))
    big = jax.ShapeDtypeStruct((S5_NBLK, width, width), BF16)
    tbig, mbig, nbig = pl.pallas_call(
        _s5_tables_body,
        grid=(S5_NBLK,),
        in_specs=[cspec, cspec, cspec, _const_spec((LANES, width)), _const_spec((LANES, width))],
        out_specs=[ospec, ospec, ospec],
        out_shape=[big, big, big],
        compiler_params=_cparams(("parallel",)),
        name="s5_tables",
    )(at, am, an, et, em)
    abig = jnp.concatenate([are.reshape(S5_NBLK, 1, S5_HALF), aim.reshape(S5_NBLK, 1, S5_HALF)], axis=2)
    return tbig, mbig, nbig, abig


def _state_to_blocks(re, im):
    nb = re.shape[0]
    f = lambda a: jnp.transpose(a.reshape(nb, S5_NBLK, S5_HALF), (1, 0, 2))
    return jnp.concatenate([f(re), f(im)], axis=2)


def _blocks_to_state(hf):
    nb = hf.shape[1]
    f = lambda a: jnp.transpose(a, (1, 0, 2)).reshape(nb, N_GROUPS, STATE_DIM)
    return f(hf[..., :S5_HALF]), f(hf[..., S5_HALF:])


def _s5_mixer(hp, hs, p, h0re, h0im):
    tabs = _s5_block_tables(p)
    zero = jnp.zeros((S5_NBLK, BATCH, 2 * S5_HALF), F32)
    yp, hfp = s5_scan(hp, tabs, zero, S5_CT, BATCH)
    ys, hfs = s5_scan(hs, tabs, _state_to_blocks(h0re, h0im), 1, DEC_BATCH)
    return (yp, ys) + _blocks_to_state(hfp) + _blocks_to_state(hfs)


def kernel(x_prompt, x_sample, cache_ckv, cache_kpe, state_ssm_re, state_ssm_im, page_table, norm_mix, norm_mlp, norm_final, mla_w_in, mla_q_norm, mla_kv_norm, mla_w_q_up, mla_w_kv_up, mla_w_o, ssm_a_re, ssm_a_im, ssm_log_dt, ssm_b_re, ssm_b_im, ssm_c_re, ssm_c_im, ssm_d, ssm_w_glu, mlp_w1, mlp_w2):
    x = jnp.concatenate([x_prompt.reshape(NP_ROWS, D_MODEL), x_sample.reshape(NS_ROWS, D_MODEL)], axis=0)
    tabs_p = _rope_tabs(jnp.arange(SEQ, dtype=jnp.int32), 1)
    tabs_s = _rope_tabs(PAST_LEN + jnp.arange(DEC_SEQ, dtype=jnp.int32), TM_PROJ // DEC_SEQ)
    cache_kpet = jnp.swapaxes(cache_kpe, 2, 3)
    w1b, w2b, wglu_b = mlp_w1.astype(BF16), mlp_w2.astype(BF16), ssm_w_glu.astype(BF16)
    p_ckv, p_kpe, p_re, p_im = [], [], [], []
    s_ckv, s_kpe, s_re, s_im = [], [], [], []
    for i in range(DEPTH):
        j = i // 2
        if i % 2 == 0:
            w = _prep_mla(mla_w_in[j], mla_q_norm[j], mla_kv_norm[j], mla_w_q_up[j],
                          mla_w_kv_up[j], mla_w_o[j])
            ckv_p, kpe_p, q_p, k_p, v_p = mla_proj(x, False, norm_mix[i], w, tabs_p)
            ckv_s, kpe_s, qext = mla_proj(x, True, norm_mix[i], w, tabs_s)
            o_p = flash_prompt(q_p, k_p, v_p)
            o_lat = attn_sample(j, page_table, qext, ckv_s, kpe_s, cache_ckv, cache_kpet)
            pre, pre_args = "oproj", (o_p, w["wo"])
            sample_mixer = functools.partial(oproj_sample, o_lat=o_lat, wuv_pair=w["wuv_pair"],
                                             wo=w["wo"])
            p_ckv.append(ckv_p.reshape(BATCH, SEQ, KV_LORA))
            p_kpe.append(kpe_p.reshape(BATCH, SEQ, D_ROPE))
            s_ckv.append(ckv_s.reshape(DEC_BATCH, DEC_SEQ, KV_LORA))
            s_kpe.append(kpe_s.reshape(DEC_BATCH, DEC_SEQ, D_ROPE))
        else:
            p = (ssm_a_re[j], ssm_a_im[j], ssm_log_dt[j], ssm_b_re[j], ssm_b_im[j],
                 ssm_c_re[j], ssm_c_im[j])
            yp, ys, pre_, pim, sre, sim = _s5_mixer(hp, hs, p, state_ssm_re[j], state_ssm_im[j])
            p_re.append(pre_); p_im.append(pim); s_re.append(sre); s_im.append(sim)
            pre, pre_args = "glu", (hp, yp, ssm_d[j], wglu_b, j)
            sample_mixer = functools.partial(glu_sample, h=hs, y=ys, d=ssm_d[j], w=wglu_b, layer=j)
        args = (i, norm_mlp[i], w1b, w2b)
        if i == DEPTH - 1:
            y_prompt = mlp(x, *args, False, pre, pre_args, "final", norm_final)
            y_sample = mlp(sample_mixer(x), *args, True, None, (), "final", norm_final)
        elif i % 2 == 0:
            x, hp = mlp(x, *args, False, pre, pre_args, "norm", norm_mix[i + 1])
            x, hs = mlp(sample_mixer(x), *args, True, None, (), "norm", norm_mix[i + 1])
            hp = hp.reshape(NP_ROWS, D_MODEL)
        else:
            x = mlp(x, *args, False, pre, pre_args)
            x = mlp(sample_mixer(x), *args, True)
    y_prompt = y_prompt.reshape(BATCH, SEQ, D_MODEL)
    y_sample = y_sample.reshape(DEC_BATCH, DEC_SEQ, D_MODEL)
    return (y_prompt, y_sample,
            jnp.stack(p_ckv), jnp.stack(p_kpe), jnp.stack(p_re), jnp.stack(p_im),
            jnp.stack(s_ckv), jnp.stack(s_kpe), jnp.stack(s_re), jnp.stack(s_im))
```
